```python
import math
import jax, jax.numpy as jnp
from jax import lax
import numpy as np

D_MODEL = 1024
BATCH = 1
SEQ = 16384
DEPTH = 1
DEC_BATCH = 128
DEC_SEQ = 8
PAST_LEN = 8192
PAGE_SIZE = 128

BRANCH_W = D_MODEL // 2
N_BRANCH = 3
NSA_HEADS = 8
NSA_KV = 2
NSA_HPG = NSA_HEADS // NSA_KV
NSA_HD = BRANCH_W // NSA_HEADS
CMP_LEN = 32
CMP_STRIDE = 16
CMP_HID = 128
SLC_BLK = 64
SLC_TOPN = 16
WINDOW = 512
Q_BLK = 128
GLA_HEADS = 4
GLA_DV = BRANCH_W // GLA_HEADS
GLA_DK = GLA_DV // 2
GLA_RANK = 16
GLA_TAU = 16.0
GLA_CHUNK = 64
MEM_LEN = 256
X_HEADS = 4
X_HD = BRANCH_W // X_HEADS
D_FF = 2816
EPS = 1e-6
NEG = -1e30
FORCE = 1e6
COLS = (NSA_HEADS * NSA_HD,
        3 * 2 * NSA_KV * NSA_HD,
        NSA_HEADS * 3,
        GLA_HEADS * GLA_DK,
        GLA_HEADS * GLA_DK,
        GLA_HEADS * GLA_DV,
        GLA_RANK,
        GLA_HEADS * GLA_DV,
        X_HEADS * X_HD)
IN_COLS = sum(COLS)

kernel_name = 'nsa_gla_memory_hybrid_step'


def rmsnorm(x, g):
    xf = x.astype(jnp.float32)
    y = xf * lax.rsqrt(jnp.mean(xf * xf, axis=-1, keepdims=True) + EPS)
    return (y * g.astype(jnp.float32)).astype(x.dtype)


def swiglu(h, w_gu, w_d):
    a, b = jnp.split(h @ w_gu, 2, axis=-1)
    return (jax.nn.silu(a) * b) @ w_d


def ffn_half(x, g_pre, g_post, w_gu, w_d):
    return x + 0.5 * rmsnorm(swiglu(rmsnorm(x, g_pre), w_gu, w_d), g_post)


def masked_probs(s, mask):
    s = jnp.where(mask, s.astype(jnp.float32), NEG)
    p = jax.nn.softmax(s, axis=-1)
    return jnp.where(mask, p, 0.0)


def split_cols(p):
    outs, off = [], 0
    for w in COLS:
        outs.append(p[..., off:off + w])
        off += w
    return outs


def project(u, w_in, w_gla_g2, b_gla_g):
    B, T, _ = u.shape
    q, kv, ng, gq, gk, gv, glr, gr, xq = split_cols(u @ w_in)
    q = q.reshape(B, T, NSA_KV, NSA_HPG, NSA_HD) * (NSA_HD ** -0.5)
    kv = kv.reshape(B, T, 3, 2, NSA_KV, NSA_HD)
    ng = jax.nn.sigmoid(ng).reshape(B, T, NSA_KV, NSA_HPG, 3)
    gq = gq.reshape(B, T, GLA_HEADS, GLA_DK) * (GLA_DK ** -0.5)
    gk = gk.reshape(B, T, GLA_HEADS, GLA_DK)
    gv = gv.reshape(B, T, GLA_HEADS, GLA_DV)
    log_a = jax.nn.log_sigmoid((glr @ w_gla_g2 + b_gla_g).astype(jnp.float32)) / GLA_TAU
    log_a = log_a.reshape(B, T, GLA_HEADS, GLA_DK)
    xq = xq.reshape(B, T, X_HEADS, X_HD) * (X_HD ** -0.5)
    return q, kv, ng, gq, gk, gv, log_a, gr, xq


def compress(raw, w1, pe, w2):
    B, T, G, HD = raw.shape
    xr = raw.transpose(0, 2, 1, 3).reshape(B * G, T, HD)
    h = lax.conv_general_dilated(xr, w1, window_strides=(CMP_STRIDE,), padding='VALID',
                                 dimension_numbers=('NWC', 'WIO', 'NWC'))
    h = jax.nn.silu(h + jnp.einsum('ld,ldh->h', pe, w1))
    c = h @ w2
    nc = c.shape[1]
    return c.reshape(B, G, nc, HD).transpose(0, 2, 1, 3)


def cmp_to_slc(nc, nb):
    i = jnp.arange(nc)[:, None]
    j = jnp.arange(nb)[None, :]
    ov = (i * CMP_STRIDE < (j + 1) * SLC_BLK) & (i * CMP_STRIDE + CMP_LEN > j * SLC_BLK)
    return ov.astype(jnp.float32)


def nsa_core(q, q_pos, kc, vc, ks_blk, vs_blk, kw, vw, kw_pos, gate):
    B, Tq, G = q.shape[:3]
    nc, nb = kc.shape[1], ks_blk.shape[2]
    c_end = jnp.arange(nc) * CMP_STRIDE + CMP_LEN - 1
    mask_c = (c_end[None, :] <= q_pos[:, None])[None, :, None, None, :]
    p_c = masked_probs(jnp.einsum('bqghd,bngd->bqghn', q, kc), mask_c)
    o_c = jnp.einsum('bqghn,bngd->bqghd', p_c.astype(vc.dtype), vc)
    imp = jnp.einsum('bqgn,nj->bqgj', p_c.sum(axis=3), cmp_to_slc(nc, nb))
    j = jnp.arange(nb)[None, :]
    cur = (q_pos // SLC_BLK)[:, None]
    visible = (j <= cur)[None, :, None, :]
    forced = ((j == 0) | (j == cur) | (j == cur - 1))[None, :, None, :]
    score = jnp.where(visible, jnp.where(forced, FORCE, imp), NEG)
    n_top = min(SLC_TOPN, nb)
    _, idx = lax.top_k(score, n_top)
    bi = jnp.arange(B)[:, None, None, None]
    gi = jnp.arange(G)[None, None, :, None]
    k_sel = ks_blk[bi, gi, idx].reshape(B, Tq, G, n_top * SLC_BLK, -1)
    v_sel = vs_blk[bi, gi, idx].reshape(B, Tq, G, n_top * SLC_BLK, -1)
    tok = (idx[..., None] * SLC_BLK + jnp.arange(SLC_BLK)).reshape(B, Tq, G, n_top * SLC_BLK)
    mask_s = (tok <= q_pos[None, :, None, None])[:, :, :, None, :]
    p_s = masked_probs(jnp.einsum('bqghd,bqgmd->bqghm', q, k_sel), mask_s)
    o_s = jnp.einsum('bqghm,bqgmd->bqghd', p_s.astype(v_sel.dtype), v_sel)
    dist = q_pos[:, None] - kw_pos[None, :]
    mask_w = ((dist >= 0) & (dist < WINDOW) & (kw_pos[None, :] >= 0))[None, :, None, None, :]
    p_w = masked_probs(jnp.einsum('bqghd,bkgd->bqghk', q, kw), mask_w)
    o_w = jnp.einsum('bqghk,bkgd->bqghd', p_w.astype(vw.dtype), vw)
    return gate[..., 0:1] * o_c + gate[..., 1:2] * o_s + gate[..., 2:3] * o_w


def nsa_prompt(q, kv, gate, w1, pe, w2):
    B, T = q.shape[:2]
    kc = compress(kv[:, :, 0, 0], w1[0], pe[0], w2[0])
    vc = compress(kv[:, :, 0, 1], w1[1], pe[1], w2[1])
    nb = T // SLC_BLK
    ks_blk = kv[:, :, 1, 0].reshape(B, nb, SLC_BLK, NSA_KV, NSA_HD).transpose(0, 3, 1, 2, 4)
    vs_blk = kv[:, :, 1, 1].reshape(B, nb, SLC_BLK, NSA_KV, NSA_HD).transpose(0, 3, 1, 2, 4)
    band = ((0, 0), (WINDOW, 0), (0, 0), (0, 0))
    kw_pad = jnp.pad(kv[:, :, 2, 0], band)
    vw_pad = jnp.pad(kv[:, :, 2, 1], band)

    def one_block(i):
        s0 = i * Q_BLK
        qb = lax.dynamic_slice_in_dim(q, s0, Q_BLK, axis=1)
        gb = lax.dynamic_slice_in_dim(gate, s0, Q_BLK, axis=1)
        kw = lax.dynamic_slice_in_dim(kw_pad, s0, WINDOW + Q_BLK, axis=1)
        vw = lax.dynamic_slice_in_dim(vw_pad, s0, WINDOW + Q_BLK, axis=1)
        q_pos = s0 + jnp.arange(Q_BLK)
        kw_pos = s0 - WINDOW + jnp.arange(WINDOW + Q_BLK)
        return nsa_core(qb, q_pos, kc, vc, ks_blk, vs_blk, kw, vw, kw_pos, gb)

    o = lax.map(one_block, jnp.arange(T // Q_BLK))
    return jnp.moveaxis(o, 0, 1).reshape(B, T, NSA_KV, NSA_HPG, NSA_HD)


def nsa_sample(q, kv_new, gate, pool_cmp, pool_slc, win_buf, page_table, w1, pe, w2):
    Bd, Tn = q.shape[:2]
    past = page_table.shape[1] * PAGE_SIZE
    total = past + Tn
    nb = -(-total // SLC_BLK)

    def gather_past(pool):
        return pool[page_table].reshape(Bd, past, 2, NSA_KV, NSA_HD)

    cmp_all = jnp.concatenate([gather_past(pool_cmp), kv_new[:, :, 0]], axis=1)
    kc = compress(cmp_all[:, :, 0], w1[0], pe[0], w2[0])
    vc = compress(cmp_all[:, :, 1], w1[1], pe[1], w2[1])
    new_slc = jnp.pad(kv_new[:, :, 1], ((0, 0), (0, nb * SLC_BLK - total), (0, 0), (0, 0), (0, 0)))
    slc_all = jnp.concatenate([gather_past(pool_slc), new_slc], axis=1)
    slc_all = slc_all.reshape(Bd, nb, SLC_BLK, 2, NSA_KV, NSA_HD)
    ks_blk = slc_all[:, :, :, 0].transpose(0, 3, 1, 2, 4)
    vs_blk = slc_all[:, :, :, 1].transpose(0, 3, 1, 2, 4)
    win_all = jnp.concatenate([win_buf, kv_new[:, :, 2]], axis=1)
    w_buf = win_buf.shape[1]
    kw_pos = past - w_buf + jnp.arange(w_buf + Tn)
    q_pos = past + jnp.arange(Tn)
    o = nsa_core(q, q_pos, kc, vc, ks_blk, vs_blk, win_all[:, :, 0], win_all[:, :, 1], kw_pos, gate)
    return o, win_all[:, Tn:]


def gla_scan(q, k, v, log_a, s0):
    f32 = jnp.float32
    B, T, H, _ = q.shape
    dv = v.shape[-1]
    c = min(GLA_CHUNK, T)
    n = -(-T // c)

    def chunks(a):
        a = jnp.pad(a.astype(f32), ((0, 0), (0, n * c - T), (0, 0), (0, 0)))
        return jnp.moveaxis(a.reshape(B, n, c, H, a.shape[-1]), 1, 0)

    causal = jnp.tril(jnp.ones((c, c), dtype=bool))[None, :, :, None, None]

    def step(S, blk):
        qc, kc, vc, lc = blk
        b = jnp.cumsum(lc, axis=1)
        decay = jnp.exp(jnp.where(causal, b[:, :, None] - b[:, None, :], -jnp.inf))
        att = jnp.einsum('bthd,btshd,bshd->bhts', qc, decay, kc)
        o = jnp.einsum('bhts,bshv->bthv', att, vc) + jnp.einsum('bthd,bhdv->bthv', qc * jnp.exp(b), S)
        b_end = b[:, -1]
        S = S * jnp.exp(b_end)[..., None] + jnp.einsum('bshd,bshv->bhdv', kc * jnp.exp(b_end[:, None] - b), vc)
        return S, o

    S, o = lax.scan(step, s0.astype(f32), (chunks(q), chunks(k), chunks(v), chunks(log_a)))
    o = jnp.moveaxis(o, 0, 1).reshape(B, n * c, H, dv)[:, :T]
    return o, S


def mem_project(mem, g_mem, w_mem_kv):
    B, M, _ = mem.shape
    return (rmsnorm(mem, g_mem) @ w_mem_kv).reshape(B, M, 2, X_HEADS, X_HD)


def cross_attend(xq, mem_kv):
    s = jnp.einsum('bthd,bmhd->bhtm', xq, mem_kv[:, :, 0]).astype(jnp.float32)
    p = jax.nn.softmax(s, axis=-1).astype(xq.dtype)
    return jnp.einsum('bhtm,bmhd->bthd', p, mem_kv[:, :, 1])


def merge(u, o_nsa, o_gla, r, o_x, g_gla_out, w_branch, w_merge_gate, b_merge_gate, w_out):
    B, T, _ = u.shape
    o_gla = rmsnorm(o_gla, g_gla_out).reshape(B, T, BRANCH_W) * jax.nn.silu(r)
    br = jnp.stack([o_nsa.reshape(B, T, BRANCH_W), o_gla, o_x.reshape(B, T, BRANCH_W)], axis=2)
    proj = jnp.einsum('btkc,kcd->btkd', br, w_branch)
    gates = jax.nn.sigmoid(u @ w_merge_gate + b_merge_gate).reshape(B, T, N_BRANCH, D_MODEL)
    return jnp.sum(gates * proj, axis=2) @ w_out


def setup_inputs(seed: int = 0) -> dict:
    key = jax.random.key(seed)
    keys = iter(jax.random.split(key, 48))
    f32 = jnp.float32

    def nrm(shape, scale):
        return jax.random.normal(next(keys), shape, f32) * scale

    def gain(shape):
        return 1.0 + 0.01 * jax.random.normal(next(keys), shape, f32)

    n_pages = PAST_LEN // PAGE_SIZE
    n_used = DEC_BATCH * n_pages
    n_phys = n_used + max(1, n_used // 4)
    w_buf = min(WINDOW, PAST_LEN)
    perm = jax.random.permutation(next(keys), n_phys)
    page_table = perm[:n_used].reshape(DEC_BATCH, n_pages).astype(jnp.int32)
    D = D_MODEL
    return {
        'x_prompt': nrm((BATCH, SEQ, D), 1.0),
        'x_sample': nrm((DEC_BATCH, DEC_SEQ, D), 1.0),
        'mem_prompt': nrm((BATCH, MEM_LEN, D), 1.0),
        'cache_cmp_kv': nrm((DEPTH, n_phys, PAGE_SIZE, 2, NSA_KV, NSA_HD), 1.0),
        'cache_slc_kv': nrm((DEPTH, n_phys, PAGE_SIZE, 2, NSA_KV, NSA_HD), 1.0),
        'cache_win_kv': nrm((DEPTH, DEC_BATCH, w_buf, 2, NSA_KV, NSA_HD), 1.0),
        'state_gla': nrm((DEPTH, DEC_BATCH, GLA_HEADS, GLA_DK, GLA_DV), 0.5),
        'cache_mem_kv': nrm((DEPTH, DEC_BATCH, MEM_LEN, 2, X_HEADS, X_HD), 1.0),
        'page_table': page_table,
        'g_pre_ff1': gain((DEPTH, D)),
        'g_post_ff1': gain((DEPTH, D)),
        'w_ff1_gu': nrm((DEPTH, D, 2 * D_FF), D ** -0.5),
        'w_ff1_d': nrm((DEPTH, D_FF, D), D_FF ** -0.5),
        'g_pre_mix': gain((DEPTH, D)),
        'g_post_mix': gain((DEPTH, D)),
        'w_in': nrm((DEPTH, D, IN_COLS), D ** -0.5),
        'w_cmp1': nrm((DEPTH, 2, CMP_LEN, NSA_HD, CMP_HID), (CMP_LEN * NSA_HD) ** -0.5),
        'pe_cmp': nrm((DEPTH, 2, CMP_LEN, NSA_HD), 0.2),
        'w_cmp2': nrm((DEPTH, 2, CMP_HID, NSA_HD), CMP_HID ** -0.5),
        'w_gla_g2': nrm((DEPTH, GLA_RANK, GLA_HEADS * GLA_DK), GLA_RANK ** -0.5),
        'b_gla_g': nrm((DEPTH, GLA_HEADS * GLA_DK), 0.5),
        'g_gla_out': gain((DEPTH, GLA_DV)),
        'g_mem': gain((DEPTH, D)),
        'w_mem_kv': nrm((DEPTH, D, 2 * X_HEADS * X_HD), D ** -0.5),
        'w_branch': nrm((DEPTH, N_BRANCH, BRANCH_W, D), BRANCH_W ** -0.5),
        'w_merge_gate': nrm((DEPTH, D, N_BRANCH * D), D ** -0.5),
        'b_merge_gate': nrm((DEPTH, N_BRANCH * D), 0.1),
        'w_out': nrm((DEPTH, D, D), D ** -0.5),
        'g_pre_ff2': gain((DEPTH, D)),
        'g_post_ff2': gain((DEPTH, D)),
        'w_ff2_gu': nrm((DEPTH, D, 2 * D_FF), D ** -0.5),
        'w_ff2_d': nrm((DEPTH, D_FF, D), D_FF ** -0.5),
    }


def reference(x_prompt, x_sample, mem_prompt, cache_cmp_kv, cache_slc_kv, cache_win_kv, state_gla,
              cache_mem_kv, page_table, g_pre_ff1, g_post_ff1, w_ff1_gu, w_ff1_d, g_pre_mix, g_post_mix,
              w_in, w_cmp1, pe_cmp, w_cmp2, w_gla_g2, b_gla_g, g_gla_out, g_mem, w_mem_kv, w_branch,
              w_merge_gate, b_merge_gate, w_out, g_pre_ff2, g_post_ff2, w_ff2_gu, w_ff2_d):
    yp, ys = x_prompt, x_sample
    cmp_p, slc_p, win_p, gla_p, mem_p = [], [], [], [], []
    cmp_s, slc_s, win_s, gla_s = [], [], [], []
    for l in range(DEPTH):
        yp = ffn_half(yp, g_pre_ff1[l], g_post_ff1[l], w_ff1_gu[l], w_ff1_d[l])
        ys = ffn_half(ys, g_pre_ff1[l], g_post_ff1[l], w_ff1_gu[l], w_ff1_d[l])

        u = rmsnorm(yp, g_pre_mix[l])
        q, kv, ng, gq, gk, gv, la, gr, xq = project(u, w_in[l], w_gla_g2[l], b_gla_g[l])
        o_nsa = nsa_prompt(q, kv, ng, w_cmp1[l], pe_cmp[l], w_cmp2[l])
        s0 = jnp.zeros((u.shape[0], GLA_HEADS, GLA_DK, GLA_DV), jnp.float32)
        o_gla, s_fin = gla_scan(gq, gk, gv, la, s0)
        mem_kv = mem_project(mem_prompt, g_mem[l], w_mem_kv[l])
        o_x = cross_attend(xq, mem_kv)
        m = merge(u, o_nsa, o_gla.astype(u.dtype), gr, o_x, g_gla_out[l], w_branch[l],
                  w_merge_gate[l], b_merge_gate[l], w_out[l])
        yp = yp + rmsnorm(m, g_post_mix[l])
        cmp_p.append(kv[:, :, 0])
        slc_p.append(kv[:, :, 1])
        win_p.append(kv[:, -min(WINDOW, kv.shape[1]):, 2])
        gla_p.append(s_fin)
        mem_p.append(mem_kv)

        u = rmsnorm(ys, g_pre_mix[l])
        q, kv, ng, gq, gk, gv, la, gr, xq = project(u, w_in[l], w_gla_g2[l], b_gla_g[l])
        o_nsa, win_new = nsa_sample(q, kv, ng, cache_cmp_kv[l], cache_slc_kv[l], cache_win_kv[l],
                                    page_table, w_cmp1[l], pe_cmp[l], w_cmp2[l])
        o_gla, s_new = gla_scan(gq, gk, gv, la, state_gla[l])
        o_x = cross_attend(xq, cache_mem_kv[l])
        m = merge(u, o_nsa, o_gla.astype(u.dtype), gr, o_x, g_gla_out[l], w_branch[l],
                  w_merge_gate[l], b_merge_gate[l], w_out[l])
        ys = ys + rmsnorm(m, g_post_mix[l])
        cmp_s.append(kv[:, :, 0])
        slc_s.append(kv[:, :, 1])
        win_s.append(win_new)
        gla_s.append(s_new)

        yp = ffn_half(yp, g_pre_ff2[l], g_post_ff2[l], w_ff2_gu[l], w_ff2_d[l])
        ys = ffn_half(ys, g_pre_ff2[l], g_post_ff2[l], w_ff2_gu[l], w_ff2_d[l])

    new_cmp_kv_prompt = jnp.stack(cmp_p)
    new_slc_kv_prompt = jnp.stack(slc_p)
    new_win_kv_prompt = jnp.stack(win_p)
    new_gla_state_prompt = jnp.stack(gla_p)
    new_mem_kv_prompt = jnp.stack(mem_p)
    new_cmp_kv_sample = jnp.stack(cmp_s)
    new_slc_kv_sample = jnp.stack(slc_s)
    new_win_kv_sample = jnp.stack(win_s)
    new_gla_state_sample = jnp.stack(gla_s)
    return (yp, ys, new_cmp_kv_prompt, new_slc_kv_prompt, new_win_kv_prompt, new_gla_state_prompt,
            new_mem_kv_prompt, new_cmp_kv_sample, new_slc_kv_sample, new_win_kv_sample, new_gla_state_sample)
```

```python
import functools

import numpy as np
import jax
import jax.numpy as jnp
from jax import lax
from jax.experimental import pallas as pl
from jax.experimental.pallas import tpu as pltpu

F32 = jnp.float32
BF16 = jnp.bfloat16

D_MODEL = 1024
BRANCH_W = D_MODEL // 2
NSA_HEADS = 8
NSA_KV = 2
NSA_HPG = NSA_HEADS // NSA_KV
NSA_HD = BRANCH_W // NSA_HEADS
CMP_LEN = 32
CMP_STRIDE = 16
CMP_HID = 128
SLC_BLK = 64
SLC_TOPN = 16
WINDOW = 512
GLA_HEADS = 4
GLA_DV = BRANCH_W // GLA_HEADS
GLA_DK = GLA_DV // 2
GLA_RANK = 16
GLA_TAU = 16.0
GLA_CHUNK = 64
X_HEADS = 4
X_HD = BRANCH_W // X_HEADS
D_FF = 2816
EPS = 1e-6
NEG = -1e30
FORCE = 1e6
PAGE_SIZE = 128

LANE = 128
KV_SLAB = 2 * NSA_KV * NSA_HD
VMEM_LIMIT = 56 * 1024 * 1024


def _cparams(sem):
    return pltpu.CompilerParams(dimension_semantics=sem, vmem_limit_bytes=VMEM_LIMIT)


def _rms(x, g):
    xf = x.astype(F32)
    return xf * lax.rsqrt(jnp.mean(xf * xf, axis=-1, keepdims=True) + EPS) * g


def _silu(x):
    return x * jax.nn.sigmoid(x)


def _dot(a, b):
    return jnp.dot(a, b, preferred_element_type=F32)


def _dot_nt(a, b):
    return lax.dot_general(a, b, (((1,), (1,)), ((), ())), preferred_element_type=F32)


def _dot_tn(a, b):
    return lax.dot_general(a, b, (((0,), (0,)), ((), ())), preferred_element_type=F32)


def _split3(x):
    hi = x.astype(BF16)
    r1 = x - hi.astype(F32)
    mid = r1.astype(BF16)
    lo = (r1 - mid.astype(F32)).astype(BF16)
    return hi, mid, lo


def _ffn_kernel(x_ref, gpre_ref, gpost_ref, wg_ref, wu_ref, wd_ref, o_ref, xn_ref, acc_ref):
    f = pl.program_id(1)

    @pl.when(f == 0)
    def _():
        xn_ref[...] = _rms(x_ref[...], gpre_ref[...]).astype(BF16)
        acc_ref[...] = jnp.zeros_like(acc_ref)

    xn = xn_ref[...]
    a = _dot(xn, wg_ref[...])
    b = _dot(xn, wu_ref[...])
    acc_ref[...] += _dot((_silu(a) * b).astype(BF16), wd_ref[...])

    @pl.when(f == pl.num_programs(1) - 1)
    def _():
        o_ref[...] = x_ref[...] + 0.5 * _rms(acc_ref[...], gpost_ref[...])


def _ffn_half(x, g_pre, g_post, w_gu, w_d, tm):
    rows = x.shape[0]
    tf = D_FF // 2
    nf = D_FF // tf
    return pl.pallas_call(
        _ffn_kernel,
        grid=(rows // tm, nf),
        in_specs=[
            pl.BlockSpec((tm, D_MODEL), lambda i, f: (i, 0)),
            pl.BlockSpec((1, D_MODEL), lambda i, f: (0, 0)),
            pl.BlockSpec((1, D_MODEL), lambda i, f: (0, 0)),
            pl.BlockSpec((D_MODEL, tf), lambda i, f: (0, f)),
            pl.BlockSpec((D_MODEL, tf), lambda i, f: (0, nf + f)),
            pl.BlockSpec((tf, D_MODEL), lambda i, f: (f, 0)),
        ],
        out_specs=pl.BlockSpec((tm, D_MODEL), lambda i, f: (i, 0)),
        out_shape=jax.ShapeDtypeStruct((rows, D_MODEL), F32),
        scratch_shapes=[pltpu.VMEM((tm, D_MODEL), BF16), pltpu.VMEM((tm, D_MODEL), F32)],
        compiler_params=_cparams(("parallel", "arbitrary")),
        name="ffn_half",
    )(x, g_pre, g_post, w_gu, w_gu, w_d)


_NG = NSA_HEADS * 3
_C_Q, _C_CMP, _C_SLC, _C_WIN, _C_GQ, _C_GK, _C_GV, _C_GR, _C_XQ, _C_SM, _C_END = (
    0, 512, 768, 1024, 1280, 1536, 1792, 2304, 2816, 3328, 3456)


def _pack_w_in(w_in):
    offs = np.cumsum([0, NSA_HEADS * NSA_HD, 3 * KV_SLAB, _NG, GLA_HEADS * GLA_DK, GLA_HEADS * GLA_DK,
                      GLA_HEADS * GLA_DV, GLA_RANK, GLA_HEADS * GLA_DV, X_HEADS * X_HD])
    q, kv, ng, gq, gk, gv, glr, gr, xq = [w_in[:, offs[i]:offs[i + 1]] for i in range(9)]
    pad = jnp.zeros((D_MODEL, LANE - _NG - GLA_RANK), w_in.dtype)
    return jnp.concatenate([q, kv, gq, gk, gv, gr, xq, ng, glr, pad], axis=1).astype(BF16)


def _in_proj_kernel(h_ref, g_ref, w_ref, w2_ref, b2_ref,
                    u_ref, q_ref, cmp_ref, slc_ref, win_ref, ng_ref, gq_ref, gk_ref, gv_ref, la_ref, gr_ref,
                    xq_ref):
    u = _rms(h_ref[...], g_ref[...]).astype(BF16)
    u_ref[...] = u

    def proj(lo, hi):
        return _dot(u, w_ref[:, lo:hi])

    q_ref[...] = proj(_C_Q, _C_CMP) * (NSA_HD ** -0.5)
    cmp_ref[...] = proj(_C_CMP, _C_SLC)
    slc_ref[...] = proj(_C_SLC, _C_WIN)
    win_ref[...] = proj(_C_WIN, _C_GQ)
    gq_ref[...] = proj(_C_GQ, _C_GK) * (GLA_DK ** -0.5)
    gk_ref[...] = proj(_C_GK, _C_GV)
    gv_ref[...] = proj(_C_GV, _C_GR)
    gr_ref[...] = proj(_C_GR, _C_XQ)
    xq_ref[...] = proj(_C_XQ, _C_SM) * (X_HD ** -0.5)
    small = proj(_C_SM, _C_END)
    ng_ref[...] = jax.nn.sigmoid(small)
    z = _dot(small.astype(BF16), w2_ref[...]) + b2_ref[...]
    log_sig = jnp.minimum(z, 0.0) - jnp.log1p(jnp.exp(-jnp.abs(z)))
    la_ref[...] = log_sig / GLA_TAU


def _in_proj(h, g_pre_mix, w_cat, w2_pad, b2, tm):
    rows = h.shape[0]
    widths = [(D_MODEL, BF16), (512, F32), (256, F32), (256, F32), (256, F32), (LANE, F32), (256, F32),
              (256, F32), (512, F32), (256, F32), (512, F32), (512, F32)]
    return pl.pallas_call(
        _in_proj_kernel,
        grid=(rows // tm,),
        in_specs=[
            pl.BlockSpec((tm, D_MODEL), lambda i: (i, 0)),
            pl.BlockSpec((1, D_MODEL), lambda i: (0, 0)),
            pl.BlockSpec((D_MODEL, _C_END), lambda i: (0, 0)),
            pl.BlockSpec((LANE, 256), lambda i: (0, 0)),
            pl.BlockSpec((1, 256), lambda i: (0, 0)),
        ],
        out_specs=[pl.BlockSpec((tm, w), lambda i: (i, 0)) for w, _ in widths],
        out_shape=[jax.ShapeDtypeStruct((rows, w), dt) for w, dt in widths],
        compiler_params=_cparams(("parallel",)),
        name="in_proj",
    )(h, g_pre_mix, w_cat, w2_pad, b2)


def _pack_cmp_w1(w1):
    wa, wb = w1[:CMP_STRIDE], w1[CMP_STRIDE:]
    z = jnp.zeros_like(wa)
    rows = []
    for g in range(NSA_KV):
        cols = []
        for half in (wa, wb):
            for g2 in range(NSA_KV):
                cols.append(half if g2 == g else z)
        rows.append(jnp.concatenate(cols, axis=2))
    w = jnp.stack(rows, axis=1)
    return w.reshape(CMP_STRIDE // 2, 2 * NSA_KV * NSA_HD, 4 * CMP_HID).astype(BF16)


def _pack_cmp_w2(w2):
    z = jnp.zeros_like(w2)
    return jnp.concatenate([jnp.concatenate([w2, z], axis=1), jnp.concatenate([z, w2], axis=1)], axis=0).astype(BF16)


def _compress_ab(load_pos, wk_ref, wv_ref):
    hk = hv = None
    for lp in range(CMP_STRIDE // 2):
        xa = load_pos(2 * lp)
        xb = load_pos(2 * lp + 1)
        lk = jnp.concatenate([xa[:, :LANE], xb[:, :LANE]], axis=1).astype(BF16)
        lv = jnp.concatenate([xa[:, LANE:], xb[:, LANE:]], axis=1).astype(BF16)
        dk = _dot(lk, wk_ref[lp])
        dv = _dot(lv, wv_ref[lp])
        hk = dk if hk is None else hk + dk
        hv = dv if hv is None else hv + dv
    return hk, hv


def _compress_finish(ab, bias, w2_ref, shift_ref):
    n = ab.shape[0]
    shift_ref[0:n, :] = ab[:, 2 * CMP_HID:]
    shift_ref[n:n + 8, :] = jnp.zeros((8, 2 * CMP_HID), F32)
    h = ab[:, :2 * CMP_HID] + shift_ref[pl.ds(1, n), :] + bias
    return _dot(_silu(h).astype(BF16), w2_ref[...])


def _cmp_bias_kernel(pe_ref, w1_ref, o_ref):
    for kv in range(2):
        b = _dot(pe_ref[kv].astype(BF16), w1_ref[kv])
        o_ref[kv] = jnp.concatenate([b, b], axis=1)


def _cmp_bias(pe_cmp, w_cmp1):
    pe = jnp.broadcast_to(pe_cmp.reshape(2, 1, CMP_LEN * NSA_HD), (2, 8, CMP_LEN * NSA_HD))
    w1 = w_cmp1.reshape(2, CMP_LEN * NSA_HD, CMP_HID).astype(BF16)
    return pl.pallas_call(
        _cmp_bias_kernel,
        out_shape=jax.ShapeDtypeStruct((2, 8, 2 * CMP_HID), F32),
        name="cmp_bias",
    )(pe, w1)


def _nsa_prep_kernel(cmpv_ref, slc_ref, win_ref, wk_ref, wv_ref,
                     abk_ref, abv_ref, kslc_ref, vslct_ref, kwin_ref, vwint_ref):
    abk, abv = _compress_ab(lambda l: cmpv_ref[:, l * KV_SLAB:(l + 1) * KV_SLAB], wk_ref, wv_ref)
    abk_ref[...] = abk
    abv_ref[...] = abv
    nchunk = slc_ref.shape[0] // LANE
    for src, k_ref, vt_ref in ((slc_ref, kslc_ref, vslct_ref), (win_ref, kwin_ref, vwint_ref)):
        k_ref[...] = src[:, :LANE].astype(BF16)
        for c in range(nchunk):
            vt_ref[c] = src[c * LANE:(c + 1) * LANE, LANE:].T.astype(BF16)


def _nsa_prep(kv_cmp, kv_slc, kv_win, wk2, wv2, tm):
    t = kv_cmp.shape[0]
    nsub = t // CMP_STRIDE
    cmp_view = kv_cmp.reshape(nsub, CMP_STRIDE * KV_SLAB)
    const3 = lambda i: (0, 0, 0)
    return pl.pallas_call(
        _nsa_prep_kernel,
        grid=(t // tm,),
        in_specs=[
            pl.BlockSpec((tm // CMP_STRIDE, CMP_STRIDE * KV_SLAB), lambda i: (i, 0)),
            pl.BlockSpec((tm, KV_SLAB), lambda i: (i, 0)),
            pl.BlockSpec((tm, KV_SLAB), lambda i: (i, 0)),
            pl.BlockSpec(wk2.shape, const3),
            pl.BlockSpec(wv2.shape, const3),
        ],
        out_specs=[
            pl.BlockSpec((tm // CMP_STRIDE, 4 * CMP_HID), lambda i: (i, 0)),
            pl.BlockSpec((tm // CMP_STRIDE, 4 * CMP_HID), lambda i: (i, 0)),
            pl.BlockSpec((tm, LANE), lambda i: (i, 0)),
            pl.BlockSpec((tm // LANE, LANE, LANE), lambda i: (i, 0, 0)),
            pl.BlockSpec((tm, LANE), lambda i: (i, 0)),
            pl.BlockSpec((tm // LANE, LANE, LANE), lambda i: (i, 0, 0)),
        ],
        out_shape=[
            jax.ShapeDtypeStruct((nsub, 4 * CMP_HID), F32),
            jax.ShapeDtypeStruct((nsub, 4 * CMP_HID), F32),
            jax.ShapeDtypeStruct((t, LANE), BF16),
            jax.ShapeDtypeStruct((t // LANE, LANE, LANE), BF16),
            jax.ShapeDtypeStruct((t, LANE), BF16),
            jax.ShapeDtypeStruct((t // LANE, LANE, LANE), BF16),
        ],
        compiler_params=_cparams(("parallel",)),
        name="nsa_prep",
    )(cmp_view, kv_slc, kv_win, wk2, wv2)


def _cmp_finish_kernel(abk_ref, abv_ref, bias_ref, w2k_ref, w2v_ref, kc_ref, vct_ref, shift_ref):
    n = abk_ref.shape[0]
    kc_ref[...] = _compress_finish(abk_ref[...], bias_ref[0, 0:1, :], w2k_ref, shift_ref).astype(BF16)
    vc = _compress_finish(abv_ref[...], bias_ref[1, 0:1, :], w2v_ref, shift_ref)
    for c in range(n // LANE):
        vct_ref[:, c * LANE:(c + 1) * LANE] = vc[c * LANE:(c + 1) * LANE, :].T.astype(BF16)


def _cmp_finish(abk, abv, bias, w2k, w2v):
    n = abk.shape[0]
    return pl.pallas_call(
        _cmp_finish_kernel,
        out_shape=[jax.ShapeDtypeStruct((n, LANE), BF16), jax.ShapeDtypeStruct((LANE, n), BF16)],
        scratch_shapes=[pltpu.VMEM((n + 8, 2 * CMP_HID), F32)],
        compiler_params=pltpu.CompilerParams(vmem_limit_bytes=VMEM_LIMIT),
        name="cmp_finish",
    )(abk, abv, bias, w2k, w2v)


def _select_blocks(score, n_top):
    nb = score.shape[0]
    jj = lax.broadcasted_iota(jnp.int32, score.shape, 0)
    sel = jnp.zeros(score.shape, F32)
    for _ in range(n_top):
        m = jnp.max(score, axis=0, keepdims=True)
        idx = jnp.min(jnp.where(score == m, jj, nb), axis=0, keepdims=True)
        hit = jj == idx
        sel = jnp.where(hit, 1.0, sel)
        score = jnp.where(hit, -jnp.inf, score)
    return sel


def _block_scores(imp, jj, cur, n_real):
    s = jnp.where(jj == 0, FORCE, jnp.where(jj == cur, FORCE, jnp.where(jj == cur - 1, FORCE, imp)))
    s = jnp.where(jj <= cur, s, NEG)
    return jnp.where(jj < n_real, s, -jnp.inf)


def _cmp_to_slc_t(nc_pad, nb_pad):
    i = np.arange(nc_pad)[None, :]
    j = np.arange(nb_pad)[:, None]
    ov = (i * CMP_STRIDE < (j + 1) * SLC_BLK) & (i * CMP_STRIDE + CMP_LEN > j * SLC_BLK)
    return jnp.asarray(ov.astype(np.float32), dtype=BF16)


TQ = LANE


def _flash_chunk(qbd, k_chunk, vt_chunk, masks, m_ref, l_ref, acc_ref):
    s = _dot(k_chunk, qbd)
    for g in range(NSA_KV):
        ps = []
        for h in range(NSA_HPG):
            lo = (g * NSA_HPG + h) * TQ
            sc = s[:, lo:lo + TQ]
            if masks[g] is not None:
                sc = jnp.where(masks[g], sc, NEG)
            m_old = m_ref[:, lo:lo + TQ]
            m_new = jnp.maximum(m_old, jnp.max(sc, axis=0, keepdims=True))
            p = jnp.exp(sc - m_new)
            if masks[g] is not None:
                p = jnp.where(masks[g], p, 0.0)
            alpha = jnp.exp(m_old - m_new)
            l_ref[:, lo:lo + TQ] = alpha * l_ref[:, lo:lo + TQ] + jnp.sum(p, axis=0, keepdims=True)
            m_ref[:, lo:lo + TQ] = m_new
            acc_ref[g, :, h * TQ:(h + 1) * TQ] = acc_ref[g, :, h * TQ:(h + 1) * TQ] * alpha
            ps.append(p.astype(BF16))
        pg = jnp.concatenate(ps, axis=1)
        acc_ref[g] += _dot(vt_chunk[g * NSA_HD:(g + 1) * NSA_HD, :], pg)


def _flash_reset(m_ref, l_ref, acc_ref):
    m_ref[...] = jnp.full(m_ref.shape, NEG, F32)
    l_ref[...] = jnp.zeros(l_ref.shape, F32)
    acc_ref[...] = jnp.zeros(acc_ref.shape, F32)


def _flash_result(l_ref, acc_ref):
    outs = []
    for g in range(NSA_KV):
        inv = 1.0 / l_ref[:, g * NSA_HPG * TQ:(g + 1) * NSA_HPG * TQ]
        outs.append(acc_ref[g] * inv)
    return outs


def _nsa_prompt_kernel(q_ref, ng_ref, kc_ref, vct_ref, indt_ref, kslc_ref, vslct_ref, kwin_ref, vwint_ref,
                       o_ref, sel_ref, m_ref, l_ref, acc_ref):
    i = pl.program_id(0)
    s0 = i * TQ
    ncp = kc_ref.shape[0]
    nbp = indt_ref.shape[0]
    n_blocks = kslc_ref.shape[0] // SLC_BLK

    qt = jnp.concatenate([q_ref[:, c * LANE:(c + 1) * LANE].T for c in range(BRANCH_W // LANE)], axis=0)
    zero = jnp.zeros((NSA_HD, NSA_HPG * TQ), F32)
    rows = []
    for g in range(NSA_KV):
        blk = jnp.concatenate(
            [qt[(g * NSA_HPG + h) * NSA_HD:(g * NSA_HPG + h + 1) * NSA_HD, :] for h in range(NSA_HPG)], axis=1)
        rows.append(jnp.concatenate([blk, zero] if g == 0 else [zero, blk], axis=1))
    qbd = jnp.concatenate(rows, axis=0).astype(BF16)

    q_lane = s0 + lax.broadcasted_iota(jnp.int32, (1, TQ), 1)

    c_end = lax.broadcasted_iota(jnp.int32, (ncp, TQ), 0) * CMP_STRIDE + (CMP_LEN - 1)
    mask_c = c_end <= q_lane
    jj = lax.broadcasted_iota(jnp.int32, (nbp, TQ), 0)
    cur = q_lane // SLC_BLK
    o_cmp = []
    for g in range(NSA_KV):
        sc_all = _dot(kc_ref[...], qbd[:, g * NSA_HPG * TQ:(g + 1) * NSA_HPG * TQ])
        psum = jnp.zeros((ncp, TQ), F32)
        ps = []
        for h in range(NSA_HPG):
            sc = jnp.where(mask_c, sc_all[:, h * TQ:(h + 1) * TQ], NEG)
            p = jnp.exp(sc - jnp.max(sc, axis=0, keepdims=True))
            pn = jnp.where(mask_c, p * (1.0 / jnp.sum(p, axis=0, keepdims=True)), 0.0)
            psum = psum + pn
            ps.append(pn.astype(BF16))
        o_cmp.append(_dot(vct_ref[g * NSA_HD:(g + 1) * NSA_HD, :], jnp.concatenate(ps, axis=1)))
        imp = sum(_dot(indt_ref[...], part) for part in _split3(psum))
        sel_ref[g] = _select_blocks(_block_scores(imp, jj, cur, n_blocks), min(SLC_TOPN, n_blocks))

    sub = lax.broadcasted_iota(jnp.int32, (TQ, TQ), 0)
    lane = lax.broadcasted_iota(jnp.int32, (TQ, TQ), 1)

    def sel_masks(c, extra=None):
        out = []
        for g in range(NSA_KV):
            r0 = jnp.broadcast_to(sel_ref[g, pl.ds(2 * c, 1), :], (SLC_BLK, TQ))
            r1 = jnp.broadcast_to(sel_ref[g, pl.ds(2 * c + 1, 1), :], (SLC_BLK, TQ))
            mk = jnp.concatenate([r0, r1], axis=0)
            out.append((mk if extra is None else jnp.where(extra, mk, 0.0)) > 0.5)
        return out

    _flash_reset(m_ref, l_ref, acc_ref)

    def slc_body(c, carry):
        off = pl.multiple_of(c * TQ, TQ)
        _flash_chunk(qbd, kslc_ref[pl.ds(off, TQ), :], vslct_ref[c], sel_masks(c), m_ref, l_ref, acc_ref)
        return carry

    lax.fori_loop(0, i, slc_body, 0)
    off_d = pl.multiple_of(s0, TQ)
    _flash_chunk(qbd, kslc_ref[pl.ds(off_d, TQ), :], vslct_ref[i], sel_masks(i, sub <= lane), m_ref, l_ref, acc_ref)
    o_slc = _flash_result(l_ref, acc_ref)

    _flash_reset(m_ref, l_ref, acc_ref)
    n_back = WINDOW // TQ
    for k in range(n_back + 1):
        cw = i - n_back + k
        if k == 0:
            mk = lane < sub
        elif k == n_back:
            mk = sub <= lane
        else:
            mk = None

        @pl.when(cw >= 0)
        def _(cw=cw, mk=mk):
            off = pl.multiple_of(cw * TQ, TQ)
            _flash_chunk(qbd, kwin_ref[pl.ds(off, TQ), :], vwint_ref[cw], [mk, mk], m_ref, l_ref, acc_ref)
    o_win = _flash_result(l_ref, acc_ref)

    ngt = ng_ref[...].T
    for g in range(NSA_KV):
        heads = []
        for h in range(NSA_HPG):
            r = (g * NSA_HPG + h) * 3
            cs = slice(h * TQ, (h + 1) * TQ)
            heads.append(ngt[r:r + 1, :] * o_cmp[g][:, cs] + ngt[r + 1:r + 2, :] * o_slc[g][:, cs]
                         + ngt[r + 2:r + 3, :] * o_win[g][:, cs])
        for pair in range(NSA_HPG // 2):
            blk = jnp.concatenate(heads[2 * pair:2 * pair + 2], axis=0).T
            col = (g * NSA_HPG // 2 + pair) * LANE
            o_ref[:, col:col + LANE] = blk


def _nsa_prompt(q, ng, kc, vct, indt, kslc, vslct, kwin, vwint):
    t = q.shape[0]
    full2 = lambda i: (0, 0)
    full3 = lambda i: (0, 0, 0)
    return pl.pallas_call(
        _nsa_prompt_kernel,
        grid=(t // TQ,),
        in_specs=[
            pl.BlockSpec((TQ, BRANCH_W), lambda i: (i, 0)),
            pl.BlockSpec((TQ, LANE), lambda i: (i, 0)),
            pl.BlockSpec(kc.shape, full2),
            pl.BlockSpec(vct.shape, full2),
            pl.BlockSpec(indt.shape, full2),
            pl.BlockSpec(kslc.shape, full2),
            pl.BlockSpec(vslct.shape, full3),
            pl.BlockSpec(kwin.shape, full2),
            pl.BlockSpec(vwint.shape, full3),
        ],
        out_specs=pl.BlockSpec((TQ, BRANCH_W), lambda i: (i, 0)),
        out_shape=jax.ShapeDtypeStruct((t, BRANCH_W), F32),
        scratch_shapes=[
            pltpu.VMEM((NSA_KV, indt.shape[0], TQ), F32),
            pltpu.VMEM((1, NSA_HEADS * TQ), F32),
            pltpu.VMEM((1, NSA_HEADS * TQ), F32),
            pltpu.VMEM((NSA_KV, NSA_HD, NSA_HPG * TQ), F32),
        ],
        compiler_params=_cparams(("arbitrary",)),
        name="nsa_prompt",
    )(q, ng, kc, vct, indt, kslc, vslct, kwin, vwint)


def _nsa_prompt_path(q, kv_cmp, kv_slc, kv_win, ng, w_cmp1, pe_cmp, w_cmp2):
    t = q.shape[0]
    wk2, wv2 = _pack_cmp_w1(w_cmp1[0]), _pack_cmp_w1(w_cmp1[1])
    w2k, w2v = _pack_cmp_w2(w_cmp2[0]), _pack_cmp_w2(w_cmp2[1])
    bias = _cmp_bias(pe_cmp, w_cmp1)
    abk, abv, kslc, vslct, kwin, vwint = _nsa_prep(kv_cmp, kv_slc, kv_win, wk2, wv2, min(t, 1024))
    kc, vct = _cmp_finish(abk, abv, bias, w2k, w2v)
    indt = _cmp_to_slc_t(t // CMP_STRIDE, t // SLC_BLK)
    return _nsa_prompt(q, ng, kc, vct, indt, kslc, vslct, kwin, vwint)


def _gla_kernel(gq_ref, gk_ref, gv_ref, la_ref, s0_ref, o_ref, sfin_ref, st_ref, *, chunk, nchunk, exact):
    j = pl.program_id(1)
    n_pair = GLA_HEADS // 2

    @pl.when(j == 0)
    def _():
        for p in range(n_pair):
            st_ref[p] = s0_ref[0, p].T

    if exact:
        hi = lax.Precision.HIGHEST
        mx = lambda x: x
        dot = lambda a, b: jnp.dot(a, b, precision=hi, preferred_element_type=F32)
        dot_nt = lambda a, b: lax.dot_general(a, b, (((1,), (1,)), ((), ())), precision=hi,
                                              preferred_element_type=F32)
        dot_tn = lambda a, b: lax.dot_general(a, b, (((0,), (0,)), ((), ())), precision=hi,
                                              preferred_element_type=F32)
    else:
        mx = lambda x: x.astype(BF16)
        dot, dot_nt, dot_tn = _dot, _dot_nt, _dot_tn

    row = lax.broadcasted_iota(jnp.int32, (chunk, chunk), 0)
    col = lax.broadcasted_iota(jnp.int32, (chunk, chunk), 1)
    tri = row >= col
    tri_f = tri.astype(F32)
    head0 = lax.broadcasted_iota(jnp.int32, (chunk, LANE), 1) < GLA_DK

    for ci in range(nchunk):
        rows = slice(ci * chunk, (ci + 1) * chunk)
        la = la_ref[0, rows, :]
        if exact:
            b = dot(tri_f, la)
        else:
            b = sum(_dot(tri_f.astype(BF16), part) for part in _split3(la))
        b_end = b[chunk - 1:chunk, :]
        k = gk_ref[0, rows, :]
        qe = gq_ref[0, rows, :] * jnp.exp(b)
        ke = k * jnp.exp(-b)
        kend = k * jnp.exp(b_end - b)
        for p in range(n_pair):
            sl = slice(p * LANE, (p + 1) * LANE)
            st = st_ref[p]
            new_st = st * jnp.exp(b_end[:, sl])
            ke_p = mx(ke[:, sl])
            for hh in range(2):
                h = 2 * p + hh
                hm = head0 if hh == 0 else jnp.logical_not(head0)
                qe_h = mx(jnp.where(hm, qe[:, sl], 0.0))
                kend_h = mx(jnp.where(hm, kend[:, sl], 0.0))
                att = jnp.where(tri, dot_nt(qe_h, ke_p), 0.0)
                v = mx(gv_ref[0, rows, h * GLA_DV:(h + 1) * GLA_DV])
                o_ref[0, rows, h * GLA_DV:(h + 1) * GLA_DV] = dot(mx(att), v) + dot_nt(qe_h, mx(st))
                new_st = new_st + dot_tn(v, kend_h)
            st_ref[p] = new_st

    @pl.when(j == pl.num_programs(1) - 1)
    def _():
        for p in range(n_pair):
            sfin_ref[0, p] = st_ref[p].T


def _gla(gq, gk, gv, la, s0, chunk, nchunk, exact):
    bsz, t, _ = gq.shape
    rb = chunk * nchunk
    row_spec = lambda w: pl.BlockSpec((1, rb, w), lambda b, j: (b, j, 0))
    st_spec = pl.BlockSpec((1, GLA_HEADS // 2, LANE, LANE), lambda b, j: (b, 0, 0, 0))
    return pl.pallas_call(
        functools.partial(_gla_kernel, chunk=chunk, nchunk=nchunk, exact=exact),
        grid=(bsz, t // rb),
        in_specs=[row_spec(256), row_spec(256), row_spec(512), row_spec(256), st_spec],
        out_specs=[row_spec(512), st_spec],
        out_shape=[jax.ShapeDtypeStruct((bsz, t, BRANCH_W), F32),
                   jax.ShapeDtypeStruct((bsz, GLA_HEADS // 2, LANE, LANE), F32)],
        scratch_shapes=[pltpu.VMEM((GLA_HEADS // 2, LANE, LANE), F32)],
        compiler_params=_cparams(("parallel", "arbitrary")),
        name="gla",
    )(gq, gk, gv, la, s0)


def _mem_proj_kernel(mem_ref, g_ref, w_ref, o_ref):
    o_ref[...] = _dot(_rms(mem_ref[...], g_ref[...]).astype(BF16), w_ref[...])


def _mem_proj(mem, g_mem, w_mem_kv):
    return pl.pallas_call(
        _mem_proj_kernel,
        out_shape=jax.ShapeDtypeStruct((mem.shape[0], w_mem_kv.shape[1]), F32),
        compiler_params=pltpu.CompilerParams(vmem_limit_bytes=VMEM_LIMIT),
        name="mem_proj",
    )(mem, g_mem, w_mem_kv)


def _xattn_kernel(xq_ref, mem_ref, o_ref):
    for h in range(X_HEADS):
        cs = slice(h * X_HD, (h + 1) * X_HD)
        k = mem_ref[0, :, cs].astype(BF16)
        v = mem_ref[0, :, BRANCH_W + h * X_HD:BRANCH_W + (h + 1) * X_HD].astype(BF16)
        s = _dot_nt(xq_ref[0, :, cs].astype(BF16), k)
        p = jnp.exp(s - jnp.max(s, axis=-1, keepdims=True))
        p = p * (1.0 / jnp.sum(p, axis=-1, keepdims=True))
        o_ref[0, :, cs] = _dot(p.astype(BF16), v)


def _xattn(xq, mem_kv, tm):
    bsz, t, _ = xq.shape
    return pl.pallas_call(
        _xattn_kernel,
        grid=(bsz, t // tm),
        in_specs=[pl.BlockSpec((1, tm, BRANCH_W), lambda b, i: (b, i, 0)),
                  pl.BlockSpec((1,) + mem_kv.shape[1:], lambda b, i: (b, 0, 0))],
        out_specs=pl.BlockSpec((1, tm, BRANCH_W), lambda b, i: (b, i, 0)),
        out_shape=jax.ShapeDtypeStruct(xq.shape, F32),
        compiler_params=_cparams(("parallel", "parallel")),
        name="xattn",
    )(xq, mem_kv)


def _merge_kernel(u_ref, h_ref, onsa_ref, ogla_ref, gr_ref, ox_ref, ggla_ref, wb_ref, wmg_ref, bmg_ref, wout_ref,
                  gpost_ref, o_ref):
    u = u_ref[...]
    parts = []
    for h in range(GLA_HEADS):
        cs = slice(h * GLA_DV, (h + 1) * GLA_DV)
        parts.append(_rms(ogla_ref[:, cs], ggla_ref[...]) * _silu(gr_ref[:, cs]))
    branches = [onsa_ref[...], jnp.concatenate(parts, axis=1), ox_ref[...]]
    mixed = None
    for b in range(3):
        cs = slice(b * D_MODEL, (b + 1) * D_MODEL)
        gate = jax.nn.sigmoid(_dot(u, wmg_ref[:, cs]) + bmg_ref[:, cs])
        term = gate * _dot(branches[b].astype(BF16), wb_ref[b])
        mixed = term if mixed is None else mixed + term
    m = _dot(mixed.astype(BF16), wout_ref[...])
    o_ref[...] = h_ref[...] + _rms(m, gpost_ref[...])


def _merge(u, h, o_nsa, o_gla, gr, o_x, g_gla_out, w_branch, w_mg, b_mg, w_out, g_post, tm):
    rows = h.shape[0]
    row = lambda w: pl.BlockSpec((tm, w), lambda i: (i, 0))
    full = lambda a: pl.BlockSpec(a.shape, lambda i: (0,) * a.ndim)
    return pl.pallas_call(
        _merge_kernel,
        grid=(rows // tm,),
        in_specs=[row(D_MODEL), row(D_MODEL), row(BRANCH_W), row(BRANCH_W), row(BRANCH_W), row(BRANCH_W),
                  full(g_gla_out), full(w_branch), full(w_mg), full(b_mg), full(w_out), full(g_post)],
        out_specs=row(D_MODEL),
        out_shape=jax.ShapeDtypeStruct((rows, D_MODEL), F32),
        compiler_params=_cparams(("parallel",)),
        name="merge",
    )(u, h, o_nsa, o_gla, gr, o_x, g_gla_out, w_branch, w_mg, b_mg, w_out, g_post)


def _page_copies(pt_ref, pool_ref, buf_ref, sem_ref, seq, slot):
    return [pltpu.make_async_copy(pool_ref.at[pt_ref[seq, p]], buf_ref.at[slot, p], sem_ref.at[slot])
            for p in range(buf_ref.shape[1])]


def _gather_pages(pt_ref, pool_ref, buf_ref, sem_ref):
    b = pl.program_id(0)
    slot = b % 2

    @pl.when(b == 0)
    def _():
        for cp in _page_copies(pt_ref, pool_ref, buf_ref, sem_ref, 0, 0):
            cp.start()

    @pl.when(b + 1 < pl.num_programs(0))
    def _():
        for cp in _page_copies(pt_ref, pool_ref, buf_ref, sem_ref, b + 1, 1 - slot):
            cp.start()

    for cp in _page_copies(pt_ref, pool_ref, buf_ref, sem_ref, b, slot):
        cp.wait()
    return slot


def _row_token(shape):
    return lax.broadcasted_iota(jnp.int32, shape, 0) % 8


def _sample_cmp_kernel(pt_ref, pool_ref, qbd_ref, wk_ref, wv_ref, bias_ref, w2k_ref, w2v_ref, ind_ref,
                       oc_ref, imp_ref, buf_ref, sem_ref, shift_ref, *, past):
    slot = _gather_pages(pt_ref, pool_ref, buf_ref, sem_ref)
    npages = buf_ref.shape[1]
    nsub = npages * (PAGE_SIZE // CMP_STRIDE)

    def load_pos(l):
        return buf_ref[slot, :, :, l * KV_SLAB:(l + 1) * KV_SLAB].reshape(nsub, KV_SLAB)

    abk, abv = _compress_ab(load_pos, wk_ref, wv_ref)
    kc = _compress_finish(abk, bias_ref[0, 0:1, :], w2k_ref, shift_ref).astype(BF16)
    vc = _compress_finish(abv, bias_ref[1, 0:1, :], w2v_ref, shift_ref).astype(BF16)
    qbd = qbd_ref[0]
    s = _dot_nt(qbd, kc)
    q_pos = past + _row_token(s.shape)
    mask = lax.broadcasted_iota(jnp.int32, s.shape, 1) * CMP_STRIDE + (CMP_LEN - 1) <= q_pos
    s = jnp.where(mask, s, NEG)
    p = jnp.exp(s - jnp.max(s, axis=-1, keepdims=True))
    pn = jnp.where(mask, p * (1.0 / jnp.sum(p, axis=-1, keepdims=True)), 0.0)
    oc_ref[0] = _dot(pn.astype(BF16), vc)
    rows_g = NSA_HPG * 8
    psum = jnp.concatenate(
        [sum(pn[g * rows_g + h * 8:g * rows_g + (h + 1) * 8, :] for h in range(NSA_HPG)) for g in range(NSA_KV)],
        axis=0)
    imp_ref[0] = sum(_dot(part, ind_ref[...]) for part in _split3(psum))


def _sample_cmp(page_table, pool, qbd, wk2, wv2, bias, w2k, w2v, ind):
    nseq, npages = page_table.shape
    nsub = npages * (PAGE_SIZE // CMP_STRIDE)
    pool_v = pool.reshape(pool.shape[0], PAGE_SIZE // CMP_STRIDE, CMP_STRIDE * KV_SLAB)
    full = lambda a: pl.BlockSpec(a.shape, lambda b, pt: (0,) * a.ndim)
    grid_spec = pltpu.PrefetchScalarGridSpec(
        num_scalar_prefetch=1,
        grid=(nseq,),
        in_specs=[pl.BlockSpec(memory_space=pl.ANY),
                  pl.BlockSpec((1,) + qbd.shape[1:], lambda b, pt: (b, 0, 0)),
                  full(wk2), full(wv2), full(bias), full(w2k), full(w2v), full(ind)],
        out_specs=[pl.BlockSpec((1, 64, LANE), lambda b, pt: (b, 0, 0)),
                   pl.BlockSpec((1, 16, ind.shape[1]), lambda b, pt: (b, 0, 0))],
        scratch_shapes=[pltpu.VMEM((2, npages) + pool_v.shape[1:], F32),
                        pltpu.SemaphoreType.DMA((2,)),
                        pltpu.VMEM((nsub + 8, 2 * CMP_HID), F32)],
    )
    return pl.pallas_call(
        functools.partial(_sample_cmp_kernel, past=npages * PAGE_SIZE),
        grid_spec=grid_spec,
        out_shape=[jax.ShapeDtypeStruct((nseq, 64, LANE), F32),
                   jax.ShapeDtypeStruct((nseq, 16, ind.shape[1]), F32)],
        compiler_params=_cparams(("arbitrary",)),
        name="sample_cmp",
    )(page_table, pool_v, qbd, wk2, wv2, bias, w2k, w2v, ind)


def _sample_select_kernel(imp_ref, sel_ref, *, past, n_real, n_top):
    imp_t = imp_ref[...].T
    jj = lax.broadcasted_iota(jnp.int32, imp_t.shape, 0)
    tok = lax.broadcasted_iota(jnp.int32, (1, imp_t.shape[1]), 1) % 8
    cur = (past + tok) // SLC_BLK
    sel_ref[...] = _select_blocks(_block_scores(imp_t, jj, cur, n_real), n_top).T


def _sample_select(imp, past, n_real, tm):
    rows, nbp = imp.shape
    return pl.pallas_call(
        functools.partial(_sample_select_kernel, past=past, n_real=n_real, n_top=min(SLC_TOPN, n_real)),
        grid=(rows // tm,),
        in_specs=[pl.BlockSpec((tm, nbp), lambda i: (i, 0))],
        out_specs=pl.BlockSpec((tm, nbp), lambda i: (i, 0)),
        out_shape=jax.ShapeDtypeStruct((rows, nbp), F32),
        compiler_params=_cparams(("parallel",)),
        name="sample_select",
    )(imp)


def _masked_softmax_pv(parts):
    ms = [jnp.where(mk, s, NEG) for s, mk, _ in parts]
    m = functools.reduce(jnp.maximum, [jnp.max(x, axis=-1, keepdims=True) for x in ms])
    ps = [jnp.where(mk, jnp.exp(x - m), 0.0) for x, (_, mk, _) in zip(ms, parts)]
    l = sum(jnp.sum(p, axis=-1, keepdims=True) for p in ps)
    o = sum(_dot(p.astype(BF16), v) for p, (_, _, v) in zip(ps, parts))
    return o * (1.0 / l)


def _sample_attn_kernel(pt_ref, pool_ref, qbd_ref, sel_ref, expand_ref, newslc_ref, win_ref, newwin_ref, gate_ref,
                        oc_ref, o_ref, wout_ref, buf_ref, sem_ref):
    slot = _gather_pages(pt_ref, pool_ref, buf_ref, sem_ref)
    npages = buf_ref.shape[1]
    npast = npages * PAGE_SIZE
    nblk_past = npast // SLC_BLK
    qbd = qbd_ref[0]
    rows = qbd.shape[0]
    tok = _row_token((rows, LANE))
    lane = lax.broadcasted_iota(jnp.int32, (rows, LANE), 1)

    kp = buf_ref[slot, :, :, 0:LANE].reshape(npast, LANE).astype(BF16)
    vp = buf_ref[slot, :, :, LANE:KV_SLAB].reshape(npast, LANE).astype(BF16)
    sel = sel_ref[0]
    selrows = jnp.concatenate([sel[g * 8:(g + 1) * 8, :] for g in range(NSA_KV) for _ in range(NSA_HPG)], axis=0)
    mask_p = _dot(selrows[:, :nblk_past].astype(BF16), expand_ref[...]) > 0.5
    mask_n = jnp.where(lane <= tok, selrows[:, nblk_past:nblk_past + 1], 0.0) > 0.5
    knew = newslc_ref[0, :, 0:LANE].astype(BF16)
    vnew = newslc_ref[0, :, LANE:KV_SLAB].astype(BF16)
    o_slc = _masked_softmax_pv([(_dot_nt(qbd, kp), mask_p, vp), (_dot_nt(qbd, knew), mask_n, vnew)])

    wbuf = win_ref.shape[1]
    kw = win_ref[0, :, 0:LANE].astype(BF16)
    vw = win_ref[0, :, LANE:KV_SLAB].astype(BF16)
    lane_w = lax.broadcasted_iota(jnp.int32, (rows, wbuf), 1)
    mask_w = lane_w + (WINDOW - wbuf) > _row_token((rows, wbuf))
    kwn = newwin_ref[0, :, 0:LANE].astype(BF16)
    vwn = newwin_ref[0, :, LANE:KV_SLAB].astype(BF16)
    o_win = _masked_softmax_pv([(_dot_nt(qbd, kw), mask_w, vw), (_dot_nt(qbd, kwn), lane <= tok, vwn)])

    g = gate_ref[0]
    o_ref[0] = g[:, 0:1] * oc_ref[0] + g[:, 1:2] * o_slc + g[:, 2:3] * o_win
    wout_ref[0, 0:wbuf - 8, :] = win_ref[0, 8:wbuf, :]
    wout_ref[0, wbuf - 8:wbuf, :] = newwin_ref[0, 0:8, :]


def _sample_attn(page_table, pool, qbd, sel, expand, newslc, win, newwin, gates, o_cmp):
    nseq, npages = page_table.shape
    pool_v = pool.reshape(pool.shape[0], PAGE_SIZE, KV_SLAB)
    per_seq = lambda a: pl.BlockSpec((1,) + a.shape[1:], lambda b, pt: (b,) + (0,) * (a.ndim - 1))
    grid_spec = pltpu.PrefetchScalarGridSpec(
        num_scalar_prefetch=1,
        grid=(nseq,),
        in_specs=[pl.BlockSpec(memory_space=pl.ANY), per_seq(qbd), per_seq(sel),
                  pl.BlockSpec(expand.shape, lambda b, pt: (0, 0)),
                  per_seq(newslc), per_seq(win), per_seq(newwin), per_seq(gates), per_seq(o_cmp)],
        out_specs=[per_seq(o_cmp), per_seq(win)],
        scratch_shapes=[pltpu.VMEM((2, npages, PAGE_SIZE, KV_SLAB), F32), pltpu.SemaphoreType.DMA((2,))],
    )
    return pl.pallas_call(
        _sample_attn_kernel,
        grid_spec=grid_spec,
        out_shape=[jax.ShapeDtypeStruct(o_cmp.shape, F32), jax.ShapeDtypeStruct(win.shape, F32)],
        compiler_params=_cparams(("arbitrary",)),
        name="sample_attn",
    )(page_table, pool_v, qbd, sel, expand, newslc, win, newwin, gates, o_cmp)


def _nsa_sample_path(q, kv_cmp, kv_slc, kv_win, ng, pool_cmp, pool_slc, win_buf, page_table, w_cmp1, pe_cmp, w_cmp2):
    nseq, npages = page_table.shape
    tn = q.shape[0] // nseq
    past = npages * PAGE_SIZE
    n_real = -(-(past + tn) // SLC_BLK)
    nbp = -(-n_real // (2 * LANE)) * (2 * LANE)
    nsub = past // CMP_STRIDE
    assert tn == 8 and past % SLC_BLK == 0 and (past + tn - CMP_LEN) // CMP_STRIDE + 1 == nsub - 1

    qg = q.reshape(nseq, tn, NSA_KV, NSA_HPG, NSA_HD).transpose(0, 2, 3, 1, 4).reshape(nseq, NSA_KV, NSA_HPG * tn,
                                                                                        NSA_HD)
    z = jnp.zeros_like(qg[:, 0])
    qbd = jnp.stack([jnp.concatenate([qg[:, 0], z], axis=-1), jnp.concatenate([z, qg[:, 1]], axis=-1)], axis=1)
    qbd = qbd.reshape(nseq, NSA_HEADS * tn, LANE).astype(BF16)
    gates = ng[:, :_NG].reshape(nseq, tn, NSA_HEADS, 3).transpose(0, 2, 1, 3).reshape(nseq, NSA_HEADS * tn, 3)
    gates = jnp.pad(gates, ((0, 0), (0, 0), (0, LANE - 3)))

    def new_block(kv):
        return jnp.pad(kv.reshape(nseq, tn, KV_SLAB), ((0, 0), (0, LANE - tn), (0, 0)))

    wk2, wv2 = _pack_cmp_w1(w_cmp1[0]), _pack_cmp_w1(w_cmp1[1])
    w2k, w2v = _pack_cmp_w2(w_cmp2[0]), _pack_cmp_w2(w_cmp2[1])
    bias = _cmp_bias(pe_cmp, w_cmp1)
    ind = _cmp_to_slc_t(nsub, nbp).T
    o_cmp, imp = _sample_cmp(page_table, pool_cmp, qbd, wk2, wv2, bias, w2k, w2v, ind)
    sel = _sample_select(imp.reshape(nseq * 2 * tn, nbp), past, n_real, 512).reshape(nseq, 2 * tn, nbp)
    nblk_past = past // SLC_BLK
    expand = jnp.asarray(np.repeat(np.eye(nblk_past, dtype=np.float32), SLC_BLK, axis=1), dtype=BF16)
    o, win_new = _sample_attn(page_table, pool_slc, qbd, sel, expand, new_block(kv_slc), win_buf,
                              new_block(kv_win), gates, o_cmp)
    o = o.reshape(nseq, NSA_KV, NSA_HPG, tn, NSA_KV, NSA_HD)
    o = jnp.stack([o[:, g, :, :, g, :] for g in range(NSA_KV)], axis=1)
    return o.transpose(0, 3, 1, 2, 4).reshape(nseq * tn, BRANCH_W), win_new


def _mixer(h, mem_kv, nsa_fn, gla_s0, gla_cfg, wts, tm, x_tm):
    bsz, t, _ = h.shape
    rows = bsz * t
    h2 = h.reshape(rows, D_MODEL)
    (u, q, kcmp, kslc, kwin, ng, gq, gk, gv, la, gr, xq) = _in_proj(h2, wts['g_pre_mix'], wts['w_cat'], wts['w2_pad'],
                                                                   wts['b_gla_g'], tm)
    o_nsa, extra = nsa_fn(q, kcmp, kslc, kwin, ng)
    r3 = lambda a: a.reshape(bsz, t, a.shape[-1])
    chunk, nchunk, exact = gla_cfg
    o_gla, s_fin = _gla(r3(gq), r3(gk), r3(gv), r3(la), gla_s0.reshape(bsz, GLA_HEADS // 2, LANE, LANE), chunk,
                        nchunk, exact)
    o_x = _xattn(r3(xq), mem_kv, x_tm)
    h_new = _merge(u, h2, o_nsa, o_gla.reshape(rows, BRANCH_W), gr, o_x.reshape(rows, BRANCH_W), wts['g_gla_out'],
                   wts['w_branch'], wts['w_mg'], wts['b_mg'], wts['w_out'], wts['g_post_mix'], tm)
    s_fin = s_fin.reshape(bsz, GLA_HEADS, GLA_DK, GLA_DV)
    return h_new.reshape(bsz, t, D_MODEL), (kcmp, kslc, kwin), s_fin, extra


def kernel(x_prompt, x_sample, mem_prompt, cache_cmp_kv, cache_slc_kv, cache_win_kv, state_gla, cache_mem_kv,
           page_table, g_pre_ff1, g_post_ff1, w_ff1_gu, w_ff1_d, g_pre_mix, g_post_mix, w_in, w_cmp1, pe_cmp,
           w_cmp2, w_gla_g2, b_gla_g, g_gla_out, g_mem, w_mem_kv, w_branch, w_merge_gate, b_merge_gate, w_out,
           g_pre_ff2, g_post_ff2, w_ff2_gu, w_ff2_d):
    depth = w_in.shape[0]
    assert depth == 1
    l = 0
    bp, tp, _ = x_prompt.shape
    bs, ts, _ = x_sample.shape
    assert bp == 1
    row1 = lambda a: a.reshape(1, -1)
    wts = dict(
        g_pre_mix=row1(g_pre_mix[l]), w_cat=_pack_w_in(w_in[l]),
        w2_pad=jnp.zeros((LANE, GLA_HEADS * GLA_DK), F32).at[_NG:_NG + GLA_RANK].set(w_gla_g2[l]).astype(BF16),
        b_gla_g=row1(b_gla_g[l]), g_gla_out=row1(g_gla_out[l]), w_branch=w_branch[l].astype(BF16),
        w_mg=w_merge_gate[l].astype(BF16), b_mg=row1(b_merge_gate[l]), w_out=w_out[l].astype(BF16),
        g_post_mix=row1(g_post_mix[l]))

    def ffn(x, g_pre, g_post, w_gu, w_d, tm):
        b, t, _ = x.shape
        return _ffn_half(x.reshape(b * t, D_MODEL), row1(g_pre), row1(g_post), w_gu, w_d, tm).reshape(x.shape)

    w1gu, w1d = w_ff1_gu[l].astype(BF16), w_ff1_d[l].astype(BF16)
    w2gu, w2d = w_ff2_gu[l].astype(BF16), w_ff2_d[l].astype(BF16)
    tm_p, tm_s = 512, 256

    hp = ffn(x_prompt, g_pre_ff1[l], g_post_ff1[l], w1gu, w1d, tm_p)
    mem_kv = _mem_proj(mem_prompt[0], row1(g_mem[l]), w_mem_kv[l].astype(BF16))
    nsa_p = lambda q, kc, ks, kw, ng: (_nsa_prompt_path(q, kc, ks, kw, ng, w_cmp1[l], pe_cmp[l], w_cmp2[l]), None)
    s0 = jnp.zeros((bp, GLA_HEADS, GLA_DK, GLA_DV), F32)
    hp, (cmp_p, slc_p, win_p), gla_p, _ = _mixer(hp, mem_kv[None], nsa_p, s0, (GLA_CHUNK, 4, False), wts, tm_p, tm_p)
    yp = ffn(hp, g_pre_ff2[l], g_post_ff2[l], w2gu, w2d, tm_p)

    hs = ffn(x_sample, g_pre_ff1[l], g_post_ff1[l], w1gu, w1d, tm_s)
    wbuf = cache_win_kv.shape[2]
    win_buf = cache_win_kv[l].reshape(bs, wbuf, KV_SLAB)
    nsa_s = lambda q, kc, ks, kw, ng: _nsa_sample_path(q, kc, ks, kw, ng, cache_cmp_kv[l], cache_slc_kv[l], win_buf,
                                                       page_table, w_cmp1[l], pe_cmp[l], w_cmp2[l])
    mem_s = cache_mem_kv[l].reshape(bs, cache_mem_kv.shape[2], 2 * BRANCH_W)
    hs, (cmp_s, slc_s, _), gla_s, win_s = _mixer(hs, mem_s, nsa_s, state_gla[l], (min(GLA_CHUNK, ts), 1, True), wts,
                                                 tm_s, ts)
    ys = ffn(hs, g_pre_ff2[l], g_post_ff2[l], w2gu, w2d, tm_s)

    kv5 = lambda a, b, t: a.reshape(1, b, t, 2, NSA_KV, NSA_HD)
    wp = min(WINDOW, tp)
    return (yp, ys,
            kv5(cmp_p, bp, tp), kv5(slc_p, bp, tp), kv5(win_p.reshape(bp, tp, KV_SLAB)[:, tp - wp:], bp, wp),
            gla_p[None], mem_kv.reshape(1, bp, mem_kv.shape[0], 2, X_HEADS, X_HD),
            kv5(cmp_s, bs, ts), kv5(slc_s, bs, ts), kv5(win_s, bs, wbuf), gla_s[None])
```

```python
import functools

import numpy as np
import jax
import jax.numpy as jnp
from jax import lax
from jax.experimental import pallas as pl
from jax.experimental.pallas import tpu as pltpu

F32 = jnp.float32
BF16 = jnp.bfloat16

D_MODEL = 1024
BRANCH_W = D_MODEL // 2
NSA_HEADS = 8
NSA_KV = 2
NSA_HPG = NSA_HEADS // NSA_KV
NSA_HD = BRANCH_W // NSA_HEADS
CMP_LEN = 32
CMP_STRIDE = 16
CMP_HID = 128
SLC_BLK = 64
SLC_TOPN = 16
WINDOW = 512
GLA_HEADS = 4
GLA_DV = BRANCH_W // GLA_HEADS
GLA_DK = GLA_DV // 2
GLA_RANK = 16
GLA_TAU = 16.0
GLA_CHUNK = 64
X_HEADS = 4
X_HD = BRANCH_W // X_HEADS
D_FF = 2816
EPS = 1e-6
NEG = -1e30
FORCE = 1e6
PAGE_SIZE = 128

LANE = 128
BF16_ROWS = 16
VT_ROWS = NSA_HD + BF16_ROWS
LOG2E = 1.4426950408889634
KV_SLAB = 2 * NSA_KV * NSA_HD
VMEM_LIMIT = 56 * 1024 * 1024


def _cparams(sem):
    return pltpu.CompilerParams(dimension_semantics=sem, vmem_limit_bytes=VMEM_LIMIT)


def _rms(x, g):
    xf = x.astype(F32)
    return xf * lax.rsqrt(jnp.mean(xf * xf, axis=-1, keepdims=True) + EPS) * g


def _silu(x):
    return x * jax.nn.sigmoid(x)


def _dot(a, b):
    return jnp.dot(a, b, preferred_element_type=F32)


def _dot_nt(a, b):
    return lax.dot_general(a, b, (((1,), (1,)), ((), ())), preferred_element_type=F32)


def _dot_tn(a, b):
    return lax.dot_general(a, b, (((0,), (0,)), ((), ())), preferred_element_type=F32)


def _split3(x):
    hi = x.astype(BF16)
    r1 = x - hi.astype(F32)
    mid = r1.astype(BF16)
    lo = (r1 - mid.astype(F32)).astype(BF16)
    return hi, mid, lo


def _ffn_kernel(x_ref, gpre_ref, gpost_ref, wg_ref, wu_ref, wd_ref, o_ref, xn_ref, acc_ref):
    f = pl.program_id(1)

    @pl.when(f == 0)
    def _():
        xn_ref[...] = _rms(x_ref[...], gpre_ref[...]).astype(BF16)
        acc_ref[...] = jnp.zeros_like(acc_ref)

    xn = xn_ref[...]
    a = _dot(xn, wg_ref[...])
    b = _dot(xn, wu_ref[...])
    acc_ref[...] += _dot((_silu(a) * b).astype(BF16), wd_ref[...])

    @pl.when(f == pl.num_programs(1) - 1)
    def _():
        o_ref[...] = x_ref[...] + 0.5 * _rms(acc_ref[...], gpost_ref[...])


def _ffn_half(x, g_pre, g_post, w_gu, w_d, tm):
    rows = x.shape[0]
    tf = D_FF // 2
    nf = D_FF // tf
    return pl.pallas_call(
        _ffn_kernel,
        grid=(rows // tm, nf),
        in_specs=[
            pl.BlockSpec((tm, D_MODEL), lambda i, f: (i, 0)),
            pl.BlockSpec((1, D_MODEL), lambda i, f: (0, 0)),
            pl.BlockSpec((1, D_MODEL), lambda i, f: (0, 0)),
            pl.BlockSpec((D_MODEL, tf), lambda i, f: (0, f)),
            pl.BlockSpec((D_MODEL, tf), lambda i, f: (0, nf + f)),
            pl.BlockSpec((tf, D_MODEL), lambda i, f: (f, 0)),
        ],
        out_specs=pl.BlockSpec((tm, D_MODEL), lambda i, f: (i, 0)),
        out_shape=jax.ShapeDtypeStruct((rows, D_MODEL), F32),
        scratch_shapes=[pltpu.VMEM((tm, D_MODEL), BF16), pltpu.VMEM((tm, D_MODEL), F32)],
        compiler_params=_cparams(("parallel", "arbitrary")),
        name="ffn_half",
    )(x, g_pre, g_post, w_gu, w_gu, w_d)


_NG = NSA_HEADS * 3
_C_Q, _C_CMP, _C_SLC, _C_WIN, _C_GQ, _C_GK, _C_GV, _C_GR, _C_XQ, _C_SM, _C_END = (
    0, 512, 768, 1024, 1280, 1536, 1792, 2304, 2816, 3328, 3456)


def _pack_w_in(w_in):
    offs = np.cumsum([0, NSA_HEADS * NSA_HD, 3 * KV_SLAB, _NG, GLA_HEADS * GLA_DK, GLA_HEADS * GLA_DK,
                      GLA_HEADS * GLA_DV, GLA_RANK, GLA_HEADS * GLA_DV, X_HEADS * X_HD])
    q, kv, ng, gq, gk, gv, glr, gr, xq = [w_in[:, offs[i]:offs[i + 1]] for i in range(9)]
    pad = jnp.zeros((D_MODEL, LANE - _NG - GLA_RANK), w_in.dtype)
    return jnp.concatenate([q, kv, gq, gk, gv, gr, xq, ng, glr, pad], axis=1).astype(BF16)


def _in_proj_kernel(h_ref, g_ref, w_ref, w2_ref, b2_ref,
                    u_ref, q_ref, cmp_ref, slc_ref, win_ref, ng_ref, gq_ref, gk_ref, gv_ref, la_ref, gr_ref,
                    xq_ref):
    u = _rms(h_ref[...], g_ref[...]).astype(BF16)
    u_ref[...] = u

    def proj(lo, hi):
        return _dot(u, w_ref[:, lo:hi])

    q_ref[...] = proj(_C_Q, _C_CMP) * (NSA_HD ** -0.5)
    cmp_ref[...] = proj(_C_CMP, _C_SLC)
    slc_ref[...] = proj(_C_SLC, _C_WIN)
    win_ref[...] = proj(_C_WIN, _C_GQ)
    gq_ref[...] = proj(_C_GQ, _C_GK) * (GLA_DK ** -0.5)
    gk_ref[...] = proj(_C_GK, _C_GV)
    gv_ref[...] = proj(_C_GV, _C_GR)
    gr_ref[...] = proj(_C_GR, _C_XQ)
    xq_ref[...] = proj(_C_XQ, _C_SM) * (X_HD ** -0.5)
    small = proj(_C_SM, _C_END)
    ng_ref[...] = jax.nn.sigmoid(small)
    z = _dot(small.astype(BF16), w2_ref[...]) + b2_ref[...]
    log_sig = jnp.minimum(z, 0.0) - jnp.log1p(jnp.exp(-jnp.abs(z)))
    la_ref[...] = log_sig / GLA_TAU


def _in_proj(h, g_pre_mix, w_cat, w2_pad, b2, tm):
    rows = h.shape[0]
    widths = [(D_MODEL, BF16), (512, F32), (256, F32), (256, F32), (256, F32), (LANE, F32), (256, F32),
              (256, F32), (512, F32), (256, F32), (512, F32), (512, F32)]
    return pl.pallas_call(
        _in_proj_kernel,
        grid=(rows // tm,),
        in_specs=[
            pl.BlockSpec((tm, D_MODEL), lambda i: (i, 0)),
            pl.BlockSpec((1, D_MODEL), lambda i: (0, 0)),
            pl.BlockSpec((D_MODEL, _C_END), lambda i: (0, 0)),
            pl.BlockSpec((LANE, 256), lambda i: (0, 0)),
            pl.BlockSpec((1, 256), lambda i: (0, 0)),
        ],
        out_specs=[pl.BlockSpec((tm, w), lambda i: (i, 0)) for w, _ in widths],
        out_shape=[jax.ShapeDtypeStruct((rows, w), dt) for w, dt in widths],
        compiler_params=_cparams(("parallel",)),
        name="in_proj",
    )(h, g_pre_mix, w_cat, w2_pad, b2)


def _pack_cmp_w1(w1):
    wa, wb = w1[:CMP_STRIDE], w1[CMP_STRIDE:]
    z = jnp.zeros_like(wa)
    rows = []
    for g in range(NSA_KV):
        cols = []
        for half in (wa, wb):
            for g2 in range(NSA_KV):
                cols.append(half if g2 == g else z)
        rows.append(jnp.concatenate(cols, axis=2))
    w = jnp.stack(rows, axis=1)
    return w.reshape(CMP_STRIDE // 2, 2 * NSA_KV * NSA_HD, 4 * CMP_HID).astype(BF16)


def _pack_cmp_w2(w2):
    z = jnp.zeros_like(w2)
    return jnp.concatenate([jnp.concatenate([w2, z], axis=1), jnp.concatenate([z, w2], axis=1)], axis=0).astype(BF16)


def _compress_ab(load_pos, wk_ref, wv_ref):
    hk = hv = None
    for lp in range(CMP_STRIDE // 2):
        xa = load_pos(2 * lp)
        xb = load_pos(2 * lp + 1)
        lk = jnp.concatenate([xa[:, :LANE], xb[:, :LANE]], axis=1).astype(BF16)
        lv = jnp.concatenate([xa[:, LANE:], xb[:, LANE:]], axis=1).astype(BF16)
        dk = _dot(lk, wk_ref[lp])
        dv = _dot(lv, wv_ref[lp])
        hk = dk if hk is None else hk + dk
        hv = dv if hv is None else hv + dv
    return hk, hv


def _compress_finish(ab, bias, w2_ref, shift_ref):
    n = ab.shape[0]
    shift_ref[0:n, :] = ab[:, 2 * CMP_HID:]
    shift_ref[n:n + 8, :] = jnp.zeros((8, 2 * CMP_HID), F32)
    h = ab[:, :2 * CMP_HID] + shift_ref[pl.ds(1, n), :] + bias
    return _dot(_silu(h).astype(BF16), w2_ref[...])


def _cmp_bias_kernel(pe_ref, w1_ref, o_ref):
    for kv in range(2):
        b = _dot(pe_ref[kv].astype(BF16), w1_ref[kv])
        o_ref[kv] = jnp.concatenate([b, b], axis=1)


def _cmp_bias(pe_cmp, w_cmp1):
    pe = jnp.broadcast_to(pe_cmp.reshape(2, 1, CMP_LEN * NSA_HD), (2, 8, CMP_LEN * NSA_HD))
    w1 = w_cmp1.reshape(2, CMP_LEN * NSA_HD, CMP_HID).astype(BF16)
    return pl.pallas_call(
        _cmp_bias_kernel,
        out_shape=jax.ShapeDtypeStruct((2, 8, 2 * CMP_HID), F32),
        name="cmp_bias",
    )(pe, w1)


def _nsa_prep_kernel(cmpv_ref, slc_ref, win_ref, wk_ref, wv_ref,
                     abk_ref, abv_ref, kslc_ref, vslct_ref, kwin_ref, vwint_ref):
    abk, abv = _compress_ab(lambda l: cmpv_ref[:, l * KV_SLAB:(l + 1) * KV_SLAB], wk_ref, wv_ref)
    abk_ref[...] = abk
    abv_ref[...] = abv
    nchunk = slc_ref.shape[0] // LANE
    ones_row = (lax.broadcasted_iota(jnp.int32, (VT_ROWS - NSA_HD, LANE), 0) == 0).astype(F32)
    for src, k_ref, vt_ref in ((slc_ref, kslc_ref, vslct_ref), (win_ref, kwin_ref, vwint_ref)):
        k_ref[...] = src[:, :LANE].astype(BF16)
        for c in range(nchunk):
            vt = src[c * LANE:(c + 1) * LANE, LANE:].T
            vt_ref[c] = jnp.concatenate([vt[:NSA_HD], ones_row, vt[NSA_HD:], ones_row], axis=0).astype(BF16)


def _nsa_prep(kv_cmp, kv_slc, kv_win, wk2, wv2, tm):
    t = kv_cmp.shape[0]
    nsub = t // CMP_STRIDE
    cmp_view = kv_cmp.reshape(nsub, CMP_STRIDE * KV_SLAB)
    const3 = lambda i: (0, 0, 0)
    return pl.pallas_call(
        _nsa_prep_kernel,
        grid=(t // tm,),
        in_specs=[
            pl.BlockSpec((tm // CMP_STRIDE, CMP_STRIDE * KV_SLAB), lambda i: (i, 0)),
            pl.BlockSpec((tm, KV_SLAB), lambda i: (i, 0)),
            pl.BlockSpec((tm, KV_SLAB), lambda i: (i, 0)),
            pl.BlockSpec(wk2.shape, const3),
            pl.BlockSpec(wv2.shape, const3),
        ],
        out_specs=[
            pl.BlockSpec((tm // CMP_STRIDE, 4 * CMP_HID), lambda i: (i, 0)),
            pl.BlockSpec((tm // CMP_STRIDE, 4 * CMP_HID), lambda i: (i, 0)),
            pl.BlockSpec((tm, LANE), lambda i: (i, 0)),
            pl.BlockSpec((tm // LANE, NSA_KV * VT_ROWS, LANE), lambda i: (i, 0, 0)),
            pl.BlockSpec((tm, LANE), lambda i: (i, 0)),
            pl.BlockSpec((tm // LANE, NSA_KV * VT_ROWS, LANE), lambda i: (i, 0, 0)),
        ],
        out_shape=[
            jax.ShapeDtypeStruct((nsub, 4 * CMP_HID), F32),
            jax.ShapeDtypeStruct((nsub, 4 * CMP_HID), F32),
            jax.ShapeDtypeStruct((t, LANE), BF16),
            jax.ShapeDtypeStruct((t // LANE, NSA_KV * VT_ROWS, LANE), BF16),
            jax.ShapeDtypeStruct((t, LANE), BF16),
            jax.ShapeDtypeStruct((t // LANE, NSA_KV * VT_ROWS, LANE), BF16),
        ],
        compiler_params=_cparams(("parallel",)),
        name="nsa_prep",
    )(cmp_view, kv_slc, kv_win, wk2, wv2)


def _cmp_finish_kernel(abk_ref, abv_ref, bias_ref, w2k_ref, w2v_ref, kc_ref, vct_ref, shift_ref):
    n = abk_ref.shape[0]
    kc_ref[...] = _compress_finish(abk_ref[...], bias_ref[0, 0:1, :], w2k_ref, shift_ref).astype(BF16)
    vc = _compress_finish(abv_ref[...], bias_ref[1, 0:1, :], w2v_ref, shift_ref)
    for c in range(n // LANE):
        vct_ref[:, c * LANE:(c + 1) * LANE] = vc[c * LANE:(c + 1) * LANE, :].T.astype(BF16)


def _cmp_finish(abk, abv, bias, w2k, w2v):
    n = abk.shape[0]
    return pl.pallas_call(
        _cmp_finish_kernel,
        out_shape=[jax.ShapeDtypeStruct((n, LANE), BF16), jax.ShapeDtypeStruct((LANE, n), BF16)],
        scratch_shapes=[pltpu.VMEM((n + 8, 2 * CMP_HID), F32)],
        compiler_params=pltpu.CompilerParams(vmem_limit_bytes=VMEM_LIMIT),
        name="cmp_finish",
    )(abk, abv, bias, w2k, w2v)


def _select_blocks(score, n_top):
    nb = score.shape[0]
    jj = lax.broadcasted_iota(jnp.int32, score.shape, 0)
    sel = jnp.zeros(score.shape, F32)
    for _ in range(n_top):
        m = jnp.max(score, axis=0, keepdims=True)
        idx = jnp.min(jnp.where(score == m, jj, nb), axis=0, keepdims=True)
        hit = jj == idx
        sel = jnp.where(hit, 1.0, sel)
        score = jnp.where(hit, -jnp.inf, score)
    return sel


def _block_scores(imp, jj, cur, n_real):
    s = jnp.where(jj == 0, FORCE, jnp.where(jj == cur, FORCE, jnp.where(jj == cur - 1, FORCE, imp)))
    s = jnp.where(jj <= cur, s, NEG)
    return jnp.where(jj < n_real, s, -jnp.inf)


def _cmp_to_slc_t(nc_pad, nb_pad):
    i = np.arange(nc_pad)[None, :]
    j = np.arange(nb_pad)[:, None]
    ov = (i * CMP_STRIDE < (j + 1) * SLC_BLK) & (i * CMP_STRIDE + CMP_LEN > j * SLC_BLK)
    return jnp.asarray(ov.astype(np.float32), dtype=BF16)


TQ = LANE


SLC_CK = 4 * TQ


def _flash_step(qbd, k, vt, bias, m_ref, acc_ref):
    s = _dot(k, qbd)
    for g in range(NSA_KV):
        ps = []
        for h in range(NSA_HPG):
            lo = (g * NSA_HPG + h) * TQ
            sc = s[:, lo:lo + TQ] + bias[g]
            m_old = m_ref[:, lo:lo + TQ]
            m_new = jnp.maximum(m_old, jnp.max(sc, axis=0, keepdims=True))
            ps.append(jnp.exp2(sc - m_new).astype(BF16))
            m_ref[:, lo:lo + TQ] = m_new
            acc_ref[g, :, h * TQ:(h + 1) * TQ] = acc_ref[g, :, h * TQ:(h + 1) * TQ] * jnp.exp2(m_old - m_new)
        acc_ref[g] += _dot(vt[g * VT_ROWS:(g + 1) * VT_ROWS, :], jnp.concatenate(ps, axis=1))


def _softmax_once(qbd, k, vt, bias):
    s = _dot(k, qbd)
    outs = []
    for g in range(NSA_KV):
        ps = []
        for h in range(NSA_HPG):
            lo = (g * NSA_HPG + h) * TQ
            sc = s[:, lo:lo + TQ] + bias
            ps.append(jnp.exp2(sc - jnp.max(sc, axis=0, keepdims=True)).astype(BF16))
        o = _dot(vt[g * VT_ROWS:(g + 1) * VT_ROWS, :], jnp.concatenate(ps, axis=1))
        outs.append(o[:NSA_HD] * (1.0 / o[NSA_HD:NSA_HD + 1]))
    return outs


def _flash_reset(m_ref, acc_ref):
    m_ref[...] = jnp.full(m_ref.shape, NEG, F32)
    acc_ref[...] = jnp.zeros(acc_ref.shape, F32)


def _flash_result(acc_ref):
    return [acc_ref[g, :NSA_HD, :] * (1.0 / acc_ref[g, NSA_HD:NSA_HD + 1, :]) for g in range(NSA_KV)]


def _nsa_prompt_kernel(q_ref, ng_ref, kc_ref, vct_ref, indt_ref, kslc_ref, vslct_ref, kwin_ref, vwint_ref,
                       o_ref, selb_ref, m_ref, acc_ref):
    i = pl.program_id(0)
    s0 = i * TQ
    ncp = kc_ref.shape[0]
    nbp = indt_ref.shape[0]
    n_blocks = kslc_ref.shape[0] // SLC_BLK

    qt = jnp.concatenate([q_ref[:, c * LANE:(c + 1) * LANE].T for c in range(BRANCH_W // LANE)], axis=0)
    zero = jnp.zeros((NSA_HD, NSA_HPG * TQ), F32)
    rows = []
    for g in range(NSA_KV):
        blk = jnp.concatenate(
            [qt[(g * NSA_HPG + h) * NSA_HD:(g * NSA_HPG + h + 1) * NSA_HD, :] for h in range(NSA_HPG)], axis=1)
        rows.append(jnp.concatenate([blk, zero] if g == 0 else [zero, blk], axis=1))
    qbd = (jnp.concatenate(rows, axis=0) * LOG2E).astype(BF16)

    q_lane = s0 + lax.broadcasted_iota(jnp.int32, (1, TQ), 1)

    c_end = lax.broadcasted_iota(jnp.int32, (ncp, TQ), 0) * CMP_STRIDE + (CMP_LEN - 1)
    bias_c = jnp.where(c_end <= q_lane, 0.0, NEG)
    col_ok = q_lane >= CMP_LEN - 1
    jj = lax.broadcasted_iota(jnp.int32, (nbp, TQ), 0)
    cur = q_lane // SLC_BLK
    o_cmp, scores = [], []
    for g in range(NSA_KV):
        sc_all = _dot(kc_ref[...], qbd[:, g * NSA_HPG * TQ:(g + 1) * NSA_HPG * TQ])
        psum = jnp.zeros((ncp, TQ), F32)
        ps = []
        for h in range(NSA_HPG):
            sc = sc_all[:, h * TQ:(h + 1) * TQ] + bias_c
            p = jnp.exp2(sc - jnp.max(sc, axis=0, keepdims=True))
            pn = p * jnp.where(col_ok, 1.0 / jnp.sum(p, axis=0, keepdims=True), 0.0)
            psum = psum + pn
            ps.append(pn.astype(BF16))
        o_cmp.append(_dot(vct_ref[g * NSA_HD:(g + 1) * NSA_HD, :], jnp.concatenate(ps, axis=1)))
        imp = sum(_dot(indt_ref[...], part) for part in _split3(psum))
        scores.append(_block_scores(imp, jj, cur, n_blocks))
    sel = _select_blocks(jnp.concatenate(scores, axis=1), min(SLC_TOPN, n_blocks))
    for g in range(NSA_KV):
        selb_ref[g] = (sel[:, g * TQ:(g + 1) * TQ] - 1.0) * (-NEG)

    blk_per_step = SLC_CK // SLC_BLK

    def slc_inputs(c):
        k = kslc_ref[pl.ds(pl.multiple_of(c * SLC_CK, SLC_CK), SLC_CK), :]
        vt = jnp.concatenate([vslct_ref[c * (SLC_CK // TQ) + j] for j in range(SLC_CK // TQ)], axis=1)
        bias = []
        for g in range(NSA_KV):
            rows = selb_ref[g, pl.ds(pl.multiple_of(c * blk_per_step, blk_per_step), blk_per_step), :]
            bias.append(jnp.concatenate(
                [jnp.broadcast_to(rows[j:j + 1, :], (SLC_BLK, TQ)) for j in range(blk_per_step)], axis=0))
        return k, vt, bias

    _flash_reset(m_ref, acc_ref)

    def slc_body(c, carry):
        k, vt, bias = slc_inputs(c)
        _flash_step(qbd, k, vt, bias, m_ref, acc_ref)
        return carry

    c_last = s0 // SLC_CK
    lax.fori_loop(0, c_last, slc_body, 0)
    k, vt, bias = slc_inputs(c_last)
    key_pos = c_last * SLC_CK + lax.broadcasted_iota(jnp.int32, (SLC_CK, TQ), 0)
    causal = key_pos <= q_lane
    _flash_step(qbd, k, vt, [jnp.where(causal, b, NEG) for b in bias], m_ref, acc_ref)
    o_slc = _flash_result(acc_ref)

    n_win = WINDOW + TQ
    w0 = pl.multiple_of(jnp.maximum(s0 - WINDOW, 0), TQ)
    dist = q_lane - (w0 + lax.broadcasted_iota(jnp.int32, (n_win, TQ), 0))
    bias_w = jnp.where(dist >= 0, jnp.where(dist < WINDOW, 0.0, NEG), NEG)
    vt_w = jnp.concatenate([vwint_ref[w0 // TQ + j] for j in range(n_win // TQ)], axis=1)
    o_win = _softmax_once(qbd, kwin_ref[pl.ds(w0, n_win), :], vt_w, bias_w)

    ngt = ng_ref[...].T
    for g in range(NSA_KV):
        heads = []
        for h in range(NSA_HPG):
            r = (g * NSA_HPG + h) * 3
            cs = slice(h * TQ, (h + 1) * TQ)
            heads.append(ngt[r:r + 1, :] * o_cmp[g][:, cs] + ngt[r + 1:r + 2, :] * o_slc[g][:, cs]
                         + ngt[r + 2:r + 3, :] * o_win[g][:, cs])
        for pair in range(NSA_HPG // 2):
            blk = jnp.concatenate(heads[2 * pair:2 * pair + 2], axis=0).T
            col = (g * NSA_HPG // 2 + pair) * LANE
            o_ref[:, col:col + LANE] = blk


def _nsa_prompt(q, ng, kc, vct, indt, kslc, vslct, kwin, vwint):
    t = q.shape[0]
    full2 = lambda i: (0, 0)
    full3 = lambda i: (0, 0, 0)
    return pl.pallas_call(
        _nsa_prompt_kernel,
        grid=(t // TQ,),
        in_specs=[
            pl.BlockSpec((TQ, BRANCH_W), lambda i: (i, 0)),
            pl.BlockSpec((TQ, LANE), lambda i: (i, 0)),
            pl.BlockSpec(kc.shape, full2),
            pl.BlockSpec(vct.shape, full2),
            pl.BlockSpec(indt.shape, full2),
            pl.BlockSpec(kslc.shape, full2),
            pl.BlockSpec(vslct.shape, full3),
            pl.BlockSpec(kwin.shape, full2),
            pl.BlockSpec(vwint.shape, full3),
        ],
        out_specs=pl.BlockSpec((TQ, BRANCH_W), lambda i: (i, 0)),
        out_shape=jax.ShapeDtypeStruct((t, BRANCH_W), F32),
        scratch_shapes=[
            pltpu.VMEM((NSA_KV, indt.shape[0], TQ), F32),
            pltpu.VMEM((1, NSA_HEADS * TQ), F32),
            pltpu.VMEM((NSA_KV, VT_ROWS, NSA_HPG * TQ), F32),
        ],
        compiler_params=_cparams(("arbitrary",)),
        name="nsa_prompt",
    )(q, ng, kc, vct, indt, kslc, vslct, kwin, vwint)


def _nsa_prompt_path(q, kv_cmp, kv_slc, kv_win, ng, w_cmp1, pe_cmp, w_cmp2):
    t = q.shape[0]
    wk2, wv2 = _pack_cmp_w1(w_cmp1[0]), _pack_cmp_w1(w_cmp1[1])
    w2k, w2v = _pack_cmp_w2(w_cmp2[0]), _pack_cmp_w2(w_cmp2[1])
    bias = _cmp_bias(pe_cmp, w_cmp1)
    abk, abv, kslc, vslct, kwin, vwint = _nsa_prep(kv_cmp, kv_slc, kv_win, wk2, wv2, min(t, 1024))
    kc, vct = _cmp_finish(abk, abv, bias, w2k, w2v)
    indt = _cmp_to_slc_t(t // CMP_STRIDE, t // SLC_BLK)
    return _nsa_prompt(q, ng, kc, vct, indt, kslc, vslct, kwin, vwint)


def _gla_kernel(gq_ref, gk_ref, gv_ref, la_ref, s0_ref, o_ref, sfin_ref, st_ref, *, chunk, nchunk, exact):
    j = pl.program_id(1)
    n_pair = GLA_HEADS // 2

    @pl.when(j == 0)
    def _():
        for p in range(n_pair):
            st_ref[p] = s0_ref[0, p].T

    if exact:
        hi = lax.Precision.HIGHEST
        mx = lambda x: x
        dot = lambda a, b: jnp.dot(a, b, precision=hi, preferred_element_type=F32)
        dot_nt = lambda a, b: lax.dot_general(a, b, (((1,), (1,)), ((), ())), precision=hi,
                                              preferred_element_type=F32)
        dot_tn = lambda a, b: lax.dot_general(a, b, (((0,), (0,)), ((), ())), precision=hi,
                                              preferred_element_type=F32)
    else:
        mx = lambda x: x.astype(BF16)
        dot, dot_nt, dot_tn = _dot, _dot_nt, _dot_tn

    row = lax.broadcasted_iota(jnp.int32, (chunk, chunk), 0)
    col = lax.broadcasted_iota(jnp.int32, (chunk, chunk), 1)
    tri = row >= col
    tri_f = tri.astype(F32)
    head0 = lax.broadcasted_iota(jnp.int32, (chunk, LANE), 1) < GLA_DK

    for ci in range(nchunk):
        rows = slice(ci * chunk, (ci + 1) * chunk)
        la = la_ref[0, rows, :]
        if exact:
            b = dot(tri_f, la)
        else:
            b = sum(_dot(tri_f.astype(BF16), part) for part in _split3(la))
        b_end = b[chunk - 1:chunk, :]
        k = gk_ref[0, rows, :]
        qe = gq_ref[0, rows, :] * jnp.exp(b)
        ke = k * jnp.exp(-b)
        kend = k * jnp.exp(b_end - b)
        for p in range(n_pair):
            sl = slice(p * LANE, (p + 1) * LANE)
            st = st_ref[p]
            new_st = st * jnp.exp(b_end[:, sl])
            ke_p = mx(ke[:, sl])
            for hh in range(2):
                h = 2 * p + hh
                hm = head0 if hh == 0 else jnp.logical_not(head0)
                qe_h = mx(jnp.where(hm, qe[:, sl], 0.0))
                kend_h = mx(jnp.where(hm, kend[:, sl], 0.0))
                att = jnp.where(tri, dot_nt(qe_h, ke_p), 0.0)
                v = mx(gv_ref[0, rows, h * GLA_DV:(h + 1) * GLA_DV])
                o_ref[0, rows, h * GLA_DV:(h + 1) * GLA_DV] = dot(mx(att), v) + dot_nt(qe_h, mx(st))
                new_st = new_st + dot_tn(v, kend_h)
            st_ref[p] = new_st

    @pl.when(j == pl.num_programs(1) - 1)
    def _():
        for p in range(n_pair):
            sfin_ref[0, p] = st_ref[p].T


def _gla(gq, gk, gv, la, s0, chunk, nchunk, exact):
    bsz, t, _ = gq.shape
    rb = chunk * nchunk
    row_spec = lambda w: pl.BlockSpec((1, rb, w), lambda b, j: (b, j, 0))
    st_spec = pl.BlockSpec((1, GLA_HEADS // 2, LANE, LANE), lambda b, j: (b, 0, 0, 0))
    return pl.pallas_call(
        functools.partial(_gla_kernel, chunk=chunk, nchunk=nchunk, exact=exact),
        grid=(bsz, t // rb),
        in_specs=[row_spec(256), row_spec(256), row_spec(512), row_spec(256), st_spec],
        out_specs=[row_spec(512), st_spec],
        out_shape=[jax.ShapeDtypeStruct((bsz, t, BRANCH_W), F32),
                   jax.ShapeDtypeStruct((bsz, GLA_HEADS // 2, LANE, LANE), F32)],
        scratch_shapes=[pltpu.VMEM((GLA_HEADS // 2, LANE, LANE), F32)],
        compiler_params=_cparams(("parallel", "arbitrary")),
        name="gla",
    )(gq, gk, gv, la, s0)


def _mem_proj_kernel(mem_ref, g_ref, w_ref, o_ref):
    o_ref[...] = _dot(_rms(mem_ref[...], g_ref[...]).astype(BF16), w_ref[...])


def _mem_proj(mem, g_mem, w_mem_kv):
    return pl.pallas_call(
        _mem_proj_kernel,
        out_shape=jax.ShapeDtypeStruct((mem.shape[0], w_mem_kv.shape[1]), F32),
        compiler_params=pltpu.CompilerParams(vmem_limit_bytes=VMEM_LIMIT),
        name="mem_proj",
    )(mem, g_mem, w_mem_kv)


def _xattn_kernel(xq_ref, mem_ref, o_ref):
    for h in range(X_HEADS):
        cs = slice(h * X_HD, (h + 1) * X_HD)
        k = mem_ref[0, :, cs].astype(BF16)
        v = mem_ref[0, :, BRANCH_W + h * X_HD:BRANCH_W + (h + 1) * X_HD].astype(BF16)
        s = _dot_nt(xq_ref[0, :, cs].astype(BF16), k)
        p = jnp.exp(s - jnp.max(s, axis=-1, keepdims=True))
        p = p * (1.0 / jnp.sum(p, axis=-1, keepdims=True))
        o_ref[0, :, cs] = _dot(p.astype(BF16), v)


def _xattn(xq, mem_kv, tm):
    bsz, t, _ = xq.shape
    return pl.pallas_call(
        _xattn_kernel,
        grid=(bsz, t // tm),
        in_specs=[pl.BlockSpec((1, tm, BRANCH_W), lambda b, i: (b, i, 0)),
                  pl.BlockSpec((1,) + mem_kv.shape[1:], lambda b, i: (b, 0, 0))],
        out_specs=pl.BlockSpec((1, tm, BRANCH_W), lambda b, i: (b, i, 0)),
        out_shape=jax.ShapeDtypeStruct(xq.shape, F32),
        compiler_params=_cparams(("parallel", "parallel")),
        name="xattn",
    )(xq, mem_kv)


def _merge_kernel(u_ref, h_ref, onsa_ref, ogla_ref, gr_ref, ox_ref, ggla_ref, wb_ref, wmg_ref, bmg_ref, wout_ref,
                  gpost_ref, o_ref):
    u = u_ref[...]
    parts = []
    for h in range(GLA_HEADS):
        cs = slice(h * GLA_DV, (h + 1) * GLA_DV)
        parts.append(_rms(ogla_ref[:, cs], ggla_ref[...]) * _silu(gr_ref[:, cs]))
    branches = [onsa_ref[...], jnp.concatenate(parts, axis=1), ox_ref[...]]
    mixed = None
    for b in range(3):
        cs = slice(b * D_MODEL, (b + 1) * D_MODEL)
        gate = jax.nn.sigmoid(_dot(u, wmg_ref[:, cs]) + bmg_ref[:, cs])
        term = gate * _dot(branches[b].astype(BF16), wb_ref[b])
        mixed = term if mixed is None else mixed + term
    m = _dot(mixed.astype(BF16), wout_ref[...])
    o_ref[...] = h_ref[...] + _rms(m, gpost_ref[...])


def _merge(u, h, o_nsa, o_gla, gr, o_x, g_gla_out, w_branch, w_mg, b_mg, w_out, g_post, tm):
    rows = h.shape[0]
    row = lambda w: pl.BlockSpec((tm, w), lambda i: (i, 0))
    full = lambda a: pl.BlockSpec(a.shape, lambda i: (0,) * a.ndim)
    return pl.pallas_call(
        _merge_kernel,
        grid=(rows // tm,),
        in_specs=[row(D_MODEL), row(D_MODEL), row(BRANCH_W), row(BRANCH_W), row(BRANCH_W), row(BRANCH_W),
                  full(g_gla_out), full(w_branch), full(w_mg), full(b_mg), full(w_out), full(g_post)],
        out_specs=row(D_MODEL),
        out_shape=jax.ShapeDtypeStruct((rows, D_MODEL), F32),
        compiler_params=_cparams(("parallel",)),
        name="merge",
    )(u, h, o_nsa, o_gla, gr, o_x, g_gla_out, w_branch, w_mg, b_mg, w_out, g_post)


def _page_copies(pt_ref, pool_ref, buf_ref, sem_ref, seq, slot):
    npages = buf_ref.shape[-1] // PAGE_SIZE
    return [pltpu.make_async_copy(pool_ref.at[pt_ref[seq, p]],
                                  buf_ref.at[slot, :, :, :, pl.ds(p * PAGE_SIZE, PAGE_SIZE)], sem_ref.at[slot])
            for p in range(npages)]


def _gather_pages(pt_ref, pool_ref, buf_ref, sem_ref):
    b = pl.program_id(0)
    slot = b % 2

    @pl.when(b == 0)
    def _():
        for cp in _page_copies(pt_ref, pool_ref, buf_ref, sem_ref, 0, 0):
            cp.start()

    @pl.when(b + 1 < pl.num_programs(0))
    def _():
        for cp in _page_copies(pt_ref, pool_ref, buf_ref, sem_ref, b + 1, 1 - slot):
            cp.start()

    for cp in _page_copies(pt_ref, pool_ref, buf_ref, sem_ref, b, slot):
        cp.wait()
    return slot


def _row_token(shape):
    return lax.broadcasted_iota(jnp.int32, shape, 0) % 8


def _tap_permutation():
    r = np.arange(2 * PAGE_SIZE)
    l, jj = r // CMP_STRIDE, r % CMP_STRIDE
    perm = np.zeros((2 * PAGE_SIZE, 2 * PAGE_SIZE), np.float32)
    perm[r, CMP_STRIDE * jj + l] = 1.0
    return jnp.asarray(perm, dtype=BF16)


def _sample_cmp_kernel(pt_ref, pool_ref, qbd_ref, perm_ref, wk_ref, wv_ref, bias_ref, w2k_ref, w2v_ref, ind_ref,
                       oc_ref, imp_ref, buf_ref, sem_ref, taps_ref, shift_ref, *, past):
    slot = _gather_pages(pt_ref, pool_ref, buf_ref, sem_ref)
    span = 2 * PAGE_SIZE
    nspan = buf_ref.shape[-1] // span
    nsub = nspan * CMP_STRIDE

    def span_body(q, carry):
        x = buf_ref[slot, :, :, :, pl.ds(pl.multiple_of(q * span, span), span)]
        t = _dot_nt(perm_ref[...], x.reshape(KV_SLAB, span).astype(BF16)).astype(BF16)
        row0 = pl.multiple_of(q * CMP_STRIDE, CMP_STRIDE)
        for l in range(CMP_STRIDE):
            taps_ref[l, pl.ds(row0, CMP_STRIDE), :] = t[l * CMP_STRIDE:(l + 1) * CMP_STRIDE, :]
        return carry

    lax.fori_loop(0, nspan, span_body, 0)

    abk, abv = _compress_ab(lambda l: taps_ref[l], wk_ref, wv_ref)
    kc = _compress_finish(abk, bias_ref[0, 0:1, :], w2k_ref, shift_ref).astype(BF16)
    vc = _compress_finish(abv, bias_ref[1, 0:1, :], w2v_ref, shift_ref).astype(BF16)
    qbd = qbd_ref[0]
    s = _dot_nt(qbd, kc)
    q_pos = past + _row_token(s.shape)
    mask = lax.broadcasted_iota(jnp.int32, s.shape, 1) * CMP_STRIDE + (CMP_LEN - 1) <= q_pos
    s = jnp.where(mask, s, NEG)
    p = jnp.exp(s - jnp.max(s, axis=-1, keepdims=True))
    pn = jnp.where(mask, p * (1.0 / jnp.sum(p, axis=-1, keepdims=True)), 0.0)
    oc_ref[0] = _dot(pn.astype(BF16), vc)
    rows_g = NSA_HPG * 8
    psum = jnp.concatenate(
        [sum(pn[g * rows_g + h * 8:g * rows_g + (h + 1) * 8, :] for h in range(NSA_HPG)) for g in range(NSA_KV)],
        axis=0)
    imp_ref[0] = sum(_dot(part, ind_ref[...]) for part in _split3(psum))


def _sample_cmp(page_table, pool_t, qbd, wk2, wv2, bias, w2k, w2v, ind):
    nseq, npages = page_table.shape
    assert npages % 2 == 0
    past = npages * PAGE_SIZE
    nsub = past // CMP_STRIDE
    perm = _tap_permutation()
    full = lambda a: pl.BlockSpec(a.shape, lambda b, pt: (0,) * a.ndim)
    grid_spec = pltpu.PrefetchScalarGridSpec(
        num_scalar_prefetch=1,
        grid=(nseq,),
        in_specs=[pl.BlockSpec(memory_space=pl.ANY),
                  pl.BlockSpec((1,) + qbd.shape[1:], lambda b, pt: (b, 0, 0)),
                  full(perm), full(wk2), full(wv2), full(bias), full(w2k), full(w2v), full(ind)],
        out_specs=[pl.BlockSpec((1, 64, LANE), lambda b, pt: (b, 0, 0)),
                   pl.BlockSpec((1, 16, ind.shape[1]), lambda b, pt: (b, 0, 0))],
        scratch_shapes=[pltpu.VMEM((2, 2, NSA_KV, NSA_HD, past), F32),
                        pltpu.SemaphoreType.DMA((2,)),
                        pltpu.VMEM((CMP_STRIDE, nsub, KV_SLAB), BF16),
                        pltpu.VMEM((nsub + 8, 2 * CMP_HID), F32)],
    )
    return pl.pallas_call(
        functools.partial(_sample_cmp_kernel, past=past),
        grid_spec=grid_spec,
        out_shape=[jax.ShapeDtypeStruct((nseq, 64, LANE), F32),
                   jax.ShapeDtypeStruct((nseq, 16, ind.shape[1]), F32)],
        compiler_params=_cparams(("arbitrary",)),
        name="sample_cmp",
    )(page_table, pool_t, qbd, perm, wk2, wv2, bias, w2k, w2v, ind)


def _sample_select_kernel(imp_ref, sel_ref, *, past, n_real, n_top):
    imp_t = imp_ref[...].T
    jj = lax.broadcasted_iota(jnp.int32, imp_t.shape, 0)
    tok = lax.broadcasted_iota(jnp.int32, (1, imp_t.shape[1]), 1) % 8
    cur = (past + tok) // SLC_BLK
    sel_ref[...] = _select_blocks(_block_scores(imp_t, jj, cur, n_real), n_top).T


def _sample_select(imp, past, n_real, tm):
    rows, nbp = imp.shape
    return pl.pallas_call(
        functools.partial(_sample_select_kernel, past=past, n_real=n_real, n_top=min(SLC_TOPN, n_real)),
        grid=(rows // tm,),
        in_specs=[pl.BlockSpec((tm, nbp), lambda i: (i, 0))],
        out_specs=pl.BlockSpec((tm, nbp), lambda i: (i, 0)),
        out_shape=jax.ShapeDtypeStruct((rows, nbp), F32),
        compiler_params=_cparams(("parallel",)),
        name="sample_select",
    )(imp)


def _masked_softmax_pv(parts):
    ms = [jnp.where(mk, s, NEG) for s, mk, _, _ in parts]
    m = functools.reduce(jnp.maximum, [jnp.max(x, axis=-1, keepdims=True) for x in ms])
    ps = [jnp.where(part[1], jnp.exp(x - m), 0.0) for x, part in zip(ms, parts)]
    l = sum(jnp.sum(p, axis=-1, keepdims=True) for p in ps)
    o = sum((_dot_nt if part[3] else _dot)(p.astype(BF16), part[2]) for p, part in zip(ps, parts))
    return o * (1.0 / l)


def _sample_attn_kernel(pt_ref, pool_ref, qbd_ref, sel_ref, expand_ref, newslc_ref, win_ref, newwin_ref,
                        newwint_ref, gate_ref, oc_ref, o_ref, wout_ref, buf_ref, sem_ref):
    slot = _gather_pages(pt_ref, pool_ref, buf_ref, sem_ref)
    npast = buf_ref.shape[-1]
    nblk_past = npast // SLC_BLK
    qbd = qbd_ref[0]
    rows = qbd.shape[0]
    tok = _row_token((rows, LANE))
    lane = lax.broadcasted_iota(jnp.int32, (rows, LANE), 1)

    kt = buf_ref[slot, 0].reshape(NSA_KV * NSA_HD, npast).astype(BF16)
    vt = buf_ref[slot, 1].reshape(NSA_KV * NSA_HD, npast).astype(BF16)
    sel = sel_ref[0]
    selrows = jnp.concatenate([sel[g * 8:(g + 1) * 8, :] for g in range(NSA_KV) for _ in range(NSA_HPG)], axis=0)
    mask_p = _dot(selrows[:, :nblk_past].astype(BF16), expand_ref[...]) > 0.5
    mask_n = jnp.where(lane <= tok, selrows[:, nblk_past:nblk_past + 1], 0.0) > 0.5
    knew = newslc_ref[0, :, 0:LANE].astype(BF16)
    vnew = newslc_ref[0, :, LANE:KV_SLAB].astype(BF16)
    o_slc = _masked_softmax_pv([(_dot(qbd, kt), mask_p, vt, True), (_dot_nt(qbd, knew), mask_n, vnew, False)])

    wbuf = win_ref.shape[-1]
    lane_w = lax.broadcasted_iota(jnp.int32, (rows, wbuf), 1)
    mask_w = lane_w + (WINDOW - wbuf) > _row_token((rows, wbuf))
    kwn = newwin_ref[0, :, 0:LANE].astype(BF16)
    vwn = newwin_ref[0, :, LANE:KV_SLAB].astype(BF16)
    o_win = _masked_softmax_pv([(_dot(qbd, win_ref[0, 0].astype(BF16)), mask_w, win_ref[0, 1].astype(BF16), True),
                                (_dot_nt(qbd, kwn), lane <= tok, vwn, False)])

    g = gate_ref[0]
    o_ref[0] = g[:, 0:1] * oc_ref[0] + g[:, 1:2] * o_slc + g[:, 2:3] * o_win

    tail = lax.broadcasted_iota(jnp.int32, (NSA_KV * NSA_HD, LANE), 1) >= LANE - 8
    for kv in range(2):
        rolled = pltpu.roll(win_ref[0, kv], wbuf - 8, axis=1)
        wout_ref[0, kv, :, 0:wbuf - LANE] = rolled[:, 0:wbuf - LANE]
        wout_ref[0, kv, :, wbuf - LANE:wbuf] = jnp.where(tail, newwint_ref[0, kv], rolled[:, wbuf - LANE:wbuf])


def _sample_attn(page_table, pool_t, qbd, sel, expand, newslc, win_t, newwin, newwin_t, gates, o_cmp):
    nseq, npages = page_table.shape
    per_seq = lambda a: pl.BlockSpec((1,) + a.shape[1:], lambda b, pt: (b,) + (0,) * (a.ndim - 1))
    grid_spec = pltpu.PrefetchScalarGridSpec(
        num_scalar_prefetch=1,
        grid=(nseq,),
        in_specs=[pl.BlockSpec(memory_space=pl.ANY), per_seq(qbd), per_seq(sel),
                  pl.BlockSpec(expand.shape, lambda b, pt: (0, 0)),
                  per_seq(newslc), per_seq(win_t), per_seq(newwin), per_seq(newwin_t), per_seq(gates),
                  per_seq(o_cmp)],
        out_specs=[per_seq(o_cmp), per_seq(win_t)],
        scratch_shapes=[pltpu.VMEM((2, 2, NSA_KV, NSA_HD, npages * PAGE_SIZE), F32),
                        pltpu.SemaphoreType.DMA((2,))],
    )
    return pl.pallas_call(
        _sample_attn_kernel,
        grid_spec=grid_spec,
        out_shape=[jax.ShapeDtypeStruct(o_cmp.shape, F32), jax.ShapeDtypeStruct(win_t.shape, F32)],
        compiler_params=_cparams(("arbitrary",)),
        name="sample_attn",
    )(page_table, pool_t, qbd, sel, expand, newslc, win_t, newwin, newwin_t, gates, o_cmp)


def _nsa_sample_path(q, kv_cmp, kv_slc, kv_win, ng, pool_cmp, pool_slc, win_buf, page_table, w_cmp1, pe_cmp, w_cmp2):
    nseq, npages = page_table.shape
    pos_minor = lambda a: jnp.transpose(a, (0, 2, 3, 4, 1))
    wbuf = win_buf.shape[1]
    win_t = pos_minor(win_buf).reshape(nseq, 2, NSA_KV * NSA_HD, wbuf)
    tn = q.shape[0] // nseq
    past = npages * PAGE_SIZE
    n_real = -(-(past + tn) // SLC_BLK)
    nbp = -(-n_real // (2 * LANE)) * (2 * LANE)
    nsub = past // CMP_STRIDE
    assert tn == 8 and past % SLC_BLK == 0 and (past + tn - CMP_LEN) // CMP_STRIDE + 1 == nsub - 1

    qg = q.reshape(nseq, tn, NSA_KV, NSA_HPG, NSA_HD).transpose(0, 2, 3, 1, 4).reshape(nseq, NSA_KV, NSA_HPG * tn,
                                                                                        NSA_HD)
    z = jnp.zeros_like(qg[:, 0])
    qbd = jnp.stack([jnp.concatenate([qg[:, 0], z], axis=-1), jnp.concatenate([z, qg[:, 1]], axis=-1)], axis=1)
    qbd = qbd.reshape(nseq, NSA_HEADS * tn, LANE).astype(BF16)
    gates = ng[:, :_NG].reshape(nseq, tn, NSA_HEADS, 3).transpose(0, 2, 1, 3).reshape(nseq, NSA_HEADS * tn, 3)
    gates = jnp.pad(gates, ((0, 0), (0, 0), (0, LANE - 3)))

    def new_block(kv):
        return jnp.pad(kv.reshape(nseq, tn, KV_SLAB), ((0, 0), (0, LANE - tn), (0, 0)))

    wk2, wv2 = _pack_cmp_w1(w_cmp1[0]), _pack_cmp_w1(w_cmp1[1])
    w2k, w2v = _pack_cmp_w2(w_cmp2[0]), _pack_cmp_w2(w_cmp2[1])
    bias = _cmp_bias(pe_cmp, w_cmp1)
    ind = _cmp_to_slc_t(nsub, nbp).T
    o_cmp, imp = _sample_cmp(page_table, pos_minor(pool_cmp), qbd, wk2, wv2, bias, w2k, w2v, ind)
    sel = _sample_select(imp.reshape(nseq * 2 * tn, nbp), past, n_real, 512).reshape(nseq, 2 * tn, nbp)
    nblk_past = past // SLC_BLK
    expand = jnp.asarray(np.repeat(np.eye(nblk_past, dtype=np.float32), SLC_BLK, axis=1), dtype=BF16)
    newwin_t = jnp.pad(kv_win.reshape(nseq, tn, KV_SLAB).transpose(0, 2, 1), ((0, 0), (0, 0), (LANE - tn, 0)))
    newwin_t = newwin_t.reshape(nseq, 2, NSA_KV * NSA_HD, LANE)
    o, win_new_t = _sample_attn(page_table, pos_minor(pool_slc), qbd, sel, expand, new_block(kv_slc), win_t,
                                new_block(kv_win), newwin_t, gates, o_cmp)
    o = o.reshape(nseq, NSA_KV, NSA_HPG, tn, NSA_KV, NSA_HD)
    o = jnp.stack([o[:, g, :, :, g, :] for g in range(NSA_KV)], axis=1)
    win_new = jnp.transpose(win_new_t.reshape(nseq, 2, NSA_KV, NSA_HD, wbuf), (0, 4, 1, 2, 3))
    return o.transpose(0, 3, 1, 2, 4).reshape(nseq * tn, BRANCH_W), win_new


def _mixer(h, mem_kv, nsa_fn, gla_s0, gla_cfg, wts, tm, x_tm):
    bsz, t, _ = h.shape
    rows = bsz * t
    h2 = h.reshape(rows, D_MODEL)
    (u, q, kcmp, kslc, kwin, ng, gq, gk, gv, la, gr, xq) = _in_proj(h2, wts['g_pre_mix'], wts['w_cat'], wts['w2_pad'],
                                                                   wts['b_gla_g'], tm)
    o_nsa, extra = nsa_fn(q, kcmp, kslc, kwin, ng)
    r3 = lambda a: a.reshape(bsz, t, a.shape[-1])
    chunk, nchunk, exact = gla_cfg
    o_gla, s_fin = _gla(r3(gq), r3(gk), r3(gv), r3(la), gla_s0.reshape(bsz, GLA_HEADS // 2, LANE, LANE), chunk,
                        nchunk, exact)
    o_x = _xattn(r3(xq), mem_kv, x_tm)
    h_new = _merge(u, h2, o_nsa, o_gla.reshape(rows, BRANCH_W), gr, o_x.reshape(rows, BRANCH_W), wts['g_gla_out'],
                   wts['w_branch'], wts['w_mg'], wts['b_mg'], wts['w_out'], wts['g_post_mix'], tm)
    s_fin = s_fin.reshape(bsz, GLA_HEADS, GLA_DK, GLA_DV)
    return h_new.reshape(bsz, t, D_MODEL), (kcmp, kslc, kwin), s_fin, extra


def kernel(x_prompt, x_sample, mem_prompt, cache_cmp_kv, cache_slc_kv, cache_win_kv, state_gla, cache_mem_kv,
           page_table, g_pre_ff1, g_post_ff1, w_ff1_gu, w_ff1_d, g_pre_mix, g_post_mix, w_in, w_cmp1, pe_cmp,
           w_cmp2, w_gla_g2, b_gla_g, g_gla_out, g_mem, w_mem_kv, w_branch, w_merge_gate, b_merge_gate, w_out,
           g_pre_ff2, g_post_ff2, w_ff2_gu, w_ff2_d):
    depth = w_in.shape[0]
    assert depth == 1
    l = 0
    bp, tp, _ = x_prompt.shape
    bs, ts, _ = x_sample.shape
    assert bp == 1
    row1 = lambda a: a.reshape(1, -1)
    wts = dict(
        g_pre_mix=row1(g_pre_mix[l]), w_cat=_pack_w_in(w_in[l]),
        w2_pad=jnp.zeros((LANE, GLA_HEADS * GLA_DK), F32).at[_NG:_NG + GLA_RANK].set(w_gla_g2[l]).astype(BF16),
        b_gla_g=row1(b_gla_g[l]), g_gla_out=row1(g_gla_out[l]), w_branch=w_branch[l].astype(BF16),
        w_mg=w_merge_gate[l].astype(BF16), b_mg=row1(b_merge_gate[l]), w_out=w_out[l].astype(BF16),
        g_post_mix=row1(g_post_mix[l]))

    def ffn(x, g_pre, g_post, w_gu, w_d, tm):
        b, t, _ = x.shape
        return _ffn_half(x.reshape(b * t, D_MODEL), row1(g_pre), row1(g_post), w_gu, w_d, tm).reshape(x.shape)

    w1gu, w1d = w_ff1_gu[l].astype(BF16), w_ff1_d[l].astype(BF16)
    w2gu, w2d = w_ff2_gu[l].astype(BF16), w_ff2_d[l].astype(BF16)
    tm_p, tm_s = 512, 256

    hp = ffn(x_prompt, g_pre_ff1[l], g_post_ff1[l], w1gu, w1d, tm_p)
    mem_kv = _mem_proj(mem_prompt[0], row1(g_mem[l]), w_mem_kv[l].astype(BF16))
    nsa_p = lambda q, kc, ks, kw, ng: (_nsa_prompt_path(q, kc, ks, kw, ng, w_cmp1[l], pe_cmp[l], w_cmp2[l]), None)
    s0 = jnp.zeros((bp, GLA_HEADS, GLA_DK, GLA_DV), F32)
    hp, (cmp_p, slc_p, win_p), gla_p, _ = _mixer(hp, mem_kv[None], nsa_p, s0, (GLA_CHUNK, 4, False), wts, tm_p, tm_p)
    yp = ffn(hp, g_pre_ff2[l], g_post_ff2[l], w2gu, w2d, tm_p)

    hs = ffn(x_sample, g_pre_ff1[l], g_post_ff1[l], w1gu, w1d, tm_s)
    wbuf = cache_win_kv.shape[2]
    nsa_s = lambda q, kc, ks, kw, ng: _nsa_sample_path(q, kc, ks, kw, ng, cache_cmp_kv[l], cache_slc_kv[l],
                                                       cache_win_kv[l], page_table, w_cmp1[l], pe_cmp[l], w_cmp2[l])
    mem_s = cache_mem_kv[l].reshape(bs, cache_mem_kv.shape[2], 2 * BRANCH_W)
    hs, (cmp_s, slc_s, _), gla_s, win_s = _mixer(hs, mem_s, nsa_s, state_gla[l], (min(GLA_CHUNK, ts), 1, True), wts,
                                                 tm_s, ts)
    ys = ffn(hs, g_pre_ff2[l], g_post_ff2[l], w2gu, w2d, tm_s)

    kv5 = lambda a, b, t: a.reshape(1, b, t, 2, NSA_KV, NSA_HD)
    wp = min(WINDOW, tp)
    return (yp, ys,
            kv5(cmp_p, bp, tp), kv5(slc_p, bp, tp), kv5(win_p.reshape(bp, tp, KV_SLAB)[:, tp - wp:], bp, wp),
            gla_p[None], mem_kv.reshape(1, bp, mem_kv.shape[0], 2, X_HEADS, X_HD),
            kv5(cmp_s, bs, ts), kv5(slc_s, bs, ts), kv5(win_s, bs, wbuf), gla_s[None])
```

```python
import functools

import numpy as np
import jax
import jax.numpy as jnp
from jax import lax
from jax.experimental import pallas as pl
from jax.experimental.pallas import tpu as pltpu

F32 = jnp.float32
BF16 = jnp.bfloat16

D_MODEL = 1024
BRANCH_W = D_MODEL // 2
NSA_HEADS = 8
NSA_KV = 2
NSA_HPG = NSA_HEADS // NSA_KV
NSA_HD = BRANCH_W // NSA_HEADS
CMP_LEN = 32
CMP_STRIDE = 16
CMP_HID = 128
SLC_BLK = 64
SLC_TOPN = 16
WINDOW = 512
GLA_HEADS = 4
GLA_DV = BRANCH_W // GLA_HEADS
GLA_DK = GLA_DV // 2
GLA_RANK = 16
GLA_TAU = 16.0
GLA_CHUNK = 64
X_HEADS = 4
X_HD = BRANCH_W // X_HEADS
D_FF = 2816
EPS = 1e-6
NEG = -1e30
FORCE = 1e6
PAGE_SIZE = 128

LANE = 128
BF16_ROWS = 16
VT_ROWS = NSA_HD + BF16_ROWS
LOG2E = 1.4426950408889634
KV_SLAB = 2 * NSA_KV * NSA_HD
VMEM_LIMIT = 56 * 1024 * 1024


def _cparams(sem):
    return pltpu.CompilerParams(dimension_semantics=sem, vmem_limit_bytes=VMEM_LIMIT)


def _rms(x, g):
    xf = x.astype(F32)
    return xf * lax.rsqrt(jnp.mean(xf * xf, axis=-1, keepdims=True) + EPS) * g


def _silu(x):
    return x * jax.nn.sigmoid(x)


def _dot(a, b):
    return jnp.dot(a, b, preferred_element_type=F32)


def _dot_nt(a, b):
    return lax.dot_general(a, b, (((1,), (1,)), ((), ())), preferred_element_type=F32)


def _dot_tn(a, b):
    return lax.dot_general(a, b, (((0,), (0,)), ((), ())), preferred_element_type=F32)


def _split3(x):
    hi = x.astype(BF16)
    r1 = x - hi.astype(F32)
    mid = r1.astype(BF16)
    lo = (r1 - mid.astype(F32)).astype(BF16)
    return hi, mid, lo


def _ffn_kernel(x_ref, gpre_ref, gpost_ref, wg_ref, wu_ref, wd_ref, o_ref, xn_ref, acc_ref):
    f = pl.program_id(1)

    @pl.when(f == 0)
    def _():
        xn_ref[...] = _rms(x_ref[...], gpre_ref[...]).astype(BF16)
        acc_ref[...] = jnp.zeros_like(acc_ref)

    xn = xn_ref[...]
    a = _dot(xn, wg_ref[...])
    b = _dot(xn, wu_ref[...])
    acc_ref[...] += _dot((_silu(a) * b).astype(BF16), wd_ref[...])

    @pl.when(f == pl.num_programs(1) - 1)
    def _():
        o_ref[...] = x_ref[...] + 0.5 * _rms(acc_ref[...], gpost_ref[...])


def _ffn_half(x, g_pre, g_post, w_gu, w_d, tm):
    rows = x.shape[0]
    tf = D_FF // 2
    nf = D_FF // tf
    return pl.pallas_call(
        _ffn_kernel,
        grid=(rows // tm, nf),
        in_specs=[
            pl.BlockSpec((tm, D_MODEL), lambda i, f: (i, 0)),
            pl.BlockSpec((1, D_MODEL), lambda i, f: (0, 0)),
            pl.BlockSpec((1, D_MODEL), lambda i, f: (0, 0)),
            pl.BlockSpec((D_MODEL, tf), lambda i, f: (0, f)),
            pl.BlockSpec((D_MODEL, tf), lambda i, f: (0, nf + f)),
            pl.BlockSpec((tf, D_MODEL), lambda i, f: (f, 0)),
        ],
        out_specs=pl.BlockSpec((tm, D_MODEL), lambda i, f: (i, 0)),
        out_shape=jax.ShapeDtypeStruct((rows, D_MODEL), F32),
        scratch_shapes=[pltpu.VMEM((tm, D_MODEL), BF16), pltpu.VMEM((tm, D_MODEL), F32)],
        compiler_params=_cparams(("parallel", "arbitrary")),
        name="ffn_half",
    )(x, g_pre, g_post, w_gu, w_gu, w_d)


_NG = NSA_HEADS * 3
_C_Q, _C_CMP, _C_SLC, _C_WIN, _C_GQ, _C_GK, _C_GV, _C_GR, _C_XQ, _C_SM, _C_END = (
    0, 512, 768, 1024, 1280, 1536, 1792, 2304, 2816, 3328, 3456)


def _pack_w_in(w_in):
    offs = np.cumsum([0, NSA_HEADS * NSA_HD, 3 * KV_SLAB, _NG, GLA_HEADS * GLA_DK, GLA_HEADS * GLA_DK,
                      GLA_HEADS * GLA_DV, GLA_RANK, GLA_HEADS * GLA_DV, X_HEADS * X_HD])
    q, kv, ng, gq, gk, gv, glr, gr, xq = [w_in[:, offs[i]:offs[i + 1]] for i in range(9)]
    pad = jnp.zeros((D_MODEL, LANE - _NG - GLA_RANK), w_in.dtype)
    return jnp.concatenate([q, kv, gq, gk, gv, gr, xq, ng, glr, pad], axis=1).astype(BF16)


def _in_proj_kernel(h_ref, g_ref, w_ref, w2_ref, b2_ref,
                    u_ref, q_ref, cmp_ref, slc_ref, win_ref, ng_ref, gq_ref, gk_ref, gv_ref, la_ref, gr_ref,
                    xq_ref):
    u = _rms(h_ref[...], g_ref[...]).astype(BF16)
    u_ref[...] = u

    def proj(lo, hi):
        return _dot(u, w_ref[:, lo:hi])

    q_ref[...] = proj(_C_Q, _C_CMP) * (NSA_HD ** -0.5)
    cmp_ref[...] = proj(_C_CMP, _C_SLC)
    slc_ref[...] = proj(_C_SLC, _C_WIN)
    win_ref[...] = proj(_C_WIN, _C_GQ)
    gq_ref[...] = proj(_C_GQ, _C_GK) * (GLA_DK ** -0.5)
    gk_ref[...] = proj(_C_GK, _C_GV)
    gv_ref[...] = proj(_C_GV, _C_GR)
    gr_ref[...] = proj(_C_GR, _C_XQ)
    xq_ref[...] = proj(_C_XQ, _C_SM) * (X_HD ** -0.5)
    small = proj(_C_SM, _C_END)
    ng_ref[...] = jax.nn.sigmoid(small)
    z = _dot(small.astype(BF16), w2_ref[...]) + b2_ref[...]
    log_sig = jnp.minimum(z, 0.0) - jnp.log1p(jnp.exp(-jnp.abs(z)))
    la_ref[...] = log_sig / GLA_TAU


def _in_proj(h, g_pre_mix, w_cat, w2_pad, b2, tm):
    rows = h.shape[0]
    widths = [(D_MODEL, BF16), (512, F32), (256, F32), (256, F32), (256, F32), (LANE, F32), (256, F32),
              (256, F32), (512, F32), (256, F32), (512, F32), (512, F32)]
    return pl.pallas_call(
        _in_proj_kernel,
        grid=(rows // tm,),
        in_specs=[
            pl.BlockSpec((tm, D_MODEL), lambda i: (i, 0)),
            pl.BlockSpec((1, D_MODEL), lambda i: (0, 0)),
            pl.BlockSpec((D_MODEL, _C_END), lambda i: (0, 0)),
            pl.BlockSpec((LANE, 256), lambda i: (0, 0)),
            pl.BlockSpec((1, 256), lambda i: (0, 0)),
        ],
        out_specs=[pl.BlockSpec((tm, w), lambda i: (i, 0)) for w, _ in widths],
        out_shape=[jax.ShapeDtypeStruct((rows, w), dt) for w, dt in widths],
        compiler_params=_cparams(("parallel",)),
        name="in_proj",
    )(h, g_pre_mix, w_cat, w2_pad, b2)


def _pack_cmp_w1(w1):
    wa, wb = w1[:CMP_STRIDE], w1[CMP_STRIDE:]
    z = jnp.zeros_like(wa)
    rows = []
    for g in range(NSA_KV):
        cols = []
        for half in (wa, wb):
            for g2 in range(NSA_KV):
                cols.append(half if g2 == g else z)
        rows.append(jnp.concatenate(cols, axis=2))
    w = jnp.stack(rows, axis=1)
    return w.reshape(CMP_STRIDE // 2, 2 * NSA_KV * NSA_HD, 4 * CMP_HID).astype(BF16)


def _pack_cmp_w2(w2):
    z = jnp.zeros_like(w2)
    return jnp.concatenate([jnp.concatenate([w2, z], axis=1), jnp.concatenate([z, w2], axis=1)], axis=0).astype(BF16)


def _compress_ab(load_pos, wk_ref, wv_ref):
    hk = hv = None
    for lp in range(CMP_STRIDE // 2):
        xa = load_pos(2 * lp)
        xb = load_pos(2 * lp + 1)
        lk = jnp.concatenate([xa[:, :LANE], xb[:, :LANE]], axis=1).astype(BF16)
        lv = jnp.concatenate([xa[:, LANE:], xb[:, LANE:]], axis=1).astype(BF16)
        dk = _dot(lk, wk_ref[lp])
        dv = _dot(lv, wv_ref[lp])
        hk = dk if hk is None else hk + dk
        hv = dv if hv is None else hv + dv
    return hk, hv


def _compress_finish(ab, bias, w2_ref, shift_ref):
    n = ab.shape[0]
    shift_ref[0:n, :] = ab[:, 2 * CMP_HID:]
    shift_ref[n:n + 8, :] = jnp.zeros((8, 2 * CMP_HID), F32)
    h = ab[:, :2 * CMP_HID] + shift_ref[pl.ds(1, n), :] + bias
    return _dot(_silu(h).astype(BF16), w2_ref[...])


def _cmp_bias_kernel(pe_ref, w1_ref, o_ref):
    for kv in range(2):
        b = _dot(pe_ref[kv].astype(BF16), w1_ref[kv])
        o_ref[kv] = jnp.concatenate([b, b], axis=1)


def _cmp_bias(pe_cmp, w_cmp1):
    pe = jnp.broadcast_to(pe_cmp.reshape(2, 1, CMP_LEN * NSA_HD), (2, 8, CMP_LEN * NSA_HD))
    w1 = w_cmp1.reshape(2, CMP_LEN * NSA_HD, CMP_HID).astype(BF16)
    return pl.pallas_call(
        _cmp_bias_kernel,
        out_shape=jax.ShapeDtypeStruct((2, 8, 2 * CMP_HID), F32),
        name="cmp_bias",
    )(pe, w1)


def _nsa_prep_kernel(cmpv_ref, slc_ref, win_ref, wk_ref, wv_ref,
                     abk_ref, abv_ref, kslc_ref, vslct_ref, kwin_ref, vwint_ref):
    abk, abv = _compress_ab(lambda l: cmpv_ref[:, l * KV_SLAB:(l + 1) * KV_SLAB], wk_ref, wv_ref)
    abk_ref[...] = abk
    abv_ref[...] = abv
    nchunk = slc_ref.shape[0] // LANE
    ones_row = (lax.broadcasted_iota(jnp.int32, (VT_ROWS - NSA_HD, LANE), 0) == 0).astype(F32)
    for src, k_ref, vt_ref in ((slc_ref, kslc_ref, vslct_ref), (win_ref, kwin_ref, vwint_ref)):
        k_ref[...] = src[:, :LANE].astype(BF16)
        for c in range(nchunk):
            vt = src[c * LANE:(c + 1) * LANE, LANE:].T
            vt_ref[c] = jnp.concatenate([vt[:NSA_HD], ones_row, vt[NSA_HD:], ones_row], axis=0).astype(BF16)


def _nsa_prep(kv_cmp, kv_slc, kv_win, wk2, wv2, tm):
    t = kv_cmp.shape[0]
    nsub = t // CMP_STRIDE
    cmp_view = kv_cmp.reshape(nsub, CMP_STRIDE * KV_SLAB)
    const3 = lambda i: (0, 0, 0)
    return pl.pallas_call(
        _nsa_prep_kernel,
        grid=(t // tm,),
        in_specs=[
            pl.BlockSpec((tm // CMP_STRIDE, CMP_STRIDE * KV_SLAB), lambda i: (i, 0)),
            pl.BlockSpec((tm, KV_SLAB), lambda i: (i, 0)),
            pl.BlockSpec((tm, KV_SLAB), lambda i: (i, 0)),
            pl.BlockSpec(wk2.shape, const3),
            pl.BlockSpec(wv2.shape, const3),
        ],
        out_specs=[
            pl.BlockSpec((tm // CMP_STRIDE, 4 * CMP_HID), lambda i: (i, 0)),
            pl.BlockSpec((tm // CMP_STRIDE, 4 * CMP_HID), lambda i: (i, 0)),
            pl.BlockSpec((tm, LANE), lambda i: (i, 0)),
            pl.BlockSpec((tm // LANE, NSA_KV * VT_ROWS, LANE), lambda i: (i, 0, 0)),
            pl.BlockSpec((tm, LANE), lambda i: (i, 0)),
            pl.BlockSpec((tm // LANE, NSA_KV * VT_ROWS, LANE), lambda i: (i, 0, 0)),
        ],
        out_shape=[
            jax.ShapeDtypeStruct((nsub, 4 * CMP_HID), F32),
            jax.ShapeDtypeStruct((nsub, 4 * CMP_HID), F32),
            jax.ShapeDtypeStruct((t, LANE), BF16),
            jax.ShapeDtypeStruct((t // LANE, NSA_KV * VT_ROWS, LANE), BF16),
            jax.ShapeDtypeStruct((t, LANE), BF16),
            jax.ShapeDtypeStruct((t // LANE, NSA_KV * VT_ROWS, LANE), BF16),
        ],
        compiler_params=_cparams(("parallel",)),
        name="nsa_prep",
    )(cmp_view, kv_slc, kv_win, wk2, wv2)


def _cmp_finish_kernel(abk_ref, abv_ref, bias_ref, w2k_ref, w2v_ref, kc_ref, vct_ref, shift_ref):
    n = abk_ref.shape[0]
    kc_ref[...] = _compress_finish(abk_ref[...], bias_ref[0, 0:1, :], w2k_ref, shift_ref).astype(BF16)
    vc = _compress_finish(abv_ref[...], bias_ref[1, 0:1, :], w2v_ref, shift_ref)
    for c in range(n // LANE):
        vct_ref[:, c * LANE:(c + 1) * LANE] = vc[c * LANE:(c + 1) * LANE, :].T.astype(BF16)


def _cmp_finish(abk, abv, bias, w2k, w2v):
    n = abk.shape[0]
    return pl.pallas_call(
        _cmp_finish_kernel,
        out_shape=[jax.ShapeDtypeStruct((n, LANE), BF16), jax.ShapeDtypeStruct((LANE, n), BF16)],
        scratch_shapes=[pltpu.VMEM((n + 8, 2 * CMP_HID), F32)],
        compiler_params=pltpu.CompilerParams(vmem_limit_bytes=VMEM_LIMIT),
        name="cmp_finish",
    )(abk, abv, bias, w2k, w2v)


def _select_blocks(score, n_top):
    nb = score.shape[0]
    jj = lax.broadcasted_iota(jnp.int32, score.shape, 0)
    taken = jnp.float32(-3e38)
    for _ in range(n_top):
        m = jnp.max(score, axis=0, keepdims=True)
        idx = jnp.min(jnp.where(score == m, jj, nb), axis=0, keepdims=True)
        score = jnp.where(jj == idx, taken, score)
    return (score == taken).astype(F32)


def _block_scores(imp, jj, cur, n_real):
    s = jnp.where(jj == 0, FORCE, jnp.where(jj == cur, FORCE, jnp.where(jj == cur - 1, FORCE, imp)))
    s = jnp.where(jj <= cur, s, NEG)
    return jnp.where(jj < n_real, s, -jnp.inf)


def _cmp_to_slc_t(nc_pad, nb_pad):
    i = np.arange(nc_pad)[None, :]
    j = np.arange(nb_pad)[:, None]
    ov = (i * CMP_STRIDE < (j + 1) * SLC_BLK) & (i * CMP_STRIDE + CMP_LEN > j * SLC_BLK)
    return jnp.asarray(ov.astype(np.float32), dtype=BF16)


TQ = LANE


SLC_CK = 8 * TQ


def _flash_step(qbd, k, vt, bias, m_ref, acc_ref):
    s = _dot(k, qbd)
    for g in range(NSA_KV):
        ps = []
        for h in range(NSA_HPG):
            lo = (g * NSA_HPG + h) * TQ
            sc = s[:, lo:lo + TQ] + bias[g]
            m_old = m_ref[:, lo:lo + TQ]
            m_new = jnp.maximum(m_old, jnp.max(sc, axis=0, keepdims=True))
            ps.append(jnp.exp2(sc - m_new).astype(BF16))
            m_ref[:, lo:lo + TQ] = m_new
            acc_ref[g, :, h * TQ:(h + 1) * TQ] = acc_ref[g, :, h * TQ:(h + 1) * TQ] * jnp.exp2(m_old - m_new)
        acc_ref[g] += _dot(vt[g * VT_ROWS:(g + 1) * VT_ROWS, :], jnp.concatenate(ps, axis=1))


def _softmax_once(qbd, k, vt, bias):
    s = _dot(k, qbd)
    outs = []
    for g in range(NSA_KV):
        ps = []
        for h in range(NSA_HPG):
            lo = (g * NSA_HPG + h) * TQ
            sc = s[:, lo:lo + TQ] + bias
            ps.append(jnp.exp2(sc - jnp.max(sc, axis=0, keepdims=True)).astype(BF16))
        o = _dot(vt[g * VT_ROWS:(g + 1) * VT_ROWS, :], jnp.concatenate(ps, axis=1))
        outs.append(o[:NSA_HD] * (1.0 / o[NSA_HD:NSA_HD + 1]))
    return outs


def _flash_reset(m_ref, acc_ref):
    m_ref[...] = jnp.full(m_ref.shape, NEG, F32)
    acc_ref[...] = jnp.zeros(acc_ref.shape, F32)


def _flash_result(acc_ref):
    return [acc_ref[g, :NSA_HD, :] * (1.0 / acc_ref[g, NSA_HD:NSA_HD + 1, :]) for g in range(NSA_KV)]


def _nsa_prompt_kernel(q_ref, ng_ref, kc_ref, vct_ref, indt_ref, kslc_ref, vslct_ref, kwin_ref, vwint_ref,
                       o_ref, selb_ref, m_ref, acc_ref):
    i = pl.program_id(0)
    s0 = i * TQ
    ncp = kc_ref.shape[0]
    nbp = indt_ref.shape[0]
    n_blocks = kslc_ref.shape[0] // SLC_BLK

    qt = jnp.concatenate([q_ref[:, c * LANE:(c + 1) * LANE].T for c in range(BRANCH_W // LANE)], axis=0)
    zero = jnp.zeros((NSA_HD, NSA_HPG * TQ), F32)
    rows = []
    for g in range(NSA_KV):
        blk = jnp.concatenate(
            [qt[(g * NSA_HPG + h) * NSA_HD:(g * NSA_HPG + h + 1) * NSA_HD, :] for h in range(NSA_HPG)], axis=1)
        rows.append(jnp.concatenate([blk, zero] if g == 0 else [zero, blk], axis=1))
    qbd = (jnp.concatenate(rows, axis=0) * LOG2E).astype(BF16)

    q_lane = s0 + lax.broadcasted_iota(jnp.int32, (1, TQ), 1)

    c_end = lax.broadcasted_iota(jnp.int32, (ncp, TQ), 0) * CMP_STRIDE + (CMP_LEN - 1)
    bias_c = jnp.where(c_end <= q_lane, 0.0, NEG)
    col_ok = q_lane >= CMP_LEN - 1
    jj = lax.broadcasted_iota(jnp.int32, (nbp, TQ), 0)
    cur = q_lane // SLC_BLK
    o_cmp, scores = [], []
    for g in range(NSA_KV):
        sc_all = _dot(kc_ref[...], qbd[:, g * NSA_HPG * TQ:(g + 1) * NSA_HPG * TQ])
        psum = jnp.zeros((ncp, TQ), F32)
        ps = []
        for h in range(NSA_HPG):
            sc = sc_all[:, h * TQ:(h + 1) * TQ] + bias_c
            p = jnp.exp2(sc - jnp.max(sc, axis=0, keepdims=True))
            pn = p * jnp.where(col_ok, 1.0 / jnp.sum(p, axis=0, keepdims=True), 0.0)
            psum = psum + pn
            ps.append(pn.astype(BF16))
        o_cmp.append(_dot(vct_ref[g * NSA_HD:(g + 1) * NSA_HD, :], jnp.concatenate(ps, axis=1)))
        imp = sum(_dot(indt_ref[...], part) for part in _split3(psum))
        scores.append(_block_scores(imp, jj, cur, n_blocks))
    sel = _select_blocks(jnp.concatenate(scores, axis=1), min(SLC_TOPN, n_blocks))
    for g in range(NSA_KV):
        selb_ref[g] = (sel[:, g * TQ:(g + 1) * TQ] - 1.0) * (-NEG)

    blk_per_step = SLC_CK // SLC_BLK

    def slc_inputs(c):
        k = kslc_ref[pl.ds(pl.multiple_of(c * SLC_CK, SLC_CK), SLC_CK), :]
        vt = jnp.concatenate([vslct_ref[c * (SLC_CK // TQ) + j] for j in range(SLC_CK // TQ)], axis=1)
        bias = []
        for g in range(NSA_KV):
            rows = selb_ref[g, pl.ds(pl.multiple_of(c * blk_per_step, blk_per_step), blk_per_step), :]
            bias.append(jnp.concatenate(
                [jnp.broadcast_to(rows[j:j + 1, :], (SLC_BLK, TQ)) for j in range(blk_per_step)], axis=0))
        return k, vt, bias

    _flash_reset(m_ref, acc_ref)

    def slc_step(c):
        k, vt, bias = slc_inputs(c)
        _flash_step(qbd, k, vt, bias, m_ref, acc_ref)

    def slc_pair(t, carry):
        slc_step(2 * t)
        slc_step(2 * t + 1)
        return carry

    c_last = s0 // SLC_CK
    lax.fori_loop(0, c_last // 2, slc_pair, 0)

    @pl.when(c_last % 2 == 1)
    def _():
        slc_step(c_last - 1)
    k, vt, bias = slc_inputs(c_last)
    key_pos = c_last * SLC_CK + lax.broadcasted_iota(jnp.int32, (SLC_CK, TQ), 0)
    causal = key_pos <= q_lane
    _flash_step(qbd, k, vt, [jnp.where(causal, b, NEG) for b in bias], m_ref, acc_ref)
    o_slc = _flash_result(acc_ref)

    n_win = WINDOW + TQ
    w0 = pl.multiple_of(jnp.maximum(s0 - WINDOW, 0), TQ)
    dist = q_lane - (w0 + lax.broadcasted_iota(jnp.int32, (n_win, TQ), 0))
    bias_w = jnp.where(dist >= 0, jnp.where(dist < WINDOW, 0.0, NEG), NEG)
    vt_w = jnp.concatenate([vwint_ref[w0 // TQ + j] for j in range(n_win // TQ)], axis=1)
    o_win = _softmax_once(qbd, kwin_ref[pl.ds(w0, n_win), :], vt_w, bias_w)

    ngt = ng_ref[...].T
    for g in range(NSA_KV):
        heads = []
        for h in range(NSA_HPG):
            r = (g * NSA_HPG + h) * 3
            cs = slice(h * TQ, (h + 1) * TQ)
            heads.append(ngt[r:r + 1, :] * o_cmp[g][:, cs] + ngt[r + 1:r + 2, :] * o_slc[g][:, cs]
                         + ngt[r + 2:r + 3, :] * o_win[g][:, cs])
        for pair in range(NSA_HPG // 2):
            blk = jnp.concatenate(heads[2 * pair:2 * pair + 2], axis=0).T
            col = (g * NSA_HPG // 2 + pair) * LANE
            o_ref[:, col:col + LANE] = blk


def _nsa_prompt(q, ng, kc, vct, indt, kslc, vslct, kwin, vwint):
    t = q.shape[0]
    full2 = lambda i: (0, 0)
    full3 = lambda i: (0, 0, 0)
    return pl.pallas_call(
        _nsa_prompt_kernel,
        grid=(t // TQ,),
        in_specs=[
            pl.BlockSpec((TQ, BRANCH_W), lambda i: (i, 0)),
            pl.BlockSpec((TQ, LANE), lambda i: (i, 0)),
            pl.BlockSpec(kc.shape, full2),
            pl.BlockSpec(vct.shape, full2),
            pl.BlockSpec(indt.shape, full2),
            pl.BlockSpec(kslc.shape, full2),
            pl.BlockSpec(vslct.shape, full3),
            pl.BlockSpec(kwin.shape, full2),
            pl.BlockSpec(vwint.shape, full3),
        ],
        out_specs=pl.BlockSpec((TQ, BRANCH_W), lambda i: (i, 0)),
        out_shape=jax.ShapeDtypeStruct((t, BRANCH_W), F32),
        scratch_shapes=[
            pltpu.VMEM((NSA_KV, indt.shape[0], TQ), F32),
            pltpu.VMEM((1, NSA_HEADS * TQ), F32),
            pltpu.VMEM((NSA_KV, VT_ROWS, NSA_HPG * TQ), F32),
        ],
        compiler_params=_cparams(("arbitrary",)),
        name="nsa_prompt",
    )(q, ng, kc, vct, indt, kslc, vslct, kwin, vwint)


def _nsa_prompt_path(q, kv_cmp, kv_slc, kv_win, ng, w_cmp1, pe_cmp, w_cmp2):
    t = q.shape[0]
    wk2, wv2 = _pack_cmp_w1(w_cmp1[0]), _pack_cmp_w1(w_cmp1[1])
    w2k, w2v = _pack_cmp_w2(w_cmp2[0]), _pack_cmp_w2(w_cmp2[1])
    bias = _cmp_bias(pe_cmp, w_cmp1)
    abk, abv, kslc, vslct, kwin, vwint = _nsa_prep(kv_cmp, kv_slc, kv_win, wk2, wv2, min(t, 1024))
    kc, vct = _cmp_finish(abk, abv, bias, w2k, w2v)
    indt = _cmp_to_slc_t(t // CMP_STRIDE, t // SLC_BLK)
    return _nsa_prompt(q, ng, kc, vct, indt, kslc, vslct, kwin, vwint)


def _gla_kernel(gq_ref, gk_ref, gv_ref, la_ref, s0_ref, o_ref, sfin_ref, st_ref, *, chunk, nchunk, exact):
    j = pl.program_id(1)
    n_pair = GLA_HEADS // 2
    nseq = gq_ref.shape[0]

    @pl.when(j == 0)
    def _():
        for s in range(nseq):
            for p in range(n_pair):
                st_ref[s, p] = s0_ref[s, p].T

    if exact:
        hi = lax.Precision.HIGHEST
        mx = lambda x: x
        dot = lambda a, b: jnp.dot(a, b, precision=hi, preferred_element_type=F32)
        dot_nt = lambda a, b: lax.dot_general(a, b, (((1,), (1,)), ((), ())), precision=hi,
                                              preferred_element_type=F32)
        dot_tn = lambda a, b: lax.dot_general(a, b, (((0,), (0,)), ((), ())), precision=hi,
                                              preferred_element_type=F32)
    else:
        mx = lambda x: x.astype(BF16)
        dot, dot_nt, dot_tn = _dot, _dot_nt, _dot_tn

    row = lax.broadcasted_iota(jnp.int32, (chunk, chunk), 0)
    col = lax.broadcasted_iota(jnp.int32, (chunk, chunk), 1)
    tri = row >= col
    tri_f = tri.astype(F32)
    head0 = lax.broadcasted_iota(jnp.int32, (chunk, LANE), 1) < GLA_DK

    for s in range(nseq):
        carry = []
        for ci in range(nchunk):
            rows = slice(ci * chunk, (ci + 1) * chunk)
            la = la_ref[s, rows, :]
            if exact:
                b = dot(tri_f, la)
            else:
                b = sum(_dot(tri_f.astype(BF16), part) for part in _split3(la))
            b_end = b[chunk - 1:chunk, :]
            k = gk_ref[s, rows, :]
            qe = gq_ref[s, rows, :] * jnp.exp(b)
            ke = k * jnp.exp(-b)
            kend = k * jnp.exp(b_end - b)
            per_pair = []
            for p in range(n_pair):
                sl = slice(p * LANE, (p + 1) * LANE)
                ke_p = mx(ke[:, sl])
                kv_sum, qes = None, []
                for hh in range(2):
                    h = 2 * p + hh
                    hm = head0 if hh == 0 else jnp.logical_not(head0)
                    qe_h = mx(jnp.where(hm, qe[:, sl], 0.0))
                    kend_h = mx(jnp.where(hm, kend[:, sl], 0.0))
                    att = jnp.where(tri, dot_nt(qe_h, ke_p), 0.0)
                    v = mx(gv_ref[s, rows, h * GLA_DV:(h + 1) * GLA_DV])
                    o_ref[s, rows, h * GLA_DV:(h + 1) * GLA_DV] = dot(mx(att), v)
                    term = dot_tn(v, kend_h)
                    kv_sum = term if kv_sum is None else kv_sum + term
                    qes.append(qe_h)
                per_pair.append((jnp.exp(b_end[:, sl]), kv_sum, qes))
            carry.append(per_pair)

        for p in range(n_pair):
            st = st_ref[s, p]
            for ci in range(nchunk):
                rows = slice(ci * chunk, (ci + 1) * chunk)
                decay, kv_sum, qes = carry[ci][p]
                st_mx = mx(st)
                for hh in range(2):
                    h = 2 * p + hh
                    o_ref[s, rows, h * GLA_DV:(h + 1) * GLA_DV] += dot_nt(qes[hh], st_mx)
                st = st * decay + kv_sum
            st_ref[s, p] = st

    @pl.when(j == pl.num_programs(1) - 1)
    def _():
        for s in range(nseq):
            for p in range(n_pair):
                sfin_ref[s, p] = st_ref[s, p].T


def _gla(gq, gk, gv, la, s0, chunk, nchunk, exact, sb):
    bsz, t, _ = gq.shape
    rb = chunk * nchunk
    row_spec = lambda w: pl.BlockSpec((sb, rb, w), lambda b, j: (b, j, 0))
    st_spec = pl.BlockSpec((sb, GLA_HEADS // 2, LANE, LANE), lambda b, j: (b, 0, 0, 0))
    return pl.pallas_call(
        functools.partial(_gla_kernel, chunk=chunk, nchunk=nchunk, exact=exact),
        grid=(bsz // sb, t // rb),
        in_specs=[row_spec(256), row_spec(256), row_spec(512), row_spec(256), st_spec],
        out_specs=[row_spec(512), st_spec],
        out_shape=[jax.ShapeDtypeStruct((bsz, t, BRANCH_W), F32),
                   jax.ShapeDtypeStruct((bsz, GLA_HEADS // 2, LANE, LANE), F32)],
        scratch_shapes=[pltpu.VMEM((sb, GLA_HEADS // 2, LANE, LANE), F32)],
        compiler_params=_cparams(("parallel", "arbitrary")),
        name="gla",
    )(gq, gk, gv, la, s0)


def _mem_proj_kernel(mem_ref, g_ref, w_ref, o_ref):
    o_ref[...] = _dot(_rms(mem_ref[...], g_ref[...]).astype(BF16), w_ref[...])


def _mem_proj(mem, g_mem, w_mem_kv):
    return pl.pallas_call(
        _mem_proj_kernel,
        out_shape=jax.ShapeDtypeStruct((mem.shape[0], w_mem_kv.shape[1]), F32),
        compiler_params=pltpu.CompilerParams(vmem_limit_bytes=VMEM_LIMIT),
        name="mem_proj",
    )(mem, g_mem, w_mem_kv)


def _xattn_kernel(xq_ref, mem_ref, o_ref):
    for b in range(xq_ref.shape[0]):
        for h in range(X_HEADS):
            cs = slice(h * X_HD, (h + 1) * X_HD)
            k = mem_ref[b, :, cs].astype(BF16)
            v = mem_ref[b, :, BRANCH_W + h * X_HD:BRANCH_W + (h + 1) * X_HD].astype(BF16)
            s = _dot_nt(xq_ref[b, :, cs].astype(BF16), k)
            p = jnp.exp(s - jnp.max(s, axis=-1, keepdims=True))
            p = p * (1.0 / jnp.sum(p, axis=-1, keepdims=True))
            o_ref[b, :, cs] = _dot(p.astype(BF16), v)


def _xattn(xq, mem_kv, tm, sb):
    bsz, t, _ = xq.shape
    return pl.pallas_call(
        _xattn_kernel,
        grid=(bsz // sb, t // tm),
        in_specs=[pl.BlockSpec((sb, tm, BRANCH_W), lambda b, i: (b, i, 0)),
                  pl.BlockSpec((sb,) + mem_kv.shape[1:], lambda b, i: (b, 0, 0))],
        out_specs=pl.BlockSpec((sb, tm, BRANCH_W), lambda b, i: (b, i, 0)),
        out_shape=jax.ShapeDtypeStruct(xq.shape, F32),
        compiler_params=_cparams(("parallel", "parallel")),
        name="xattn",
    )(xq, mem_kv)


def _merge_kernel(u_ref, h_ref, onsa_ref, ogla_ref, gr_ref, ox_ref, ggla_ref, wb_ref, wmg_ref, bmg_ref, wout_ref,
                  gpost_ref, o_ref):
    u = u_ref[...]
    parts = []
    for h in range(GLA_HEADS):
        cs = slice(h * GLA_DV, (h + 1) * GLA_DV)
        parts.append(_rms(ogla_ref[:, cs], ggla_ref[...]) * _silu(gr_ref[:, cs]))
    branches = [onsa_ref[...], jnp.concatenate(parts, axis=1), ox_ref[...]]
    mixed = None
    for b in range(3):
        cs = slice(b * D_MODEL, (b + 1) * D_MODEL)
        gate = jax.nn.sigmoid(_dot(u, wmg_ref[:, cs]) + bmg_ref[:, cs])
        term = gate * _dot(branches[b].astype(BF16), wb_ref[b])
        mixed = term if mixed is None else mixed + term
    m = _dot(mixed.astype(BF16), wout_ref[...])
    o_ref[...] = h_ref[...] + _rms(m, gpost_ref[...])


def _merge(u, h, o_nsa, o_gla, gr, o_x, g_gla_out, w_branch, w_mg, b_mg, w_out, g_post, tm):
    rows = h.shape[0]
    row = lambda w: pl.BlockSpec((tm, w), lambda i: (i, 0))
    full = lambda a: pl.BlockSpec(a.shape, lambda i: (0,) * a.ndim)
    return pl.pallas_call(
        _merge_kernel,
        grid=(rows // tm,),
        in_specs=[row(D_MODEL), row(D_MODEL), row(BRANCH_W), row(BRANCH_W), row(BRANCH_W), row(BRANCH_W),
                  full(g_gla_out), full(w_branch), full(w_mg), full(b_mg), full(w_out), full(g_post)],
        out_specs=row(D_MODEL),
        out_shape=jax.ShapeDtypeStruct((rows, D_MODEL), F32),
        compiler_params=_cparams(("parallel",)),
        name="merge",
    )(u, h, o_nsa, o_gla, gr, o_x, g_gla_out, w_branch, w_mg, b_mg, w_out, g_post)


def _page_copies(pt_ref, pool_ref, buf_ref, sem_ref, seq, slot):
    npages = buf_ref.shape[-1] // PAGE_SIZE
    return [pltpu.make_async_copy(pool_ref.at[pt_ref[seq, p]],
                                  buf_ref.at[slot, :, :, :, pl.ds(p * PAGE_SIZE, PAGE_SIZE)], sem_ref.at[slot])
            for p in range(npages)]


def _gather_pages(pt_ref, pool_ref, buf_ref, sem_ref):
    b = pl.program_id(0)
    slot = b % 2

    @pl.when(b == 0)
    def _():
        for cp in _page_copies(pt_ref, pool_ref, buf_ref, sem_ref, 0, 0):
            cp.start()

    @pl.when(b + 1 < pl.num_programs(0))
    def _():
        for cp in _page_copies(pt_ref, pool_ref, buf_ref, sem_ref, b + 1, 1 - slot):
            cp.start()

    for cp in _page_copies(pt_ref, pool_ref, buf_ref, sem_ref, b, slot):
        cp.wait()
    return slot


def _row_token(shape):
    return lax.broadcasted_iota(jnp.int32, shape, 0) % 8


def _tap_permutation():
    r = np.arange(2 * PAGE_SIZE)
    l, jj = r // CMP_STRIDE, r % CMP_STRIDE
    perm = np.zeros((2 * PAGE_SIZE, 2 * PAGE_SIZE), np.float32)
    perm[r, CMP_STRIDE * jj + l] = 1.0
    return jnp.asarray(perm, dtype=BF16)


def _sample_cmp_kernel(pt_ref, pool_ref, qbd_ref, perm_ref, wk_ref, wv_ref, bias_ref, w2k_ref, w2v_ref, ind_ref,
                       oc_ref, imp_ref, buf_ref, sem_ref, taps_ref, shift_ref, *, past):
    slot = _gather_pages(pt_ref, pool_ref, buf_ref, sem_ref)
    span = 2 * PAGE_SIZE
    nspan = buf_ref.shape[-1] // span
    nsub = nspan * CMP_STRIDE

    perm = perm_ref[...]
    for q in range(nspan):
        x = buf_ref[slot, :, :, :, q * span:(q + 1) * span]
        t = _dot_nt(perm, x.reshape(KV_SLAB, span).astype(BF16)).astype(BF16)
        for l in range(CMP_STRIDE):
            taps_ref[l, q * CMP_STRIDE:(q + 1) * CMP_STRIDE, :] = t[l * CMP_STRIDE:(l + 1) * CMP_STRIDE, :]

    abk, abv = _compress_ab(lambda l: taps_ref[l], wk_ref, wv_ref)
    kc = _compress_finish(abk, bias_ref[0, 0:1, :], w2k_ref, shift_ref).astype(BF16)
    vc = _compress_finish(abv, bias_ref[1, 0:1, :], w2v_ref, shift_ref).astype(BF16)
    qbd = qbd_ref[0]
    s = _dot_nt(qbd, kc)
    q_pos = past + _row_token(s.shape)
    mask = lax.broadcasted_iota(jnp.int32, s.shape, 1) * CMP_STRIDE + (CMP_LEN - 1) <= q_pos
    s = jnp.where(mask, s, NEG)
    p = jnp.exp(s - jnp.max(s, axis=-1, keepdims=True))
    pn = jnp.where(mask, p * (1.0 / jnp.sum(p, axis=-1, keepdims=True)), 0.0)
    oc_ref[0] = _dot(pn.astype(BF16), vc)
    rows_g = NSA_HPG * 8
    psum = jnp.concatenate(
        [sum(pn[g * rows_g + h * 8:g * rows_g + (h + 1) * 8, :] for h in range(NSA_HPG)) for g in range(NSA_KV)],
        axis=0)
    imp_ref[0] = sum(_dot(part, ind_ref[...]) for part in _split3(psum))


def _sample_cmp(page_table, pool_t, qbd, wk2, wv2, bias, w2k, w2v, ind):
    nseq, npages = page_table.shape
    assert npages % 2 == 0
    past = npages * PAGE_SIZE
    nsub = past // CMP_STRIDE
    perm = _tap_permutation()
    full = lambda a: pl.BlockSpec(a.shape, lambda b, pt: (0,) * a.ndim)
    grid_spec = pltpu.PrefetchScalarGridSpec(
        num_scalar_prefetch=1,
        grid=(nseq,),
        in_specs=[pl.BlockSpec(memory_space=pl.ANY),
                  pl.BlockSpec((1,) + qbd.shape[1:], lambda b, pt: (b, 0, 0)),
                  full(perm), full(wk2), full(wv2), full(bias), full(w2k), full(w2v), full(ind)],
        out_specs=[pl.BlockSpec((1, 64, LANE), lambda b, pt: (b, 0, 0)),
                   pl.BlockSpec((1, 16, ind.shape[1]), lambda b, pt: (b, 0, 0))],
        scratch_shapes=[pltpu.VMEM((2, 2, NSA_KV, NSA_HD, past), F32),
                        pltpu.SemaphoreType.DMA((2,)),
                        pltpu.VMEM((CMP_STRIDE, nsub, KV_SLAB), BF16),
                        pltpu.VMEM((nsub + 8, 2 * CMP_HID), F32)],
    )
    return pl.pallas_call(
        functools.partial(_sample_cmp_kernel, past=past),
        grid_spec=grid_spec,
        out_shape=[jax.ShapeDtypeStruct((nseq, 64, LANE), F32),
                   jax.ShapeDtypeStruct((nseq, 16, ind.shape[1]), F32)],
        compiler_params=_cparams(("arbitrary",)),
        name="sample_cmp",
    )(page_table, pool_t, qbd, perm, wk2, wv2, bias, w2k, w2v, ind)


def _sample_select_kernel(imp_ref, sel_ref, *, past, n_real, n_top):
    imp_t = imp_ref[...].T
    jj = lax.broadcasted_iota(jnp.int32, imp_t.shape, 0)
    tok = lax.broadcasted_iota(jnp.int32, (1, imp_t.shape[1]), 1) % 8
    cur = (past + tok) // SLC_BLK
    sel_ref[...] = _select_blocks(_block_scores(imp_t, jj, cur, n_real), n_top).T


def _sample_select(imp, past, n_real, tm):
    rows, nbp = imp.shape
    return pl.pallas_call(
        functools.partial(_sample_select_kernel, past=past, n_real=n_real, n_top=min(SLC_TOPN, n_real)),
        grid=(rows // tm,),
        in_specs=[pl.BlockSpec((tm, nbp), lambda i: (i, 0))],
        out_specs=pl.BlockSpec((tm, nbp), lambda i: (i, 0)),
        out_shape=jax.ShapeDtypeStruct((rows, nbp), F32),
        compiler_params=_cparams(("parallel",)),
        name="sample_select",
    )(imp)


def _masked_softmax_pv(parts):
    ms = [jnp.where(mk, s, NEG) for s, mk, _, _ in parts]
    m = functools.reduce(jnp.maximum, [jnp.max(x, axis=-1, keepdims=True) for x in ms])
    ps = [jnp.where(part[1], jnp.exp(x - m), 0.0) for x, part in zip(ms, parts)]
    l = sum(jnp.sum(p, axis=-1, keepdims=True) for p in ps)
    o = sum((_dot_nt if part[3] else _dot)(p.astype(BF16), part[2]) for p, part in zip(ps, parts))
    return o * (1.0 / l)


def _sample_attn_kernel(pt_ref, pool_ref, qbd_ref, sel_ref, expand_ref, newslc_ref, win_ref, newwin_ref,
                        newwint_ref, gate_ref, oc_ref, o_ref, wout_ref, buf_ref, sem_ref):
    slot = _gather_pages(pt_ref, pool_ref, buf_ref, sem_ref)
    npast = buf_ref.shape[-1]
    nblk_past = npast // SLC_BLK
    qbd = qbd_ref[0]
    rows = qbd.shape[0]
    tok = _row_token((rows, LANE))
    lane = lax.broadcasted_iota(jnp.int32, (rows, LANE), 1)

    kt = buf_ref[slot, 0].reshape(NSA_KV * NSA_HD, npast).astype(BF16)
    vt = buf_ref[slot, 1].reshape(NSA_KV * NSA_HD, npast).astype(BF16)
    sel = sel_ref[0]
    selrows = jnp.concatenate([sel[g * 8:(g + 1) * 8, :] for g in range(NSA_KV) for _ in range(NSA_HPG)], axis=0)
    mask_p = _dot(selrows[:, :nblk_past].astype(BF16), expand_ref[...]) > 0.5
    mask_n = jnp.where(lane <= tok, selrows[:, nblk_past:nblk_past + 1], 0.0) > 0.5
    knew = newslc_ref[0, :, 0:LANE].astype(BF16)
    vnew = newslc_ref[0, :, LANE:KV_SLAB].astype(BF16)
    o_slc = _masked_softmax_pv([(_dot(qbd, kt), mask_p, vt, True), (_dot_nt(qbd, knew), mask_n, vnew, False)])

    wbuf = win_ref.shape[-1]
    lane_w = lax.broadcasted_iota(jnp.int32, (rows, wbuf), 1)
    mask_w = lane_w + (WINDOW - wbuf) > _row_token((rows, wbuf))
    kwn = newwin_ref[0, :, 0:LANE].astype(BF16)
    vwn = newwin_ref[0, :, LANE:KV_SLAB].astype(BF16)
    o_win = _masked_softmax_pv([(_dot(qbd, win_ref[0, 0].astype(BF16)), mask_w, win_ref[0, 1].astype(BF16), True),
                                (_dot_nt(qbd, kwn), lane <= tok, vwn, False)])

    g = gate_ref[0]
    o_ref[0] = g[:, 0:1] * oc_ref[0] + g[:, 1:2] * o_slc + g[:, 2:3] * o_win

    tail = lax.broadcasted_iota(jnp.int32, (NSA_KV * NSA_HD, LANE), 1) >= LANE - 8
    for kv in range(2):
        rolled = pltpu.roll(win_ref[0, kv], wbuf - 8, axis=1)
        wout_ref[0, kv, :, 0:wbuf - LANE] = rolled[:, 0:wbuf - LANE]
        wout_ref[0, kv, :, wbuf - LANE:wbuf] = jnp.where(tail, newwint_ref[0, kv], rolled[:, wbuf - LANE:wbuf])


def _sample_attn(page_table, pool_t, qbd, sel, expand, newslc, win_t, newwin, newwin_t, gates, o_cmp):
    nseq, npages = page_table.shape
    per_seq = lambda a: pl.BlockSpec((1,) + a.shape[1:], lambda b, pt: (b,) + (0,) * (a.ndim - 1))
    grid_spec = pltpu.PrefetchScalarGridSpec(
        num_scalar_prefetch=1,
        grid=(nseq,),
        in_specs=[pl.BlockSpec(memory_space=pl.ANY), per_seq(qbd), per_seq(sel),
                  pl.BlockSpec(expand.shape, lambda b, pt: (0, 0)),
                  per_seq(newslc), per_seq(win_t), per_seq(newwin), per_seq(newwin_t), per_seq(gates),
                  per_seq(o_cmp)],
        out_specs=[per_seq(o_cmp), per_seq(win_t)],
        scratch_shapes=[pltpu.VMEM((2, 2, NSA_KV, NSA_HD, npages * PAGE_SIZE), F32),
                        pltpu.SemaphoreType.DMA((2,))],
    )
    return pl.pallas_call(
        _sample_attn_kernel,
        grid_spec=grid_spec,
        out_shape=[jax.ShapeDtypeStruct(o_cmp.shape, F32), jax.ShapeDtypeStruct(win_t.shape, F32)],
        compiler_params=_cparams(("arbitrary",)),
        name="sample_attn",
    )(page_table, pool_t, qbd, sel, expand, newslc, win_t, newwin, newwin_t, gates, o_cmp)


def _nsa_sample_path(q, kv_cmp, kv_slc, kv_win, ng, pool_cmp, pool_slc, win_buf, page_table, w_cmp1, pe_cmp, w_cmp2):
    nseq, npages = page_table.shape
    pos_minor = lambda a: jnp.transpose(a, (0, 2, 3, 4, 1))
    wbuf = win_buf.shape[1]
    win_t = pos_minor(win_buf).reshape(nseq, 2, NSA_KV * NSA_HD, wbuf)
    tn = q.shape[0] // nseq
    past = npages * PAGE_SIZE
    n_real = -(-(past + tn) // SLC_BLK)
    nbp = -(-n_real // (2 * LANE)) * (2 * LANE)
    nsub = past // CMP_STRIDE
    assert tn == 8 and past % SLC_BLK == 0 and (past + tn - CMP_LEN) // CMP_STRIDE + 1 == nsub - 1

    qg = q.reshape(nseq, tn, NSA_KV, NSA_HPG, NSA_HD).transpose(0, 2, 3, 1, 4).reshape(nseq, NSA_KV, NSA_HPG * tn,
                                                                                        NSA_HD)
    z = jnp.zeros_like(qg[:, 0])
    qbd = jnp.stack([jnp.concatenate([qg[:, 0], z], axis=-1), jnp.concatenate([z, qg[:, 1]], axis=-1)], axis=1)
    qbd = qbd.reshape(nseq, NSA_HEADS * tn, LANE).astype(BF16)
    gates = ng[:, :_NG].reshape(nseq, tn, NSA_HEADS, 3).transpose(0, 2, 1, 3).reshape(nseq, NSA_HEADS * tn, 3)
    gates = jnp.pad(gates, ((0, 0), (0, 0), (0, LANE - 3)))

    def new_block(kv):
        return jnp.pad(kv.reshape(nseq, tn, KV_SLAB), ((0, 0), (0, LANE - tn), (0, 0)))

    wk2, wv2 = _pack_cmp_w1(w_cmp1[0]), _pack_cmp_w1(w_cmp1[1])
    w2k, w2v = _pack_cmp_w2(w_cmp2[0]), _pack_cmp_w2(w_cmp2[1])
    bias = _cmp_bias(pe_cmp, w_cmp1)
    ind = _cmp_to_slc_t(nsub, nbp).T
    o_cmp, imp = _sample_cmp(page_table, pos_minor(pool_cmp), qbd, wk2, wv2, bias, w2k, w2v, ind)
    sel = _sample_select(imp.reshape(nseq * 2 * tn, nbp), past, n_real, 512).reshape(nseq, 2 * tn, nbp)
    nblk_past = past // SLC_BLK
    expand = jnp.asarray(np.repeat(np.eye(nblk_past, dtype=np.float32), SLC_BLK, axis=1), dtype=BF16)
    newwin_t = jnp.pad(kv_win.reshape(nseq, tn, KV_SLAB).transpose(0, 2, 1), ((0, 0), (0, 0), (LANE - tn, 0)))
    newwin_t = newwin_t.reshape(nseq, 2, NSA_KV * NSA_HD, LANE)
    o, win_new_t = _sample_attn(page_table, pos_minor(pool_slc), qbd, sel, expand, new_block(kv_slc), win_t,
                                new_block(kv_win), newwin_t, gates, o_cmp)
    o = o.reshape(nseq, NSA_KV, NSA_HPG, tn, NSA_KV, NSA_HD)
    o = jnp.stack([o[:, g, :, :, g, :] for g in range(NSA_KV)], axis=1)
    win_new = jnp.transpose(win_new_t.reshape(nseq, 2, NSA_KV, NSA_HD, wbuf), (0, 4, 1, 2, 3))
    return o.transpose(0, 3, 1, 2, 4).reshape(nseq * tn, BRANCH_W), win_new


def _mixer(h, mem_kv, nsa_fn, gla_s0, gla_cfg, wts, tm, x_tm):
    bsz, t, _ = h.shape
    rows = bsz * t
    h2 = h.reshape(rows, D_MODEL)
    (u, q, kcmp, kslc, kwin, ng, gq, gk, gv, la, gr, xq) = _in_proj(h2, wts['g_pre_mix'], wts['w_cat'], wts['w2_pad'],
                                                                   wts['b_gla_g'], tm)
    o_nsa, extra = nsa_fn(q, kcmp, kslc, kwin, ng)
    r3 = lambda a: a.reshape(bsz, t, a.shape[-1])
    chunk, nchunk, exact, sb = gla_cfg
    o_gla, s_fin = _gla(r3(gq), r3(gk), r3(gv), r3(la), gla_s0.reshape(bsz, GLA_HEADS // 2, LANE, LANE), chunk,
                        nchunk, exact, sb)
    o_x = _xattn(r3(xq), mem_kv, x_tm, sb)
    h_new = _merge(u, h2, o_nsa, o_gla.reshape(rows, BRANCH_W), gr, o_x.reshape(rows, BRANCH_W), wts['g_gla_out'],
                   wts['w_branch'], wts['w_mg'], wts['b_mg'], wts['w_out'], wts['g_post_mix'], tm)
    s_fin = s_fin.reshape(bsz, GLA_HEADS, GLA_DK, GLA_DV)
    return h_new.reshape(bsz, t, D_MODEL), (kcmp, kslc, kwin), s_fin, extra


def kernel(x_prompt, x_sample, mem_prompt, cache_cmp_kv, cache_slc_kv, cache_win_kv, state_gla, cache_mem_kv,
           page_table, g_pre_ff1, g_post_ff1, w_ff1_gu, w_ff1_d, g_pre_mix, g_post_mix, w_in, w_cmp1, pe_cmp,
           w_cmp2, w_gla_g2, b_gla_g, g_gla_out, g_mem, w_mem_kv, w_branch, w_merge_gate, b_merge_gate, w_out,
           g_pre_ff2, g_post_ff2, w_ff2_gu, w_ff2_d):
    depth = w_in.shape[0]
    assert depth == 1
    l = 0
    bp, tp, _ = x_prompt.shape
    bs, ts, _ = x_sample.shape
    assert bp == 1
    row1 = lambda a: a.reshape(1, -1)
    wts = dict(
        g_pre_mix=row1(g_pre_mix[l]), w_cat=_pack_w_in(w_in[l]),
        w2_pad=jnp.zeros((LANE, GLA_HEADS * GLA_DK), F32).at[_NG:_NG + GLA_RANK].set(w_gla_g2[l]).astype(BF16),
        b_gla_g=row1(b_gla_g[l]), g_gla_out=row1(g_gla_out[l]), w_branch=w_branch[l].astype(BF16),
        w_mg=w_merge_gate[l].astype(BF16), b_mg=row1(b_merge_gate[l]), w_out=w_out[l].astype(BF16),
        g_post_mix=row1(g_post_mix[l]))

    def ffn(x, g_pre, g_post, w_gu, w_d, tm):
        b, t, _ = x.shape
        return _ffn_half(x.reshape(b * t, D_MODEL), row1(g_pre), row1(g_post), w_gu, w_d, tm).reshape(x.shape)

    w1gu, w1d = w_ff1_gu[l].astype(BF16), w_ff1_d[l].astype(BF16)
    w2gu, w2d = w_ff2_gu[l].astype(BF16), w_ff2_d[l].astype(BF16)
    tm_p, tm_s = 512, 256

    hp = ffn(x_prompt, g_pre_ff1[l], g_post_ff1[l], w1gu, w1d, tm_p)
    mem_kv = _mem_proj(mem_prompt[0], row1(g_mem[l]), w_mem_kv[l].astype(BF16))
    nsa_p = lambda q, kc, ks, kw, ng: (_nsa_prompt_path(q, kc, ks, kw, ng, w_cmp1[l], pe_cmp[l], w_cmp2[l]), None)
    s0 = jnp.zeros((bp, GLA_HEADS, GLA_DK, GLA_DV), F32)
    hp, (cmp_p, slc_p, win_p), gla_p, _ = _mixer(hp, mem_kv[None], nsa_p, s0, (GLA_CHUNK, 4, False, 1), wts, tm_p, tm_p)
    yp = ffn(hp, g_pre_ff2[l], g_post_ff2[l], w2gu, w2d, tm_p)

    hs = ffn(x_sample, g_pre_ff1[l], g_post_ff1[l], w1gu, w1d, tm_s)
    wbuf = cache_win_kv.shape[2]
    nsa_s = lambda q, kc, ks, kw, ng: _nsa_sample_path(q, kc, ks, kw, ng, cache_cmp_kv[l], cache_slc_kv[l],
                                                       cache_win_kv[l], page_table, w_cmp1[l], pe_cmp[l], w_cmp2[l])
    mem_s = cache_mem_kv[l].reshape(bs, cache_mem_kv.shape[2], 2 * BRANCH_W)
    hs, (cmp_s, slc_s, _), gla_s, win_s = _mixer(hs, mem_s, nsa_s, state_gla[l], (min(GLA_CHUNK, ts), 1, True, 8), wts,
                                                 tm_s, ts)
    ys = ffn(hs, g_pre_ff2[l], g_post_ff2[l], w2gu, w2d, tm_s)

    kv5 = lambda a, b, t: a.reshape(1, b, t, 2, NSA_KV, NSA_HD)
    wp = min(WINDOW, tp)
    return (yp, ys,
            kv5(cmp_p, bp, tp), kv5(slc_p, bp, tp), kv5(win_p.reshape(bp, tp, KV_SLAB)[:, tp - wp:], bp, wp),
            gla_p[None], mem_kv.reshape(1, bp, mem_kv.shape[0], 2, X_HEADS, X_HD),
            kv5(cmp_s, bs, ts), kv5(slc_s, bs, ts), kv5(win_s, bs, wbuf), gla_s[None])
```

```python
import functools

import numpy as np
import jax
import jax.numpy as jnp
from jax import lax
from jax.experimental import pallas as pl
from jax.experimental.pallas import tpu as pltpu

F32 = jnp.float32
BF16 = jnp.bfloat16

D_MODEL = 1024
BRANCH_W = D_MODEL // 2
NSA_HEADS = 8
NSA_KV = 2
NSA_HPG = NSA_HEADS // NSA_KV
NSA_HD = BRANCH_W // NSA_HEADS
CMP_LEN = 32
CMP_STRIDE = 16
CMP_HID = 128
SLC_BLK = 64
SLC_TOPN = 16
WINDOW = 512
GLA_HEADS = 4
GLA_DV = BRANCH_W // GLA_HEADS
GLA_DK = GLA_DV // 2
GLA_RANK = 16
GLA_TAU = 16.0
GLA_CHUNK = 64
X_HEADS = 4
X_HD = BRANCH_W // X_HEADS
D_FF = 2816
EPS = 1e-6
NEG = -1e30
FORCE = 1e6
PAGE_SIZE = 128

LANE = 128
BF16_ROWS = 16
VT_ROWS = NSA_HD + BF16_ROWS
LOG2E = 1.4426950408889634
KV_SLAB = 2 * NSA_KV * NSA_HD
VMEM_LIMIT = 56 * 1024 * 1024


def _cparams(sem):
    return pltpu.CompilerParams(dimension_semantics=sem, vmem_limit_bytes=VMEM_LIMIT)


def _rms(x, g):
    xf = x.astype(F32)
    return xf * lax.rsqrt(jnp.mean(xf * xf, axis=-1, keepdims=True) + EPS) * g


def _silu(x):
    return x * jax.nn.sigmoid(x)


def _dot(a, b):
    return jnp.dot(a, b, preferred_element_type=F32)


def _dot_nt(a, b):
    return lax.dot_general(a, b, (((1,), (1,)), ((), ())), preferred_element_type=F32)


def _dot_tn(a, b):
    return lax.dot_general(a, b, (((0,), (0,)), ((), ())), preferred_element_type=F32)


def _split3(x):
    hi = x.astype(BF16)
    r1 = x - hi.astype(F32)
    mid = r1.astype(BF16)
    lo = (r1 - mid.astype(F32)).astype(BF16)
    return hi, mid, lo


def _ffn_kernel(x_ref, gpre_ref, gpost_ref, wg_ref, wu_ref, wd_ref, o_ref, xn_ref, acc_ref):
    f = pl.program_id(1)

    @pl.when(f == 0)
    def _():
        xn_ref[...] = _rms(x_ref[...], gpre_ref[...]).astype(BF16)
        acc_ref[...] = jnp.zeros_like(acc_ref)

    xn = xn_ref[...]
    a = _dot(xn, wg_ref[...])
    b = _dot(xn, wu_ref[...])
    acc_ref[...] += _dot((_silu(a) * b).astype(BF16), wd_ref[...])

    @pl.when(f == pl.num_programs(1) - 1)
    def _():
        o_ref[...] = x_ref[...] + 0.5 * _rms(acc_ref[...], gpost_ref[...])


def _ffn_half(x, g_pre, g_post, w_gu, w_d, tm):
    rows = x.shape[0]
    tf = D_FF // 2
    nf = D_FF // tf
    return pl.pallas_call(
        _ffn_kernel,
        grid=(rows // tm, nf),
        in_specs=[
            pl.BlockSpec((tm, D_MODEL), lambda i, f: (i, 0)),
            pl.BlockSpec((1, D_MODEL), lambda i, f: (0, 0)),
            pl.BlockSpec((1, D_MODEL), lambda i, f: (0, 0)),
            pl.BlockSpec((D_MODEL, tf), lambda i, f: (0, f)),
            pl.BlockSpec((D_MODEL, tf), lambda i, f: (0, nf + f)),
            pl.BlockSpec((tf, D_MODEL), lambda i, f: (f, 0)),
        ],
        out_specs=pl.BlockSpec((tm, D_MODEL), lambda i, f: (i, 0)),
        out_shape=jax.ShapeDtypeStruct((rows, D_MODEL), F32),
        scratch_shapes=[pltpu.VMEM((tm, D_MODEL), BF16), pltpu.VMEM((tm, D_MODEL), F32)],
        compiler_params=_cparams(("parallel", "arbitrary")),
        name="ffn_half",
    )(x, g_pre, g_post, w_gu, w_gu, w_d)


_NG = NSA_HEADS * 3
_C_Q, _C_CMP, _C_SLC, _C_WIN, _C_GQ, _C_GK, _C_GV, _C_GR, _C_XQ, _C_SM, _C_END = (
    0, 512, 768, 1024, 1280, 1536, 1792, 2304, 2816, 3328, 3456)


def _pack_w_in(w_in):
    offs = np.cumsum([0, NSA_HEADS * NSA_HD, 3 * KV_SLAB, _NG, GLA_HEADS * GLA_DK, GLA_HEADS * GLA_DK,
                      GLA_HEADS * GLA_DV, GLA_RANK, GLA_HEADS * GLA_DV, X_HEADS * X_HD])
    q, kv, ng, gq, gk, gv, glr, gr, xq = [w_in[:, offs[i]:offs[i + 1]] for i in range(9)]
    pad = jnp.zeros((D_MODEL, LANE - _NG - GLA_RANK), w_in.dtype)
    return jnp.concatenate([q, kv, gq, gk, gv, gr, xq, ng, glr, pad], axis=1).astype(BF16)


def _in_proj_kernel(h_ref, g_ref, w_ref, w2_ref, b2_ref,
                    u_ref, q_ref, cmp_ref, slc_ref, win_ref, ng_ref, gq_ref, gk_ref, gv_ref, la_ref, gr_ref,
                    xq_ref):
    u = _rms(h_ref[...], g_ref[...]).astype(BF16)
    u_ref[...] = u

    def proj(lo, hi):
        return _dot(u, w_ref[:, lo:hi])

    q_ref[...] = proj(_C_Q, _C_CMP) * (NSA_HD ** -0.5)
    cmp_ref[...] = proj(_C_CMP, _C_SLC)
    slc_ref[...] = proj(_C_SLC, _C_WIN)
    win_ref[...] = proj(_C_WIN, _C_GQ)
    gq_ref[...] = proj(_C_GQ, _C_GK) * (GLA_DK ** -0.5)
    gk_ref[...] = proj(_C_GK, _C_GV)
    gv_ref[...] = proj(_C_GV, _C_GR)
    gr_ref[...] = proj(_C_GR, _C_XQ)
    xq_ref[...] = proj(_C_XQ, _C_SM) * (X_HD ** -0.5)
    small = proj(_C_SM, _C_END)
    ng_ref[...] = jax.nn.sigmoid(small)
    z = _dot(small.astype(BF16), w2_ref[...]) + b2_ref[...]
    log_sig = jnp.minimum(z, 0.0) - jnp.log1p(jnp.exp(-jnp.abs(z)))
    la_ref[...] = log_sig / GLA_TAU


def _in_proj(h, g_pre_mix, w_cat, w2_pad, b2, tm):
    rows = h.shape[0]
    widths = [(D_MODEL, BF16), (512, F32), (256, F32), (256, F32), (256, F32), (LANE, F32), (256, F32),
              (256, F32), (512, F32), (256, F32), (512, F32), (512, F32)]
    return pl.pallas_call(
        _in_proj_kernel,
        grid=(rows // tm,),
        in_specs=[
            pl.BlockSpec((tm, D_MODEL), lambda i: (i, 0)),
            pl.BlockSpec((1, D_MODEL), lambda i: (0, 0)),
            pl.BlockSpec((D_MODEL, _C_END), lambda i: (0, 0)),
            pl.BlockSpec((LANE, 256), lambda i: (0, 0)),
            pl.BlockSpec((1, 256), lambda i: (0, 0)),
        ],
        out_specs=[pl.BlockSpec((tm, w), lambda i: (i, 0)) for w, _ in widths],
        out_shape=[jax.ShapeDtypeStruct((rows, w), dt) for w, dt in widths],
        compiler_params=_cparams(("parallel",)),
        name="in_proj",
    )(h, g_pre_mix, w_cat, w2_pad, b2)


def _pack_cmp_w1(w1):
    wa, wb = w1[:CMP_STRIDE], w1[CMP_STRIDE:]
    z = jnp.zeros_like(wa)
    rows = []
    for g in range(NSA_KV):
        cols = []
        for half in (wa, wb):
            for g2 in range(NSA_KV):
                cols.append(half if g2 == g else z)
        rows.append(jnp.concatenate(cols, axis=2))
    w = jnp.stack(rows, axis=1)
    return w.reshape(CMP_STRIDE // 2, 2 * NSA_KV * NSA_HD, 4 * CMP_HID).astype(BF16)


def _pack_cmp_w2(w2):
    z = jnp.zeros_like(w2)
    return jnp.concatenate([jnp.concatenate([w2, z], axis=1), jnp.concatenate([z, w2], axis=1)], axis=0).astype(BF16)


def _compress_ab(load_pos, wk_ref, wv_ref):
    hk = hv = None
    for lp in range(CMP_STRIDE // 2):
        xa = load_pos(2 * lp)
        xb = load_pos(2 * lp + 1)
        lk = jnp.concatenate([xa[:, :LANE], xb[:, :LANE]], axis=1).astype(BF16)
        lv = jnp.concatenate([xa[:, LANE:], xb[:, LANE:]], axis=1).astype(BF16)
        dk = _dot(lk, wk_ref[lp])
        dv = _dot(lv, wv_ref[lp])
        hk = dk if hk is None else hk + dk
        hv = dv if hv is None else hv + dv
    return hk, hv


def _compress_finish(ab, bias, w2_ref, shift_ref):
    n = ab.shape[0]
    shift_ref[0:n, :] = ab[:, 2 * CMP_HID:]
    shift_ref[n:n + 8, :] = jnp.zeros((8, 2 * CMP_HID), F32)
    h = ab[:, :2 * CMP_HID] + shift_ref[pl.ds(1, n), :] + bias
    return _dot(_silu(h).astype(BF16), w2_ref[...])


def _cmp_bias_kernel(pe_ref, w1_ref, o_ref):
    for kv in range(2):
        b = _dot(pe_ref[kv].astype(BF16), w1_ref[kv])
        o_ref[kv] = jnp.concatenate([b, b], axis=1)


def _cmp_bias(pe_cmp, w_cmp1):
    pe = jnp.broadcast_to(pe_cmp.reshape(2, 1, CMP_LEN * NSA_HD), (2, 8, CMP_LEN * NSA_HD))
    w1 = w_cmp1.reshape(2, CMP_LEN * NSA_HD, CMP_HID).astype(BF16)
    return pl.pallas_call(
        _cmp_bias_kernel,
        out_shape=jax.ShapeDtypeStruct((2, 8, 2 * CMP_HID), F32),
        name="cmp_bias",
    )(pe, w1)


def _nsa_prep_kernel(cmpv_ref, slc_ref, win_ref, wk_ref, wv_ref,
                     abk_ref, abv_ref, kslc_ref, vslct_ref, kwin_ref, vwint_ref):
    abk, abv = _compress_ab(lambda l: cmpv_ref[:, l * KV_SLAB:(l + 1) * KV_SLAB], wk_ref, wv_ref)
    abk_ref[...] = abk
    abv_ref[...] = abv
    nchunk = slc_ref.shape[0] // LANE
    ones_row = (lax.broadcasted_iota(jnp.int32, (VT_ROWS - NSA_HD, LANE), 0) == 0).astype(F32)
    for src, k_ref, vt_ref in ((slc_ref, kslc_ref, vslct_ref), (win_ref, kwin_ref, vwint_ref)):
        k_ref[...] = src[:, :LANE].astype(BF16)
        for c in range(nchunk):
            vt = src[c * LANE:(c + 1) * LANE, LANE:].T
            vt_ref[c] = jnp.concatenate([vt[:NSA_HD], ones_row, vt[NSA_HD:], ones_row], axis=0).astype(BF16)


def _nsa_prep(kv_cmp, kv_slc, kv_win, wk2, wv2, tm):
    t = kv_cmp.shape[0]
    nsub = t // CMP_STRIDE
    cmp_view = kv_cmp.reshape(nsub, CMP_STRIDE * KV_SLAB)
    const3 = lambda i: (0, 0, 0)
    return pl.pallas_call(
        _nsa_prep_kernel,
        grid=(t // tm,),
        in_specs=[
            pl.BlockSpec((tm // CMP_STRIDE, CMP_STRIDE * KV_SLAB), lambda i: (i, 0)),
            pl.BlockSpec((tm, KV_SLAB), lambda i: (i, 0)),
            pl.BlockSpec((tm, KV_SLAB), lambda i: (i, 0)),
            pl.BlockSpec(wk2.shape, const3),
            pl.BlockSpec(wv2.shape, const3),
        ],
        out_specs=[
            pl.BlockSpec((tm // CMP_STRIDE, 4 * CMP_HID), lambda i: (i, 0)),
            pl.BlockSpec((tm // CMP_STRIDE, 4 * CMP_HID), lambda i: (i, 0)),
            pl.BlockSpec((tm, LANE), lambda i: (i, 0)),
            pl.BlockSpec((tm // LANE, NSA_KV * VT_ROWS, LANE), lambda i: (i, 0, 0)),
            pl.BlockSpec((tm, LANE), lambda i: (i, 0)),
            pl.BlockSpec((tm // LANE, NSA_KV * VT_ROWS, LANE), lambda i: (i, 0, 0)),
        ],
        out_shape=[
            jax.ShapeDtypeStruct((nsub, 4 * CMP_HID), F32),
            jax.ShapeDtypeStruct((nsub, 4 * CMP_HID), F32),
            jax.ShapeDtypeStruct((t, LANE), BF16),
            jax.ShapeDtypeStruct((t // LANE, NSA_KV * VT_ROWS, LANE), BF16),
            jax.ShapeDtypeStruct((t, LANE), BF16),
            jax.ShapeDtypeStruct((t // LANE, NSA_KV * VT_ROWS, LANE), BF16),
        ],
        compiler_params=_cparams(("parallel",)),
        name="nsa_prep",
    )(cmp_view, kv_slc, kv_win, wk2, wv2)


def _cmp_finish_kernel(abk_ref, abv_ref, bias_ref, w2k_ref, w2v_ref, kc_ref, vct_ref, shift_ref):
    n = abk_ref.shape[0]
    kc_ref[...] = _compress_finish(abk_ref[...], bias_ref[0, 0:1, :], w2k_ref, shift_ref).astype(BF16)
    vc = _compress_finish(abv_ref[...], bias_ref[1, 0:1, :], w2v_ref, shift_ref)
    for c in range(n // LANE):
        vct_ref[:, c * LANE:(c + 1) * LANE] = vc[c * LANE:(c + 1) * LANE, :].T.astype(BF16)


def _cmp_finish(abk, abv, bias, w2k, w2v):
    n = abk.shape[0]
    return pl.pallas_call(
        _cmp_finish_kernel,
        out_shape=[jax.ShapeDtypeStruct((n, LANE), BF16), jax.ShapeDtypeStruct((LANE, n), BF16)],
        scratch_shapes=[pltpu.VMEM((n + 8, 2 * CMP_HID), F32)],
        compiler_params=pltpu.CompilerParams(vmem_limit_bytes=VMEM_LIMIT),
        name="cmp_finish",
    )(abk, abv, bias, w2k, w2v)


def _select_blocks(score, n_top):
    nb = score.shape[0]
    jj = lax.broadcasted_iota(jnp.int32, score.shape, 0)
    taken = jnp.float32(-3e38)
    for _ in range(n_top):
        m = jnp.max(score, axis=0, keepdims=True)
        idx = jnp.min(jnp.where(score == m, jj, nb), axis=0, keepdims=True)
        score = jnp.where(jj == idx, taken, score)
    return (score == taken).astype(F32)


def _block_scores(imp, jj, cur, n_real):
    s = jnp.where(jj == 0, FORCE, jnp.where(jj == cur, FORCE, jnp.where(jj == cur - 1, FORCE, imp)))
    s = jnp.where(jj <= cur, s, NEG)
    return jnp.where(jj < n_real, s, -jnp.inf)


def _cmp_to_slc_t(nc_pad, nb_pad):
    i = np.arange(nc_pad)[None, :]
    j = np.arange(nb_pad)[:, None]
    ov = (i * CMP_STRIDE < (j + 1) * SLC_BLK) & (i * CMP_STRIDE + CMP_LEN > j * SLC_BLK)
    return jnp.asarray(ov.astype(np.float32), dtype=BF16)


TQ = LANE


SLC_CK = 8 * TQ


def _flash_step(qbd, k, vt, bias, m_ref, acc_ref):
    s = _dot(k, qbd)
    for g in range(NSA_KV):
        ps = []
        for h in range(NSA_HPG):
            lo = (g * NSA_HPG + h) * TQ
            sc = s[:, lo:lo + TQ] + bias[g]
            m_old = m_ref[:, lo:lo + TQ]
            m_new = jnp.maximum(m_old, jnp.max(sc, axis=0, keepdims=True))
            ps.append(jnp.exp2(sc - m_new).astype(BF16))
            m_ref[:, lo:lo + TQ] = m_new
            acc_ref[g, :, h * TQ:(h + 1) * TQ] = acc_ref[g, :, h * TQ:(h + 1) * TQ] * jnp.exp2(m_old - m_new)
        acc_ref[g] += _dot(vt[g * VT_ROWS:(g + 1) * VT_ROWS, :], jnp.concatenate(ps, axis=1))


def _softmax_once(qbd, k, vt, bias):
    s = _dot(k, qbd)
    outs = []
    for g in range(NSA_KV):
        ps = []
        for h in range(NSA_HPG):
            lo = (g * NSA_HPG + h) * TQ
            sc = s[:, lo:lo + TQ] + bias
            ps.append(jnp.exp2(sc - jnp.max(sc, axis=0, keepdims=True)).astype(BF16))
        o = _dot(vt[g * VT_ROWS:(g + 1) * VT_ROWS, :], jnp.concatenate(ps, axis=1))
        outs.append(o[:NSA_HD] * (1.0 / o[NSA_HD:NSA_HD + 1]))
    return outs


def _flash_reset(m_ref, acc_ref):
    m_ref[...] = jnp.full(m_ref.shape, NEG, F32)
    acc_ref[...] = jnp.zeros(acc_ref.shape, F32)


def _flash_result(acc_ref):
    return [acc_ref[g, :NSA_HD, :] * (1.0 / acc_ref[g, NSA_HD:NSA_HD + 1, :]) for g in range(NSA_KV)]


def _nsa_prompt_kernel(q_ref, ng_ref, kc_ref, vct_ref, indt_ref, kslc_ref, vslct_ref, kwin_ref, vwint_ref,
                       o_ref, selb_ref, m_ref, acc_ref):
    i = pl.program_id(0)
    s0 = i * TQ
    ncp = kc_ref.shape[0]
    nbp = indt_ref.shape[0]
    n_blocks = kslc_ref.shape[0] // SLC_BLK

    qt = jnp.concatenate([q_ref[:, c * LANE:(c + 1) * LANE].T for c in range(BRANCH_W // LANE)], axis=0)
    zero = jnp.zeros((NSA_HD, NSA_HPG * TQ), F32)
    rows = []
    for g in range(NSA_KV):
        blk = jnp.concatenate(
            [qt[(g * NSA_HPG + h) * NSA_HD:(g * NSA_HPG + h + 1) * NSA_HD, :] for h in range(NSA_HPG)], axis=1)
        rows.append(jnp.concatenate([blk, zero] if g == 0 else [zero, blk], axis=1))
    qbd = (jnp.concatenate(rows, axis=0) * LOG2E).astype(BF16)

    q_lane = s0 + lax.broadcasted_iota(jnp.int32, (1, TQ), 1)

    n_win = WINDOW + TQ
    w0 = pl.multiple_of(jnp.maximum(s0 - WINDOW, 0), TQ)
    dist = q_lane - (w0 + lax.broadcasted_iota(jnp.int32, (n_win, TQ), 0))
    bias_w = jnp.where(dist >= 0, jnp.where(dist < WINDOW, 0.0, NEG), NEG)
    vt_w = jnp.concatenate([vwint_ref[w0 // TQ + j] for j in range(n_win // TQ)], axis=1)
    o_win = _softmax_once(qbd, kwin_ref[pl.ds(w0, n_win), :], vt_w, bias_w)

    c_end = lax.broadcasted_iota(jnp.int32, (ncp, TQ), 0) * CMP_STRIDE + (CMP_LEN - 1)
    bias_c = jnp.where(c_end <= q_lane, 0.0, NEG)
    col_ok = q_lane >= CMP_LEN - 1
    jj = lax.broadcasted_iota(jnp.int32, (nbp, TQ), 0)
    cur = q_lane // SLC_BLK
    o_cmp, scores = [], []
    for g in range(NSA_KV):
        sc_all = _dot(kc_ref[...], qbd[:, g * NSA_HPG * TQ:(g + 1) * NSA_HPG * TQ])
        psum = jnp.zeros((ncp, TQ), F32)
        ps = []
        for h in range(NSA_HPG):
            sc = sc_all[:, h * TQ:(h + 1) * TQ] + bias_c
            p = jnp.exp2(sc - jnp.max(sc, axis=0, keepdims=True))
            pn = p * jnp.where(col_ok, 1.0 / jnp.sum(p, axis=0, keepdims=True), 0.0)
            psum = psum + pn
            ps.append(pn.astype(BF16))
        o_cmp.append(_dot(vct_ref[g * NSA_HD:(g + 1) * NSA_HD, :], jnp.concatenate(ps, axis=1)))
        imp = sum(_dot(indt_ref[...], part) for part in _split3(psum))
        scores.append(_block_scores(imp, jj, cur, n_blocks))
    sel = _select_blocks(jnp.concatenate(scores, axis=1), min(SLC_TOPN, n_blocks))
    for g in range(NSA_KV):
        selb_ref[g] = (sel[:, g * TQ:(g + 1) * TQ] - 1.0) * (-NEG)

    blk_per_step = SLC_CK // SLC_BLK

    def slc_inputs(c):
        k = kslc_ref[pl.ds(pl.multiple_of(c * SLC_CK, SLC_CK), SLC_CK), :]
        vt = jnp.concatenate([vslct_ref[c * (SLC_CK // TQ) + j] for j in range(SLC_CK // TQ)], axis=1)
        bias = []
        for g in range(NSA_KV):
            rows = selb_ref[g, pl.ds(pl.multiple_of(c * blk_per_step, blk_per_step), blk_per_step), :]
            bias.append(jnp.concatenate(
                [jnp.broadcast_to(rows[j:j + 1, :], (SLC_BLK, TQ)) for j in range(blk_per_step)], axis=0))
        return k, vt, bias

    _flash_reset(m_ref, acc_ref)

    def slc_step(c):
        k, vt, bias = slc_inputs(c)
        _flash_step(qbd, k, vt, bias, m_ref, acc_ref)

    def slc_pair(t, carry):
        slc_step(2 * t)
        slc_step(2 * t + 1)
        return carry

    c_last = s0 // SLC_CK
    lax.fori_loop(0, c_last // 2, slc_pair, 0)

    @pl.when(c_last % 2 == 1)
    def _():
        slc_step(c_last - 1)
    k, vt, bias = slc_inputs(c_last)
    key_pos = c_last * SLC_CK + lax.broadcasted_iota(jnp.int32, (SLC_CK, TQ), 0)
    causal = key_pos <= q_lane
    _flash_step(qbd, k, vt, [jnp.where(causal, b, NEG) for b in bias], m_ref, acc_ref)
    o_slc = _flash_result(acc_ref)

    ngt = ng_ref[...].T
    for g in range(NSA_KV):
        heads = []
        for h in range(NSA_HPG):
            r = (g * NSA_HPG + h) * 3
            cs = slice(h * TQ, (h + 1) * TQ)
            heads.append(ngt[r:r + 1, :] * o_cmp[g][:, cs] + ngt[r + 1:r + 2, :] * o_slc[g][:, cs]
                         + ngt[r + 2:r + 3, :] * o_win[g][:, cs])
        for pair in range(NSA_HPG // 2):
            blk = jnp.concatenate(heads[2 * pair:2 * pair + 2], axis=0).T
            col = (g * NSA_HPG // 2 + pair) * LANE
            o_ref[:, col:col + LANE] = blk


def _nsa_prompt(q, ng, kc, vct, indt, kslc, vslct, kwin, vwint):
    t = q.shape[0]
    full2 = lambda i: (0, 0)
    full3 = lambda i: (0, 0, 0)
    return pl.pallas_call(
        _nsa_prompt_kernel,
        grid=(t // TQ,),
        in_specs=[
            pl.BlockSpec((TQ, BRANCH_W), lambda i: (i, 0)),
            pl.BlockSpec((TQ, LANE), lambda i: (i, 0)),
            pl.BlockSpec(kc.shape, full2),
            pl.BlockSpec(vct.shape, full2),
            pl.BlockSpec(indt.shape, full2),
            pl.BlockSpec(kslc.shape, full2),
            pl.BlockSpec(vslct.shape, full3),
            pl.BlockSpec(kwin.shape, full2),
            pl.BlockSpec(vwint.shape, full3),
        ],
        out_specs=pl.BlockSpec((TQ, BRANCH_W), lambda i: (i, 0)),
        out_shape=jax.ShapeDtypeStruct((t, BRANCH_W), F32),
        scratch_shapes=[
            pltpu.VMEM((NSA_KV, indt.shape[0], TQ), F32),
            pltpu.VMEM((1, NSA_HEADS * TQ), F32),
            pltpu.VMEM((NSA_KV, VT_ROWS, NSA_HPG * TQ), F32),
        ],
        compiler_params=_cparams(("arbitrary",)),
        name="nsa_prompt",
    )(q, ng, kc, vct, indt, kslc, vslct, kwin, vwint)


def _nsa_prompt_path(q, kv_cmp, kv_slc, kv_win, ng, w_cmp1, pe_cmp, w_cmp2):
    t = q.shape[0]
    wk2, wv2 = _pack_cmp_w1(w_cmp1[0]), _pack_cmp_w1(w_cmp1[1])
    w2k, w2v = _pack_cmp_w2(w_cmp2[0]), _pack_cmp_w2(w_cmp2[1])
    bias = _cmp_bias(pe_cmp, w_cmp1)
    abk, abv, kslc, vslct, kwin, vwint = _nsa_prep(kv_cmp, kv_slc, kv_win, wk2, wv2, min(t, 1024))
    kc, vct = _cmp_finish(abk, abv, bias, w2k, w2v)
    indt = _cmp_to_slc_t(t // CMP_STRIDE, t // SLC_BLK)
    return _nsa_prompt(q, ng, kc, vct, indt, kslc, vslct, kwin, vwint)


def _gla_kernel(gq_ref, gk_ref, gv_ref, la_ref, s0_ref, o_ref, sfin_ref, st_ref, *, chunk, nchunk, exact):
    j = pl.program_id(1)
    n_pair = GLA_HEADS // 2
    nseq = gq_ref.shape[0]

    @pl.when(j == 0)
    def _():
        for s in range(nseq):
            for p in range(n_pair):
                st_ref[s, p] = s0_ref[s, p].T

    if exact:
        hi = lax.Precision.HIGHEST
        mx = lambda x: x
        dot = lambda a, b: jnp.dot(a, b, precision=hi, preferred_element_type=F32)
        dot_nt = lambda a, b: lax.dot_general(a, b, (((1,), (1,)), ((), ())), precision=hi,
                                              preferred_element_type=F32)
        dot_tn = lambda a, b: lax.dot_general(a, b, (((0,), (0,)), ((), ())), precision=hi,
                                              preferred_element_type=F32)
    else:
        mx = lambda x: x.astype(BF16)
        dot, dot_nt, dot_tn = _dot, _dot_nt, _dot_tn

    row = lax.broadcasted_iota(jnp.int32, (chunk, chunk), 0)
    col = lax.broadcasted_iota(jnp.int32, (chunk, chunk), 1)
    tri_f = (row >= col).astype(F32)
    row2 = lax.broadcasted_iota(jnp.int32, (2 * chunk, 2 * chunk), 0)
    col2 = lax.broadcasted_iota(jnp.int32, (2 * chunk, 2 * chunk), 1)
    tri2 = row2 % chunk >= col2 % chunk
    own_lanes = (lax.broadcasted_iota(jnp.int32, (2 * chunk, LANE), 1) // GLA_DK
                 == lax.broadcasted_iota(jnp.int32, (2 * chunk, LANE), 0) // chunk)

    def stack_heads(x):
        return mx(jnp.where(own_lanes, jnp.concatenate([x, x], axis=0), 0.0))

    for s in range(nseq):
        carry = []
        for ci in range(nchunk):
            rows = slice(ci * chunk, (ci + 1) * chunk)
            la = la_ref[s, rows, :]
            if exact:
                b = dot(tri_f, la)
            else:
                b = sum(_dot(tri_f.astype(BF16), part) for part in _split3(la))
            b_end = b[chunk - 1:chunk, :]
            k = gk_ref[s, rows, :]
            qe = gq_ref[s, rows, :] * jnp.exp(b)
            ke = k * jnp.exp(-b)
            kend = k * jnp.exp(b_end - b)
            per_pair = []
            for p in range(n_pair):
                sl = slice(p * LANE, (p + 1) * LANE)
                q2 = stack_heads(qe[:, sl])
                att = jnp.where(tri2, dot_nt(q2, stack_heads(ke[:, sl])), 0.0)
                v2 = mx(jnp.concatenate([gv_ref[s, rows, (2 * p + hh) * GLA_DV:(2 * p + hh + 1) * GLA_DV]
                                         for hh in range(2)], axis=0))
                o2 = dot(mx(att), v2)
                for hh in range(2):
                    o_ref[s, rows, (2 * p + hh) * GLA_DV:(2 * p + hh + 1) * GLA_DV] = o2[hh * chunk:(hh + 1) * chunk]
                per_pair.append((jnp.exp(b_end[:, sl]), dot_tn(v2, stack_heads(kend[:, sl])), q2))
            carry.append(per_pair)

        for p in range(n_pair):
            st = st_ref[s, p]
            for ci in range(nchunk):
                rows = slice(ci * chunk, (ci + 1) * chunk)
                decay, kv_sum, q2 = carry[ci][p]
                inter = dot_nt(q2, mx(st))
                for hh in range(2):
                    o_ref[s, rows, (2 * p + hh) * GLA_DV:(2 * p + hh + 1) * GLA_DV] += inter[hh * chunk:(hh + 1) * chunk]
                st = st * decay + kv_sum
            st_ref[s, p] = st

    @pl.when(j == pl.num_programs(1) - 1)
    def _():
        for s in range(nseq):
            for p in range(n_pair):
                sfin_ref[s, p] = st_ref[s, p].T


def _gla(gq, gk, gv, la, s0, chunk, nchunk, exact, sb):
    bsz, t, _ = gq.shape
    rb = chunk * nchunk
    row_spec = lambda w: pl.BlockSpec((sb, rb, w), lambda b, j: (b, j, 0))
    st_spec = pl.BlockSpec((sb, GLA_HEADS // 2, LANE, LANE), lambda b, j: (b, 0, 0, 0))
    return pl.pallas_call(
        functools.partial(_gla_kernel, chunk=chunk, nchunk=nchunk, exact=exact),
        grid=(bsz // sb, t // rb),
        in_specs=[row_spec(256), row_spec(256), row_spec(512), row_spec(256), st_spec],
        out_specs=[row_spec(512), st_spec],
        out_shape=[jax.ShapeDtypeStruct((bsz, t, BRANCH_W), F32),
                   jax.ShapeDtypeStruct((bsz, GLA_HEADS // 2, LANE, LANE), F32)],
        scratch_shapes=[pltpu.VMEM((sb, GLA_HEADS // 2, LANE, LANE), F32)],
        compiler_params=_cparams(("parallel", "arbitrary")),
        name="gla",
    )(gq, gk, gv, la, s0)


def _mem_proj_kernel(mem_ref, g_ref, w_ref, o_ref):
    o_ref[...] = _dot(_rms(mem_ref[...], g_ref[...]).astype(BF16), w_ref[...])


def _mem_proj(mem, g_mem, w_mem_kv):
    return pl.pallas_call(
        _mem_proj_kernel,
        out_shape=jax.ShapeDtypeStruct((mem.shape[0], w_mem_kv.shape[1]), F32),
        compiler_params=pltpu.CompilerParams(vmem_limit_bytes=VMEM_LIMIT),
        name="mem_proj",
    )(mem, g_mem, w_mem_kv)


def _xattn_kernel(xq_ref, mem_ref, o_ref):
    tm = xq_ref.shape[1]
    if tm * X_HEADS <= LANE:
        lane_head = lax.broadcasted_iota(jnp.int32, (tm, BRANCH_W), 1) // X_HD
        for b in range(xq_ref.shape[0]):
            xq = xq_ref[b]
            qbd = jnp.concatenate([jnp.where(lane_head == h, xq, 0.0) for h in range(X_HEADS)], axis=0)
            s = _dot_nt(qbd.astype(BF16), mem_ref[b, :, :BRANCH_W].astype(BF16))
            p = jnp.exp(s - jnp.max(s, axis=-1, keepdims=True))
            p = p * (1.0 / jnp.sum(p, axis=-1, keepdims=True))
            o = _dot(p.astype(BF16), mem_ref[b, :, BRANCH_W:].astype(BF16))
            o_ref[b] = sum(jnp.where(lane_head == h, o[h * tm:(h + 1) * tm], 0.0) for h in range(X_HEADS))
        return
    for b in range(xq_ref.shape[0]):
        for h in range(X_HEADS):
            cs = slice(h * X_HD, (h + 1) * X_HD)
            k = mem_ref[b, :, cs].astype(BF16)
            v = mem_ref[b, :, BRANCH_W + h * X_HD:BRANCH_W + (h + 1) * X_HD].astype(BF16)
            s = _dot_nt(xq_ref[b, :, cs].astype(BF16), k)
            p = jnp.exp(s - jnp.max(s, axis=-1, keepdims=True))
            p = p * (1.0 / jnp.sum(p, axis=-1, keepdims=True))
            o_ref[b, :, cs] = _dot(p.astype(BF16), v)


def _xattn(xq, mem_kv, tm, sb):
    bsz, t, _ = xq.shape
    return pl.pallas_call(
        _xattn_kernel,
        grid=(bsz // sb, t // tm),
        in_specs=[pl.BlockSpec((sb, tm, BRANCH_W), lambda b, i: (b, i, 0)),
                  pl.BlockSpec((sb,) + mem_kv.shape[1:], lambda b, i: (b, 0, 0))],
        out_specs=pl.BlockSpec((sb, tm, BRANCH_W), lambda b, i: (b, i, 0)),
        out_shape=jax.ShapeDtypeStruct(xq.shape, F32),
        compiler_params=_cparams(("parallel", "parallel")),
        name="xattn",
    )(xq, mem_kv)


def _merge_kernel(u_ref, h_ref, onsa_ref, ogla_ref, gr_ref, ox_ref, ggla_ref, wb_ref, wmg_ref, bmg_ref, wout_ref,
                  gpost_ref, o_ref):
    u = u_ref[...]
    parts = []
    for h in range(GLA_HEADS):
        cs = slice(h * GLA_DV, (h + 1) * GLA_DV)
        parts.append(_rms(ogla_ref[:, cs], ggla_ref[...]) * _silu(gr_ref[:, cs]))
    branches = [onsa_ref[...], jnp.concatenate(parts, axis=1), ox_ref[...]]
    mixed = None
    for b in range(3):
        cs = slice(b * D_MODEL, (b + 1) * D_MODEL)
        gate = jax.nn.sigmoid(_dot(u, wmg_ref[:, cs]) + bmg_ref[:, cs])
        term = gate * _dot(branches[b].astype(BF16), wb_ref[b])
        mixed = term if mixed is None else mixed + term
    m = _dot(mixed.astype(BF16), wout_ref[...])
    o_ref[...] = h_ref[...] + _rms(m, gpost_ref[...])


def _merge(u, h, o_nsa, o_gla, gr, o_x, g_gla_out, w_branch, w_mg, b_mg, w_out, g_post, tm):
    rows = h.shape[0]
    row = lambda w: pl.BlockSpec((tm, w), lambda i: (i, 0))
    full = lambda a: pl.BlockSpec(a.shape, lambda i: (0,) * a.ndim)
    return pl.pallas_call(
        _merge_kernel,
        grid=(rows // tm,),
        in_specs=[row(D_MODEL), row(D_MODEL), row(BRANCH_W), row(BRANCH_W), row(BRANCH_W), row(BRANCH_W),
                  full(g_gla_out), full(w_branch), full(w_mg), full(b_mg), full(w_out), full(g_post)],
        out_specs=row(D_MODEL),
        out_shape=jax.ShapeDtypeStruct((rows, D_MODEL), F32),
        compiler_params=_cparams(("parallel",)),
        name="merge",
    )(u, h, o_nsa, o_gla, gr, o_x, g_gla_out, w_branch, w_mg, b_mg, w_out, g_post)


def _page_copies(pt_ref, pool_ref, buf_ref, sem_ref, seq, slot):
    npages = buf_ref.shape[-1] // PAGE_SIZE
    return [pltpu.make_async_copy(pool_ref.at[pt_ref[seq, p]],
                                  buf_ref.at[slot, :, :, :, pl.ds(p * PAGE_SIZE, PAGE_SIZE)], sem_ref.at[slot])
            for p in range(npages)]


def _gather_pages(pt_ref, pool_ref, buf_ref, sem_ref):
    b = pl.program_id(0)
    slot = b % 2

    @pl.when(b == 0)
    def _():
        for cp in _page_copies(pt_ref, pool_ref, buf_ref, sem_ref, 0, 0):
            cp.start()

    @pl.when(b + 1 < pl.num_programs(0))
    def _():
        for cp in _page_copies(pt_ref, pool_ref, buf_ref, sem_ref, b + 1, 1 - slot):
            cp.start()

    for cp in _page_copies(pt_ref, pool_ref, buf_ref, sem_ref, b, slot):
        cp.wait()
    return slot


def _row_token(shape):
    return lax.broadcasted_iota(jnp.int32, shape, 0) % 8


def _tap_permutation():
    r = np.arange(2 * PAGE_SIZE)
    l, jj = r // CMP_STRIDE, r % CMP_STRIDE
    perm = np.zeros((2 * PAGE_SIZE, 2 * PAGE_SIZE), np.float32)
    perm[r, CMP_STRIDE * jj + l] = 1.0
    return jnp.asarray(perm, dtype=BF16)


def _sample_cmp_kernel(pt_ref, pool_ref, qbd_ref, perm_ref, wk_ref, wv_ref, bias_ref, w2k_ref, w2v_ref, ind_ref,
                       oc_ref, imp_ref, buf_ref, sem_ref, taps_ref, shift_ref, *, past):
    slot = _gather_pages(pt_ref, pool_ref, buf_ref, sem_ref)
    span = 2 * PAGE_SIZE
    nspan = buf_ref.shape[-1] // span
    nsub = nspan * CMP_STRIDE

    perm = perm_ref[...]
    for q in range(nspan):
        x = buf_ref[slot, :, :, :, q * span:(q + 1) * span]
        t = _dot_nt(perm, x.reshape(KV_SLAB, span).astype(BF16)).astype(BF16)
        for l in range(CMP_STRIDE):
            taps_ref[l, q * CMP_STRIDE:(q + 1) * CMP_STRIDE, :] = t[l * CMP_STRIDE:(l + 1) * CMP_STRIDE, :]

    abk, abv = _compress_ab(lambda l: taps_ref[l], wk_ref, wv_ref)
    kc = _compress_finish(abk, bias_ref[0, 0:1, :], w2k_ref, shift_ref).astype(BF16)
    vc = _compress_finish(abv, bias_ref[1, 0:1, :], w2v_ref, shift_ref).astype(BF16)
    qbd = qbd_ref[0]
    s = _dot_nt(qbd, kc)
    q_pos = past + _row_token(s.shape)
    mask = lax.broadcasted_iota(jnp.int32, s.shape, 1) * CMP_STRIDE + (CMP_LEN - 1) <= q_pos
    s = jnp.where(mask, s, NEG)
    p = jnp.exp2(s - jnp.max(s, axis=-1, keepdims=True))
    pn = jnp.where(mask, p * (1.0 / jnp.sum(p, axis=-1, keepdims=True)), 0.0)
    oc_ref[0] = _dot(pn.astype(BF16), vc)
    rows_g = NSA_HPG * 8
    psum = jnp.concatenate(
        [sum(pn[g * rows_g + h * 8:g * rows_g + (h + 1) * 8, :] for h in range(NSA_HPG)) for g in range(NSA_KV)],
        axis=0)
    imp_ref[0] = sum(_dot(part, ind_ref[...]) for part in _split3(psum))


def _sample_cmp(page_table, pool_t, qbd, wk2, wv2, bias, w2k, w2v, ind):
    nseq, npages = page_table.shape
    assert npages % 2 == 0
    past = npages * PAGE_SIZE
    nsub = past // CMP_STRIDE
    perm = _tap_permutation()
    full = lambda a: pl.BlockSpec(a.shape, lambda b, pt: (0,) * a.ndim)
    grid_spec = pltpu.PrefetchScalarGridSpec(
        num_scalar_prefetch=1,
        grid=(nseq,),
        in_specs=[pl.BlockSpec(memory_space=pl.ANY),
                  pl.BlockSpec((1,) + qbd.shape[1:], lambda b, pt: (b, 0, 0)),
                  full(perm), full(wk2), full(wv2), full(bias), full(w2k), full(w2v), full(ind)],
        out_specs=[pl.BlockSpec((1, 64, LANE), lambda b, pt: (b, 0, 0)),
                   pl.BlockSpec((1, 16, ind.shape[1]), lambda b, pt: (b, 0, 0))],
        scratch_shapes=[pltpu.VMEM((2, 2, NSA_KV, NSA_HD, past), F32),
                        pltpu.SemaphoreType.DMA((2,)),
                        pltpu.VMEM((CMP_STRIDE, nsub, KV_SLAB), BF16),
                        pltpu.VMEM((nsub + 8, 2 * CMP_HID), F32)],
    )
    return pl.pallas_call(
        functools.partial(_sample_cmp_kernel, past=past),
        grid_spec=grid_spec,
        out_shape=[jax.ShapeDtypeStruct((nseq, 64, LANE), F32),
                   jax.ShapeDtypeStruct((nseq, 16, ind.shape[1]), F32)],
        compiler_params=_cparams(("arbitrary",)),
        name="sample_cmp",
    )(page_table, pool_t, qbd, perm, wk2, wv2, bias, w2k, w2v, ind)


def _sample_select_kernel(imp_ref, sel_ref, *, past, n_real, n_top):
    imp_t = imp_ref[...].T
    jj = lax.broadcasted_iota(jnp.int32, imp_t.shape, 0)
    tok = lax.broadcasted_iota(jnp.int32, (1, imp_t.shape[1]), 1) % 8
    cur = (past + tok) // SLC_BLK
    sel_ref[...] = _select_blocks(_block_scores(imp_t, jj, cur, n_real), n_top).T


def _sample_select(imp, past, n_real, tm):
    rows, nbp = imp.shape
    return pl.pallas_call(
        functools.partial(_sample_select_kernel, past=past, n_real=n_real, n_top=min(SLC_TOPN, n_real)),
        grid=(rows // tm,),
        in_specs=[pl.BlockSpec((tm, nbp), lambda i: (i, 0))],
        out_specs=pl.BlockSpec((tm, nbp), lambda i: (i, 0)),
        out_shape=jax.ShapeDtypeStruct((rows, nbp), F32),
        compiler_params=_cparams(("parallel",)),
        name="sample_select",
    )(imp)


def _biased_softmax_pv(parts):
    ms = [s + bias for s, bias, _, _ in parts]
    m = functools.reduce(jnp.maximum, [jnp.max(x, axis=-1, keepdims=True) for x in ms])
    ps = [jnp.exp2(x - m) for x in ms]
    l = sum(jnp.sum(p, axis=-1, keepdims=True) for p in ps)
    o = sum((_dot_nt if part[3] else _dot)(p.astype(BF16), part[2]) for p, part in zip(ps, parts))
    return o * (1.0 / l)


def _sample_attn_kernel(pt_ref, pool_ref, qbd_ref, sel_ref, expand_ref, newslc_ref, win_ref, newwin_ref,
                        newwint_ref, gate_ref, oc_ref, o_ref, wout_ref, buf_ref, sem_ref):
    slot = _gather_pages(pt_ref, pool_ref, buf_ref, sem_ref)
    npast = buf_ref.shape[-1]
    nblk_past = npast // SLC_BLK
    qbd = qbd_ref[0]
    rows = qbd.shape[0]
    tok = _row_token((rows, LANE))
    lane = lax.broadcasted_iota(jnp.int32, (rows, LANE), 1)

    kt = buf_ref[slot, 0].reshape(NSA_KV * NSA_HD, npast).astype(BF16)
    vt = buf_ref[slot, 1].reshape(NSA_KV * NSA_HD, npast).astype(BF16)
    sel = sel_ref[0]
    selrows = jnp.concatenate([sel[g * 8:(g + 1) * 8, :] for g in range(NSA_KV) for _ in range(NSA_HPG)], axis=0)
    selb = (selrows - 1.0) * (-NEG)
    bias_p = _dot(selb[:, :nblk_past].astype(BF16), expand_ref[...])
    bias_n = jnp.where(lane <= tok, selb[:, nblk_past:nblk_past + 1], NEG)
    knew = newslc_ref[0, :, 0:LANE].astype(BF16)
    vnew = newslc_ref[0, :, LANE:KV_SLAB].astype(BF16)
    o_slc = _biased_softmax_pv([(_dot(qbd, kt), bias_p, vt, True), (_dot_nt(qbd, knew), bias_n, vnew, False)])

    wbuf = win_ref.shape[-1]
    lane_w = lax.broadcasted_iota(jnp.int32, (rows, wbuf), 1)
    bias_w = jnp.where(lane_w + (WINDOW - wbuf) > _row_token((rows, wbuf)), 0.0, NEG)
    kwn = newwin_ref[0, :, 0:LANE].astype(BF16)
    vwn = newwin_ref[0, :, LANE:KV_SLAB].astype(BF16)
    o_win = _biased_softmax_pv([(_dot(qbd, win_ref[0, 0].astype(BF16)), bias_w, win_ref[0, 1].astype(BF16), True),
                                (_dot_nt(qbd, kwn), jnp.where(lane <= tok, 0.0, NEG), vwn, False)])

    g = gate_ref[0]
    o_ref[0] = g[:, 0:1] * oc_ref[0] + g[:, 1:2] * o_slc + g[:, 2:3] * o_win

    tail = lax.broadcasted_iota(jnp.int32, (NSA_KV * NSA_HD, LANE), 1) >= LANE - 8
    for kv in range(2):
        rolled = pltpu.roll(win_ref[0, kv], wbuf - 8, axis=1)
        wout_ref[0, kv, :, 0:wbuf - LANE] = rolled[:, 0:wbuf - LANE]
        wout_ref[0, kv, :, wbuf - LANE:wbuf] = jnp.where(tail, newwint_ref[0, kv], rolled[:, wbuf - LANE:wbuf])


def _sample_attn(page_table, pool_t, qbd, sel, expand, newslc, win_t, newwin, newwin_t, gates, o_cmp):
    nseq, npages = page_table.shape
    per_seq = lambda a: pl.BlockSpec((1,) + a.shape[1:], lambda b, pt: (b,) + (0,) * (a.ndim - 1))
    grid_spec = pltpu.PrefetchScalarGridSpec(
        num_scalar_prefetch=1,
        grid=(nseq,),
        in_specs=[pl.BlockSpec(memory_space=pl.ANY), per_seq(qbd), per_seq(sel),
                  pl.BlockSpec(expand.shape, lambda b, pt: (0, 0)),
                  per_seq(newslc), per_seq(win_t), per_seq(newwin), per_seq(newwin_t), per_seq(gates),
                  per_seq(o_cmp)],
        out_specs=[per_seq(o_cmp), per_seq(win_t)],
        scratch_shapes=[pltpu.VMEM((2, 2, NSA_KV, NSA_HD, npages * PAGE_SIZE), F32),
                        pltpu.SemaphoreType.DMA((2,))],
    )
    return pl.pallas_call(
        _sample_attn_kernel,
        grid_spec=grid_spec,
        out_shape=[jax.ShapeDtypeStruct(o_cmp.shape, F32), jax.ShapeDtypeStruct(win_t.shape, F32)],
        compiler_params=_cparams(("arbitrary",)),
        name="sample_attn",
    )(page_table, pool_t, qbd, sel, expand, newslc, win_t, newwin, newwin_t, gates, o_cmp)


def _nsa_sample_path(q, kv_cmp, kv_slc, kv_win, ng, pool_cmp, pool_slc, win_buf, page_table, w_cmp1, pe_cmp, w_cmp2):
    nseq, npages = page_table.shape
    pos_minor = lambda a: jnp.transpose(a, (0, 2, 3, 4, 1))
    wbuf = win_buf.shape[1]
    win_t = pos_minor(win_buf).reshape(nseq, 2, NSA_KV * NSA_HD, wbuf)
    tn = q.shape[0] // nseq
    past = npages * PAGE_SIZE
    n_real = -(-(past + tn) // SLC_BLK)
    nbp = -(-n_real // (2 * LANE)) * (2 * LANE)
    nsub = past // CMP_STRIDE
    assert tn == 8 and past % SLC_BLK == 0 and (past + tn - CMP_LEN) // CMP_STRIDE + 1 == nsub - 1

    qg = q.reshape(nseq, tn, NSA_KV, NSA_HPG, NSA_HD).transpose(0, 2, 3, 1, 4).reshape(nseq, NSA_KV, NSA_HPG * tn,
                                                                                        NSA_HD)
    z = jnp.zeros_like(qg[:, 0])
    qbd = jnp.stack([jnp.concatenate([qg[:, 0], z], axis=-1), jnp.concatenate([z, qg[:, 1]], axis=-1)], axis=1)
    qbd = (qbd.reshape(nseq, NSA_HEADS * tn, LANE) * LOG2E).astype(BF16)
    gates = ng[:, :_NG].reshape(nseq, tn, NSA_HEADS, 3).transpose(0, 2, 1, 3).reshape(nseq, NSA_HEADS * tn, 3)
    gates = jnp.pad(gates, ((0, 0), (0, 0), (0, LANE - 3)))

    def new_block(kv):
        return jnp.pad(kv.reshape(nseq, tn, KV_SLAB), ((0, 0), (0, LANE - tn), (0, 0)))

    wk2, wv2 = _pack_cmp_w1(w_cmp1[0]), _pack_cmp_w1(w_cmp1[1])
    w2k, w2v = _pack_cmp_w2(w_cmp2[0]), _pack_cmp_w2(w_cmp2[1])
    bias = _cmp_bias(pe_cmp, w_cmp1)
    ind = _cmp_to_slc_t(nsub, nbp).T
    o_cmp, imp = _sample_cmp(page_table, pos_minor(pool_cmp), qbd, wk2, wv2, bias, w2k, w2v, ind)
    sel = _sample_select(imp.reshape(nseq * 2 * tn, nbp), past, n_real, 512).reshape(nseq, 2 * tn, nbp)
    nblk_past = past // SLC_BLK
    expand = jnp.asarray(np.repeat(np.eye(nblk_past, dtype=np.float32), SLC_BLK, axis=1), dtype=BF16)
    newwin_t = jnp.pad(kv_win.reshape(nseq, tn, KV_SLAB).transpose(0, 2, 1), ((0, 0), (0, 0), (LANE - tn, 0)))
    newwin_t = newwin_t.reshape(nseq, 2, NSA_KV * NSA_HD, LANE)
    o, win_new_t = _sample_attn(page_table, pos_minor(pool_slc), qbd, sel, expand, new_block(kv_slc), win_t,
                                new_block(kv_win), newwin_t, gates, o_cmp)
    o = o.reshape(nseq, NSA_KV, NSA_HPG, tn, NSA_KV, NSA_HD)
    o = jnp.stack([o[:, g, :, :, g, :] for g in range(NSA_KV)], axis=1)
    win_new = jnp.transpose(win_new_t.reshape(nseq, 2, NSA_KV, NSA_HD, wbuf), (0, 4, 1, 2, 3))
    return o.transpose(0, 3, 1, 2, 4).reshape(nseq * tn, BRANCH_W), win_new


def _mixer(h, mem_kv, nsa_fn, gla_s0, gla_cfg, wts, tm, x_tm):
    bsz, t, _ = h.shape
    rows = bsz * t
    h2 = h.reshape(rows, D_MODEL)
    (u, q, kcmp, kslc, kwin, ng, gq, gk, gv, la, gr, xq) = _in_proj(h2, wts['g_pre_mix'], wts['w_cat'], wts['w2_pad'],
                                                                   wts['b_gla_g'], tm)
    o_nsa, extra = nsa_fn(q, kcmp, kslc, kwin, ng)
    r3 = lambda a: a.reshape(bsz, t, a.shape[-1])
    chunk, nchunk, exact, sb = gla_cfg
    o_gla, s_fin = _gla(r3(gq), r3(gk), r3(gv), r3(la), gla_s0.reshape(bsz, GLA_HEADS // 2, LANE, LANE), chunk,
                        nchunk, exact, sb)
    o_x = _xattn(r3(xq), mem_kv, x_tm, sb)
    h_new = _merge(u, h2, o_nsa, o_gla.reshape(rows, BRANCH_W), gr, o_x.reshape(rows, BRANCH_W), wts['g_gla_out'],
                   wts['w_branch'], wts['w_mg'], wts['b_mg'], wts['w_out'], wts['g_post_mix'], tm)
    s_fin = s_fin.reshape(bsz, GLA_HEADS, GLA_DK, GLA_DV)
    return h_new.reshape(bsz, t, D_MODEL), (kcmp, kslc, kwin), s_fin, extra


def kernel(x_prompt, x_sample, mem_prompt, cache_cmp_kv, cache_slc_kv, cache_win_kv, state_gla, cache_mem_kv,
           page_table, g_pre_ff1, g_post_ff1, w_ff1_gu, w_ff1_d, g_pre_mix, g_post_mix, w_in, w_cmp1, pe_cmp,
           w_cmp2, w_gla_g2, b_gla_g, g_gla_out, g_mem, w_mem_kv, w_branch, w_merge_gate, b_merge_gate, w_out,
           g_pre_ff2, g_post_ff2, w_ff2_gu, w_ff2_d):
    depth = w_in.shape[0]
    assert depth == 1
    l = 0
    bp, tp, _ = x_prompt.shape
    bs, ts, _ = x_sample.shape
    assert bp == 1
    row1 = lambda a: a.reshape(1, -1)
    wts = dict(
        g_pre_mix=row1(g_pre_mix[l]), w_cat=_pack_w_in(w_in[l]),
        w2_pad=jnp.zeros((LANE, GLA_HEADS * GLA_DK), F32).at[_NG:_NG + GLA_RANK].set(w_gla_g2[l]).astype(BF16),
        b_gla_g=row1(b_gla_g[l]), g_gla_out=row1(g_gla_out[l]), w_branch=w_branch[l].astype(BF16),
        w_mg=w_merge_gate[l].astype(BF16), b_mg=row1(b_merge_gate[l]), w_out=w_out[l].astype(BF16),
        g_post_mix=row1(g_post_mix[l]))

    def ffn(x, g_pre, g_post, w_gu, w_d, tm):
        b, t, _ = x.shape
        return _ffn_half(x.reshape(b * t, D_MODEL), row1(g_pre), row1(g_post), w_gu, w_d, tm).reshape(x.shape)

    w1gu, w1d = w_ff1_gu[l].astype(BF16), w_ff1_d[l].astype(BF16)
    w2gu, w2d = w_ff2_gu[l].astype(BF16), w_ff2_d[l].astype(BF16)
    tm_p, tm_s = 512, 256

    hp = ffn(x_prompt, g_pre_ff1[l], g_post_ff1[l], w1gu, w1d, tm_p)
    mem_kv = _mem_proj(mem_prompt[0], row1(g_mem[l]), w_mem_kv[l].astype(BF16))
    nsa_p = lambda q, kc, ks, kw, ng: (_nsa_prompt_path(q, kc, ks, kw, ng, w_cmp1[l], pe_cmp[l], w_cmp2[l]), None)
    s0 = jnp.zeros((bp, GLA_HEADS, GLA_DK, GLA_DV), F32)
    hp, (cmp_p, slc_p, win_p), gla_p, _ = _mixer(hp, mem_kv[None], nsa_p, s0, (2 * GLA_CHUNK, 4, False, 1), wts, tm_p, tm_p)
    yp = ffn(hp, g_pre_ff2[l], g_post_ff2[l], w2gu, w2d, tm_p)

    hs = ffn(x_sample, g_pre_ff1[l], g_post_ff1[l], w1gu, w1d, tm_s)
    wbuf = cache_win_kv.shape[2]
    nsa_s = lambda q, kc, ks, kw, ng: _nsa_sample_path(q, kc, ks, kw, ng, cache_cmp_kv[l], cache_slc_kv[l],
                                                       cache_win_kv[l], page_table, w_cmp1[l], pe_cmp[l], w_cmp2[l])
    mem_s = cache_mem_kv[l].reshape(bs, cache_mem_kv.shape[2], 2 * BRANCH_W)
    hs, (cmp_s, slc_s, _), gla_s, win_s = _mixer(hs, mem_s, nsa_s, state_gla[l], (min(GLA_CHUNK, ts), 1, True, 8), wts,
                                                 tm_s, ts)
    ys = ffn(hs, g_pre_ff2[l], g_post_ff2[l], w2gu, w2d, tm_s)

    kv5 = lambda a, b, t: a.reshape(1, b, t, 2, NSA_KV, NSA_HD)
    wp = min(WINDOW, tp)
    return (yp, ys,
            kv5(cmp_p, bp, tp), kv5(slc_p, bp, tp), kv5(win_p.reshape(bp, tp, KV_SLAB)[:, tp - wp:], bp, wp),
            gla_p[None], mem_kv.reshape(1, bp, mem_kv.shape[0], 2, X_HEADS, X_HD),
            kv5(cmp_s, bs, ts), kv5(slc_s, bs, ts), kv5(win_s, bs, wbuf), gla_s[None])
```

```python
import functools

import numpy as np
import jax
import jax.numpy as jnp
from jax import lax
from jax.experimental import pallas as pl
from jax.experimental.pallas import tpu as pltpu

F32 = jnp.float32
BF16 = jnp.bfloat16

D_MODEL = 1024
BRANCH_W = D_MODEL // 2
NSA_HEADS = 8
NSA_KV = 2
NSA_HPG = NSA_HEADS // NSA_KV
NSA_HD = BRANCH_W // NSA_HEADS
CMP_LEN = 32
CMP_STRIDE = 16
CMP_HID = 128
SLC_BLK = 64
SLC_TOPN = 16
WINDOW = 512
GLA_HEADS = 4
GLA_DV = BRANCH_W // GLA_HEADS
GLA_DK = GLA_DV // 2
GLA_RANK = 16
GLA_TAU = 16.0
GLA_CHUNK = 64
X_HEADS = 4
X_HD = BRANCH_W // X_HEADS
D_FF = 2816
EPS = 1e-6
NEG = -1e30
FORCE = 1e6
PAGE_SIZE = 128

LANE = 128
BF16_ROWS = 16
VT_ROWS = NSA_HD + BF16_ROWS
LOG2E = 1.4426950408889634
KV_SLAB = 2 * NSA_KV * NSA_HD
VMEM_LIMIT = 56 * 1024 * 1024


def _cparams(sem):
    return pltpu.CompilerParams(dimension_semantics=sem, vmem_limit_bytes=VMEM_LIMIT)


def _rms(x, g):
    xf = x.astype(F32)
    return xf * lax.rsqrt(jnp.mean(xf * xf, axis=-1, keepdims=True) + EPS) * g


def _silu(x):
    return x * jax.nn.sigmoid(x)


def _dot(a, b):
    return jnp.dot(a, b, preferred_element_type=F32)


def _dot_nt(a, b):
    return lax.dot_general(a, b, (((1,), (1,)), ((), ())), preferred_element_type=F32)


def _dot_tn(a, b):
    return lax.dot_general(a, b, (((0,), (0,)), ((), ())), preferred_element_type=F32)


def _split3(x):
    hi = x.astype(BF16)
    r1 = x - hi.astype(F32)
    mid = r1.astype(BF16)
    lo = (r1 - mid.astype(F32)).astype(BF16)
    return hi, mid, lo


def _ffn_kernel(x_ref, gpre_ref, gpost_ref, wg_ref, wu_ref, wd_ref, o_ref, xn_ref, acc_ref):
    f = pl.program_id(1)

    @pl.when(f == 0)
    def _():
        xn_ref[...] = _rms(x_ref[...], gpre_ref[...]).astype(BF16)
        acc_ref[...] = jnp.zeros_like(acc_ref)

    xn = xn_ref[...]
    a = _dot(xn, wg_ref[...])
    b = _dot(xn, wu_ref[...])
    acc_ref[...] += _dot((_silu(a) * b).astype(BF16), wd_ref[...])

    @pl.when(f == pl.num_programs(1) - 1)
    def _():
        o_ref[...] = x_ref[...] + 0.5 * _rms(acc_ref[...], gpost_ref[...])


def _ffn_half(x, g_pre, g_post, w_gu, w_d, tm):
    rows = x.shape[0]
    tf = D_FF // 2
    nf = D_FF // tf
    return pl.pallas_call(
        _ffn_kernel,
        grid=(rows // tm, nf),
        in_specs=[
            pl.BlockSpec((tm, D_MODEL), lambda i, f: (i, 0)),
            pl.BlockSpec((1, D_MODEL), lambda i, f: (0, 0)),
            pl.BlockSpec((1, D_MODEL), lambda i, f: (0, 0)),
            pl.BlockSpec((D_MODEL, tf), lambda i, f: (0, f)),
            pl.BlockSpec((D_MODEL, tf), lambda i, f: (0, nf + f)),
            pl.BlockSpec((tf, D_MODEL), lambda i, f: (f, 0)),
        ],
        out_specs=pl.BlockSpec((tm, D_MODEL), lambda i, f: (i, 0)),
        out_shape=jax.ShapeDtypeStruct((rows, D_MODEL), F32),
        scratch_shapes=[pltpu.VMEM((tm, D_MODEL), BF16), pltpu.VMEM((tm, D_MODEL), F32)],
        compiler_params=_cparams(("parallel", "arbitrary")),
        name="ffn_half",
    )(x, g_pre, g_post, w_gu, w_gu, w_d)


_NG = NSA_HEADS * 3
_C_Q, _C_CMP, _C_SLC, _C_WIN, _C_GQ, _C_GK, _C_GV, _C_GR, _C_XQ, _C_SM, _C_END = (
    0, 512, 768, 1024, 1280, 1536, 1792, 2304, 2816, 3328, 3456)


def _pack_w_in(w_in):
    offs = np.cumsum([0, NSA_HEADS * NSA_HD, 3 * KV_SLAB, _NG, GLA_HEADS * GLA_DK, GLA_HEADS * GLA_DK,
                      GLA_HEADS * GLA_DV, GLA_RANK, GLA_HEADS * GLA_DV, X_HEADS * X_HD])
    q, kv, ng, gq, gk, gv, glr, gr, xq = [w_in[:, offs[i]:offs[i + 1]] for i in range(9)]
    pad = jnp.zeros((D_MODEL, LANE - _NG - GLA_RANK), w_in.dtype)
    return jnp.concatenate([q, kv, gq, gk, gv, gr, xq, ng, glr, pad], axis=1).astype(BF16)


def _in_proj_kernel(h_ref, g_ref, w_ref, w2_ref, b2_ref,
                    u_ref, q_ref, cmp_ref, slc_ref, win_ref, ng_ref, gq_ref, gk_ref, gv_ref, la_ref, gr_ref,
                    xq_ref):
    u = _rms(h_ref[...], g_ref[...]).astype(BF16)
    u_ref[...] = u

    def proj(lo, hi):
        return _dot(u, w_ref[:, lo:hi])

    q_ref[...] = proj(_C_Q, _C_CMP) * (NSA_HD ** -0.5)
    cmp_ref[...] = proj(_C_CMP, _C_SLC)
    slc_ref[...] = proj(_C_SLC, _C_WIN)
    win_ref[...] = proj(_C_WIN, _C_GQ)
    gq_ref[...] = proj(_C_GQ, _C_GK) * (GLA_DK ** -0.5)
    gk_ref[...] = proj(_C_GK, _C_GV)
    gv_ref[...] = proj(_C_GV, _C_GR)
    gr_ref[...] = proj(_C_GR, _C_XQ)
    xq_ref[...] = proj(_C_XQ, _C_SM) * (X_HD ** -0.5)
    small = proj(_C_SM, _C_END)
    ng_ref[...] = jax.nn.sigmoid(small)
    z = _dot(small.astype(BF16), w2_ref[...]) + b2_ref[...]
    log_sig = jnp.minimum(z, 0.0) - jnp.log1p(jnp.exp(-jnp.abs(z)))
    la_ref[...] = log_sig / GLA_TAU


def _in_proj(h, g_pre_mix, w_cat, w2_pad, b2, tm):
    rows = h.shape[0]
    widths = [(D_MODEL, BF16), (512, F32), (256, F32), (256, F32), (256, F32), (LANE, F32), (256, F32),
              (256, F32), (512, F32), (256, F32), (512, F32), (512, F32)]
    return pl.pallas_call(
        _in_proj_kernel,
        grid=(rows // tm,),
        in_specs=[
            pl.BlockSpec((tm, D_MODEL), lambda i: (i, 0)),
            pl.BlockSpec((1, D_MODEL), lambda i: (0, 0)),
            pl.BlockSpec((D_MODEL, _C_END), lambda i: (0, 0)),
            pl.BlockSpec((LANE, 256), lambda i: (0, 0)),
            pl.BlockSpec((1, 256), lambda i: (0, 0)),
        ],
        out_specs=[pl.BlockSpec((tm, w), lambda i: (i, 0)) for w, _ in widths],
        out_shape=[jax.ShapeDtypeStruct((rows, w), dt) for w, dt in widths],
        compiler_params=_cparams(("parallel",)),
        name="in_proj",
    )(h, g_pre_mix, w_cat, w2_pad, b2)


def _pack_cmp_w1(w1):
    wa, wb = w1[:CMP_STRIDE], w1[CMP_STRIDE:]
    z = jnp.zeros_like(wa)
    rows = []
    for g in range(NSA_KV):
        cols = []
        for half in (wa, wb):
            for g2 in range(NSA_KV):
                cols.append(half if g2 == g else z)
        rows.append(jnp.concatenate(cols, axis=2))
    w = jnp.stack(rows, axis=1)
    return w.reshape(CMP_STRIDE // 2, 2 * NSA_KV * NSA_HD, 4 * CMP_HID).astype(BF16)


def _pack_cmp_w2(w2):
    z = jnp.zeros_like(w2)
    return jnp.concatenate([jnp.concatenate([w2, z], axis=1), jnp.concatenate([z, w2], axis=1)], axis=0).astype(BF16)


def _compress_ab(load_pos, wk_ref, wv_ref):
    hk = hv = None
    for lp in range(CMP_STRIDE // 2):
        xa = load_pos(2 * lp)
        xb = load_pos(2 * lp + 1)
        lk = jnp.concatenate([xa[:, :LANE], xb[:, :LANE]], axis=1).astype(BF16)
        lv = jnp.concatenate([xa[:, LANE:], xb[:, LANE:]], axis=1).astype(BF16)
        dk = _dot(lk, wk_ref[lp])
        dv = _dot(lv, wv_ref[lp])
        hk = dk if hk is None else hk + dk
        hv = dv if hv is None else hv + dv
    return hk, hv


def _compress_finish(ab, bias, w2_ref, shift_ref):
    n = ab.shape[0]
    shift_ref[0:n, :] = ab[:, 2 * CMP_HID:]
    shift_ref[n:n + 8, :] = jnp.zeros((8, 2 * CMP_HID), F32)
    h = ab[:, :2 * CMP_HID] + shift_ref[pl.ds(1, n), :] + bias
    return _dot(_silu(h).astype(BF16), w2_ref[...])


def _cmp_bias_kernel(pe_ref, w1_ref, o_ref):
    for kv in range(2):
        b = _dot(pe_ref[kv].astype(BF16), w1_ref[kv])
        o_ref[kv] = jnp.concatenate([b, b], axis=1)


def _cmp_bias(pe_cmp, w_cmp1):
    pe = jnp.broadcast_to(pe_cmp.reshape(2, 1, CMP_LEN * NSA_HD), (2, 8, CMP_LEN * NSA_HD))
    w1 = w_cmp1.reshape(2, CMP_LEN * NSA_HD, CMP_HID).astype(BF16)
    return pl.pallas_call(
        _cmp_bias_kernel,
        out_shape=jax.ShapeDtypeStruct((2, 8, 2 * CMP_HID), F32),
        name="cmp_bias",
    )(pe, w1)


def _nsa_prep_kernel(cmpv_ref, slc_ref, win_ref, wk_ref, wv_ref,
                     abk_ref, abv_ref, kslc_ref, vslct_ref, kwin_ref, vwint_ref):
    abk, abv = _compress_ab(lambda l: cmpv_ref[:, l * KV_SLAB:(l + 1) * KV_SLAB], wk_ref, wv_ref)
    abk_ref[...] = abk
    abv_ref[...] = abv
    nchunk = slc_ref.shape[0] // LANE
    ones_row = (lax.broadcasted_iota(jnp.int32, (VT_ROWS - NSA_HD, LANE), 0) == 0).astype(F32)
    for src, k_ref, vt_ref in ((slc_ref, kslc_ref, vslct_ref), (win_ref, kwin_ref, vwint_ref)):
        k_ref[...] = src[:, :LANE].astype(BF16)
        for c in range(nchunk):
            vt = src[c * LANE:(c + 1) * LANE, LANE:].T
            vt_ref[c] = jnp.concatenate([vt[:NSA_HD], ones_row, vt[NSA_HD:], ones_row], axis=0).astype(BF16)


def _nsa_prep(kv_cmp, kv_slc, kv_win, wk2, wv2, tm):
    t = kv_cmp.shape[0]
    nsub = t // CMP_STRIDE
    cmp_view = kv_cmp.reshape(nsub, CMP_STRIDE * KV_SLAB)
    const3 = lambda i: (0, 0, 0)
    return pl.pallas_call(
        _nsa_prep_kernel,
        grid=(t // tm,),
        in_specs=[
            pl.BlockSpec((tm // CMP_STRIDE, CMP_STRIDE * KV_SLAB), lambda i: (i, 0)),
            pl.BlockSpec((tm, KV_SLAB), lambda i: (i, 0)),
            pl.BlockSpec((tm, KV_SLAB), lambda i: (i, 0)),
            pl.BlockSpec(wk2.shape, const3),
            pl.BlockSpec(wv2.shape, const3),
        ],
        out_specs=[
            pl.BlockSpec((tm // CMP_STRIDE, 4 * CMP_HID), lambda i: (i, 0)),
            pl.BlockSpec((tm // CMP_STRIDE, 4 * CMP_HID), lambda i: (i, 0)),
            pl.BlockSpec((tm, LANE), lambda i: (i, 0)),
            pl.BlockSpec((tm // LANE, NSA_KV * VT_ROWS, LANE), lambda i: (i, 0, 0)),
            pl.BlockSpec((tm, LANE), lambda i: (i, 0)),
            pl.BlockSpec((tm // LANE, NSA_KV * VT_ROWS, LANE), lambda i: (i, 0, 0)),
        ],
        out_shape=[
            jax.ShapeDtypeStruct((nsub, 4 * CMP_HID), F32),
            jax.ShapeDtypeStruct((nsub, 4 * CMP_HID), F32),
            jax.ShapeDtypeStruct((t, LANE), BF16),
            jax.ShapeDtypeStruct((t // LANE, NSA_KV * VT_ROWS, LANE), BF16),
            jax.ShapeDtypeStruct((t, LANE), BF16),
            jax.ShapeDtypeStruct((t // LANE, NSA_KV * VT_ROWS, LANE), BF16),
        ],
        compiler_params=_cparams(("parallel",)),
        name="nsa_prep",
    )(cmp_view, kv_slc, kv_win, wk2, wv2)


def _cmp_finish_kernel(abk_ref, abv_ref, bias_ref, w2k_ref, w2v_ref, kc_ref, vct_ref, shift_ref):
    n = abk_ref.shape[0]
    kc_ref[...] = _compress_finish(abk_ref[...], bias_ref[0, 0:1, :], w2k_ref, shift_ref).astype(BF16)
    vc = _compress_finish(abv_ref[...], bias_ref[1, 0:1, :], w2v_ref, shift_ref)
    for c in range(n // LANE):
        vct_ref[:, c * LANE:(c + 1) * LANE] = vc[c * LANE:(c + 1) * LANE, :].T.astype(BF16)


def _cmp_finish(abk, abv, bias, w2k, w2v):
    n = abk.shape[0]
    return pl.pallas_call(
        _cmp_finish_kernel,
        out_shape=[jax.ShapeDtypeStruct((n, LANE), BF16), jax.ShapeDtypeStruct((LANE, n), BF16)],
        scratch_shapes=[pltpu.VMEM((n + 8, 2 * CMP_HID), F32)],
        compiler_params=pltpu.CompilerParams(vmem_limit_bytes=VMEM_LIMIT),
        name="cmp_finish",
    )(abk, abv, bias, w2k, w2v)


def _select_blocks(score, n_top):
    nb = score.shape[0]
    jj = lax.broadcasted_iota(jnp.int32, score.shape, 0)
    taken = jnp.float32(-3e38)
    for _ in range(n_top):
        m = jnp.max(score, axis=0, keepdims=True)
        idx = jnp.min(jnp.where(score == m, jj, nb), axis=0, keepdims=True)
        score = jnp.where(jj == idx, taken, score)
    return (score == taken).astype(F32)


def _block_scores(imp, jj, cur, n_real):
    s = jnp.where(jj == 0, FORCE, jnp.where(jj == cur, FORCE, jnp.where(jj == cur - 1, FORCE, imp)))
    s = jnp.where(jj <= cur, s, NEG)
    return jnp.where(jj < n_real, s, -jnp.inf)


def _cmp_to_slc_t(nc_pad, nb_pad):
    i = np.arange(nc_pad)[None, :]
    j = np.arange(nb_pad)[:, None]
    ov = (i * CMP_STRIDE < (j + 1) * SLC_BLK) & (i * CMP_STRIDE + CMP_LEN > j * SLC_BLK)
    return jnp.asarray(ov.astype(np.float32), dtype=BF16)


TQ = LANE


SLC_CK = 8 * TQ


def _flash_step(qbd, k, vt, bias, m_ref, acc_ref):
    s = _dot(k, qbd)
    for g in range(NSA_KV):
        ps = []
        for h in range(NSA_HPG):
            lo = (g * NSA_HPG + h) * TQ
            sc = s[:, lo:lo + TQ] + bias[g]
            m_old = m_ref[:, lo:lo + TQ]
            m_new = jnp.maximum(m_old, jnp.max(sc, axis=0, keepdims=True))
            ps.append(jnp.exp2((sc - m_new).astype(BF16)))
            m_ref[:, lo:lo + TQ] = m_new
            acc_ref[g, :, h * TQ:(h + 1) * TQ] = acc_ref[g, :, h * TQ:(h + 1) * TQ] * jnp.exp2(m_old - m_new)
        acc_ref[g] += _dot(vt[g * VT_ROWS:(g + 1) * VT_ROWS, :], jnp.concatenate(ps, axis=1))


def _softmax_once(qbd, k, vt, bias):
    s = _dot(k, qbd)
    outs = []
    for g in range(NSA_KV):
        ps = []
        for h in range(NSA_HPG):
            lo = (g * NSA_HPG + h) * TQ
            sc = s[:, lo:lo + TQ] + bias
            ps.append(jnp.exp2((sc - jnp.max(sc, axis=0, keepdims=True)).astype(BF16)))
        o = _dot(vt[g * VT_ROWS:(g + 1) * VT_ROWS, :], jnp.concatenate(ps, axis=1))
        outs.append(o[:NSA_HD] * (1.0 / o[NSA_HD:NSA_HD + 1]))
    return outs


def _flash_reset(m_ref, acc_ref):
    m_ref[...] = jnp.full(m_ref.shape, NEG, F32)
    acc_ref[...] = jnp.zeros(acc_ref.shape, F32)


def _flash_result(acc_ref):
    return [acc_ref[g, :NSA_HD, :] * (1.0 / acc_ref[g, NSA_HD:NSA_HD + 1, :]) for g in range(NSA_KV)]


def _nsa_prompt_kernel(q_ref, ng_ref, kc_ref, vct_ref, indt_ref, kslc_ref, vslct_ref, kwin_ref, vwint_ref,
                       o_ref, selb_ref, m_ref, acc_ref):
    i = pl.program_id(0)
    s0 = i * TQ
    ncp = kc_ref.shape[0]
    nbp = indt_ref.shape[0]
    n_blocks = kslc_ref.shape[0] // SLC_BLK

    qt = jnp.concatenate([q_ref[:, c * LANE:(c + 1) * LANE].T for c in range(BRANCH_W // LANE)], axis=0)
    zero = jnp.zeros((NSA_HD, NSA_HPG * TQ), F32)
    rows = []
    for g in range(NSA_KV):
        blk = jnp.concatenate(
            [qt[(g * NSA_HPG + h) * NSA_HD:(g * NSA_HPG + h + 1) * NSA_HD, :] for h in range(NSA_HPG)], axis=1)
        rows.append(jnp.concatenate([blk, zero] if g == 0 else [zero, blk], axis=1))
    qbd = (jnp.concatenate(rows, axis=0) * LOG2E).astype(BF16)

    q_lane = s0 + lax.broadcasted_iota(jnp.int32, (1, TQ), 1)

    n_win = WINDOW + TQ
    w0 = pl.multiple_of(jnp.maximum(s0 - WINDOW, 0), TQ)
    dist = q_lane - (w0 + lax.broadcasted_iota(jnp.int32, (n_win, TQ), 0))
    bias_w = jnp.where(dist >= 0, jnp.where(dist < WINDOW, 0.0, NEG), NEG)
    vt_w = jnp.concatenate([vwint_ref[w0 // TQ + j] for j in range(n_win // TQ)], axis=1)
    o_win = _softmax_once(qbd, kwin_ref[pl.ds(w0, n_win), :], vt_w, bias_w)

    c_end = lax.broadcasted_iota(jnp.int32, (ncp, TQ), 0) * CMP_STRIDE + (CMP_LEN - 1)
    bias_c = jnp.where(c_end <= q_lane, 0.0, NEG)
    col_ok = q_lane >= CMP_LEN - 1
    jj = lax.broadcasted_iota(jnp.int32, (nbp, TQ), 0)
    cur = q_lane // SLC_BLK
    o_cmp, scores = [], []
    for g in range(NSA_KV):
        sc_all = _dot(kc_ref[...], qbd[:, g * NSA_HPG * TQ:(g + 1) * NSA_HPG * TQ])
        psum = jnp.zeros((ncp, TQ), F32)
        ps = []
        for h in range(NSA_HPG):
            sc = sc_all[:, h * TQ:(h + 1) * TQ] + bias_c
            p = jnp.exp2(sc - jnp.max(sc, axis=0, keepdims=True))
            pn = p * jnp.where(col_ok, 1.0 / jnp.sum(p, axis=0, keepdims=True), 0.0)
            psum = psum + pn
            ps.append(pn.astype(BF16))
        o_cmp.append(_dot(vct_ref[g * NSA_HD:(g + 1) * NSA_HD, :], jnp.concatenate(ps, axis=1)))
        imp = sum(_dot(indt_ref[...], part) for part in _split3(psum))
        scores.append(_block_scores(imp, jj, cur, n_blocks))
    sel = _select_blocks(jnp.concatenate(scores, axis=1), min(SLC_TOPN, n_blocks))
    for g in range(NSA_KV):
        selb_ref[g] = (sel[:, g * TQ:(g + 1) * TQ] - 1.0) * (-NEG)

    blk_per_step = SLC_CK // SLC_BLK

    def slc_inputs(c):
        k = kslc_ref[pl.ds(pl.multiple_of(c * SLC_CK, SLC_CK), SLC_CK), :]
        vt = jnp.concatenate([vslct_ref[c * (SLC_CK // TQ) + j] for j in range(SLC_CK // TQ)], axis=1)
        bias = []
        for g in range(NSA_KV):
            rows = selb_ref[g, pl.ds(pl.multiple_of(c * blk_per_step, blk_per_step), blk_per_step), :]
            bias.append(jnp.concatenate(
                [jnp.broadcast_to(rows[j:j + 1, :], (SLC_BLK, TQ)) for j in range(blk_per_step)], axis=0))
        return k, vt, bias

    _flash_reset(m_ref, acc_ref)

    def slc_step(c):
        k, vt, bias = slc_inputs(c)
        _flash_step(qbd, k, vt, bias, m_ref, acc_ref)

    def slc_pair(t, carry):
        slc_step(2 * t)
        slc_step(2 * t + 1)
        return carry

    c_last = s0 // SLC_CK
    lax.fori_loop(0, c_last // 2, slc_pair, 0)

    @pl.when(c_last % 2 == 1)
    def _():
        slc_step(c_last - 1)
    k, vt, bias = slc_inputs(c_last)
    key_pos = c_last * SLC_CK + lax.broadcasted_iota(jnp.int32, (SLC_CK, TQ), 0)
    causal = key_pos <= q_lane
    _flash_step(qbd, k, vt, [jnp.where(causal, b, NEG) for b in bias], m_ref, acc_ref)
    o_slc = _flash_result(acc_ref)

    ngt = ng_ref[...].T
    for g in range(NSA_KV):
        heads = []
        for h in range(NSA_HPG):
            r = (g * NSA_HPG + h) * 3
            cs = slice(h * TQ, (h + 1) * TQ)
            heads.append(ngt[r:r + 1, :] * o_cmp[g][:, cs] + ngt[r + 1:r + 2, :] * o_slc[g][:, cs]
                         + ngt[r + 2:r + 3, :] * o_win[g][:, cs])
        for pair in range(NSA_HPG // 2):
            blk = jnp.concatenate(heads[2 * pair:2 * pair + 2], axis=0).T
            col = (g * NSA_HPG // 2 + pair) * LANE
            o_ref[:, col:col + LANE] = blk


def _nsa_prompt(q, ng, kc, vct, indt, kslc, vslct, kwin, vwint):
    t = q.shape[0]
    full2 = lambda i: (0, 0)
    full3 = lambda i: (0, 0, 0)
    return pl.pallas_call(
        _nsa_prompt_kernel,
        grid=(t // TQ,),
        in_specs=[
            pl.BlockSpec((TQ, BRANCH_W), lambda i: (i, 0)),
            pl.BlockSpec((TQ, LANE), lambda i: (i, 0)),
            pl.BlockSpec(kc.shape, full2),
            pl.BlockSpec(vct.shape, full2),
            pl.BlockSpec(indt.shape, full2),
            pl.BlockSpec(kslc.shape, full2),
            pl.BlockSpec(vslct.shape, full3),
            pl.BlockSpec(kwin.shape, full2),
            pl.BlockSpec(vwint.shape, full3),
        ],
        out_specs=pl.BlockSpec((TQ, BRANCH_W), lambda i: (i, 0)),
        out_shape=jax.ShapeDtypeStruct((t, BRANCH_W), F32),
        scratch_shapes=[
            pltpu.VMEM((NSA_KV, indt.shape[0], TQ), F32),
            pltpu.VMEM((1, NSA_HEADS * TQ), F32),
            pltpu.VMEM((NSA_KV, VT_ROWS, NSA_HPG * TQ), F32),
        ],
        compiler_params=_cparams(("arbitrary",)),
        name="nsa_prompt",
    )(q, ng, kc, vct, indt, kslc, vslct, kwin, vwint)


def _nsa_prompt_path(q, kv_cmp, kv_slc, kv_win, ng, w_cmp1, pe_cmp, w_cmp2):
    t = q.shape[0]
    wk2, wv2 = _pack_cmp_w1(w_cmp1[0]), _pack_cmp_w1(w_cmp1[1])
    w2k, w2v = _pack_cmp_w2(w_cmp2[0]), _pack_cmp_w2(w_cmp2[1])
    bias = _cmp_bias(pe_cmp, w_cmp1)
    abk, abv, kslc, vslct, kwin, vwint = _nsa_prep(kv_cmp, kv_slc, kv_win, wk2, wv2, min(t, 1024))
    kc, vct = _cmp_finish(abk, abv, bias, w2k, w2v)
    indt = _cmp_to_slc_t(t // CMP_STRIDE, t // SLC_BLK)
    return _nsa_prompt(q, ng, kc, vct, indt, kslc, vslct, kwin, vwint)


def _gla_kernel(gq_ref, gk_ref, gv_ref, la_ref, s0_ref, o_ref, sfin_ref, st_ref, *, chunk, nchunk, exact):
    j = pl.program_id(1)
    n_pair = GLA_HEADS // 2
    nseq = gq_ref.shape[0]

    @pl.when(j == 0)
    def _():
        for s in range(nseq):
            for p in range(n_pair):
                st_ref[s, p] = s0_ref[s, p].T

    if exact:
        hi = lax.Precision.HIGHEST
        mx = lambda x: x
        dot = lambda a, b: jnp.dot(a, b, precision=hi, preferred_element_type=F32)
        dot_nt = lambda a, b: lax.dot_general(a, b, (((1,), (1,)), ((), ())), precision=hi,
                                              preferred_element_type=F32)
        dot_tn = lambda a, b: lax.dot_general(a, b, (((0,), (0,)), ((), ())), precision=hi,
                                              preferred_element_type=F32)
    else:
        mx = lambda x: x.astype(BF16)
        dot, dot_nt, dot_tn = _dot, _dot_nt, _dot_tn

    row = lax.broadcasted_iota(jnp.int32, (chunk, chunk), 0)
    col = lax.broadcasted_iota(jnp.int32, (chunk, chunk), 1)
    tri_f = (row >= col).astype(F32)
    row2 = lax.broadcasted_iota(jnp.int32, (2 * chunk, 2 * chunk), 0)
    col2 = lax.broadcasted_iota(jnp.int32, (2 * chunk, 2 * chunk), 1)
    tri2 = row2 % chunk >= col2 % chunk
    own_lanes = (lax.broadcasted_iota(jnp.int32, (2 * chunk, LANE), 1) // GLA_DK
                 == lax.broadcasted_iota(jnp.int32, (2 * chunk, LANE), 0) // chunk)

    def stack_heads(x):
        return mx(jnp.where(own_lanes, jnp.concatenate([x, x], axis=0), 0.0))

    for s in range(nseq):
        carry = []
        for ci in range(nchunk):
            rows = slice(ci * chunk, (ci + 1) * chunk)
            la = la_ref[s, rows, :]
            if exact:
                b = dot(tri_f, la)
            else:
                b = sum(_dot(tri_f.astype(BF16), part) for part in _split3(la))
            b_end = b[chunk - 1:chunk, :]
            k = gk_ref[s, rows, :]
            qe = gq_ref[s, rows, :] * jnp.exp(b)
            ke = k * jnp.exp(-b)
            kend = k * jnp.exp(b_end - b)
            per_pair = []
            for p in range(n_pair):
                sl = slice(p * LANE, (p + 1) * LANE)
                q2 = stack_heads(qe[:, sl])
                att = jnp.where(tri2, dot_nt(q2, stack_heads(ke[:, sl])), 0.0)
                v2 = mx(jnp.concatenate([gv_ref[s, rows, (2 * p + hh) * GLA_DV:(2 * p + hh + 1) * GLA_DV]
                                         for hh in range(2)], axis=0))
                o2 = dot(mx(att), v2)
                for hh in range(2):
                    o_ref[s, rows, (2 * p + hh) * GLA_DV:(2 * p + hh + 1) * GLA_DV] = o2[hh * chunk:(hh + 1) * chunk]
                per_pair.append((jnp.exp(b_end[:, sl]), dot_tn(v2, stack_heads(kend[:, sl])), q2))
            carry.append(per_pair)

        for p in range(n_pair):
            st = st_ref[s, p]
            for ci in range(nchunk):
                rows = slice(ci * chunk, (ci + 1) * chunk)
                decay, kv_sum, q2 = carry[ci][p]
                inter = dot_nt(q2, mx(st))
                for hh in range(2):
                    o_ref[s, rows, (2 * p + hh) * GLA_DV:(2 * p + hh + 1) * GLA_DV] += inter[hh * chunk:(hh + 1) * chunk]
                st = st * decay + kv_sum
            st_ref[s, p] = st

    @pl.when(j == pl.num_programs(1) - 1)
    def _():
        for s in range(nseq):
            for p in range(n_pair):
                sfin_ref[s, p] = st_ref[s, p].T


def _gla(gq, gk, gv, la, s0, chunk, nchunk, exact, sb):
    bsz, t, _ = gq.shape
    rb = chunk * nchunk
    row_spec = lambda w: pl.BlockSpec((sb, rb, w), lambda b, j: (b, j, 0))
    st_spec = pl.BlockSpec((sb, GLA_HEADS // 2, LANE, LANE), lambda b, j: (b, 0, 0, 0))
    return pl.pallas_call(
        functools.partial(_gla_kernel, chunk=chunk, nchunk=nchunk, exact=exact),
        grid=(bsz // sb, t // rb),
        in_specs=[row_spec(256), row_spec(256), row_spec(512), row_spec(256), st_spec],
        out_specs=[row_spec(512), st_spec],
        out_shape=[jax.ShapeDtypeStruct((bsz, t, BRANCH_W), F32),
                   jax.ShapeDtypeStruct((bsz, GLA_HEADS // 2, LANE, LANE), F32)],
        scratch_shapes=[pltpu.VMEM((sb, GLA_HEADS // 2, LANE, LANE), F32)],
        compiler_params=_cparams(("parallel", "arbitrary")),
        name="gla",
    )(gq, gk, gv, la, s0)


def _mem_proj_kernel(mem_ref, g_ref, w_ref, o_ref):
    o_ref[...] = _dot(_rms(mem_ref[...], g_ref[...]).astype(BF16), w_ref[...])


def _mem_proj(mem, g_mem, w_mem_kv):
    return pl.pallas_call(
        _mem_proj_kernel,
        out_shape=jax.ShapeDtypeStruct((mem.shape[0], w_mem_kv.shape[1]), F32),
        compiler_params=pltpu.CompilerParams(vmem_limit_bytes=VMEM_LIMIT),
        name="mem_proj",
    )(mem, g_mem, w_mem_kv)


def _xattn_kernel(xq_ref, mem_ref, o_ref):
    tm = xq_ref.shape[1]
    n_mem = mem_ref.shape[1] // (2 * X_HEADS)

    def mem_part(b, kv, h):
        return mem_ref[b, pl.ds(kv * X_HEADS + h, n_mem, stride=2 * X_HEADS), :].astype(BF16)

    if tm * X_HEADS <= LANE:
        lane_head = lax.broadcasted_iota(jnp.int32, (tm, BRANCH_W), 1) // X_HD
        for b in range(xq_ref.shape[0]):
            xq = xq_ref[b]
            qbd = jnp.concatenate([jnp.where(lane_head == h, xq, 0.0) for h in range(X_HEADS)], axis=0)
            k_all = jnp.concatenate([mem_part(b, 0, h) for h in range(X_HEADS)], axis=1)
            v_all = jnp.concatenate([mem_part(b, 1, h) for h in range(X_HEADS)], axis=1)
            s = _dot_nt(qbd.astype(BF16), k_all)
            p = jnp.exp(s - jnp.max(s, axis=-1, keepdims=True))
            p = p * (1.0 / jnp.sum(p, axis=-1, keepdims=True))
            o = _dot(p.astype(BF16), v_all)
            o_ref[b] = sum(jnp.where(lane_head == h, o[h * tm:(h + 1) * tm], 0.0) for h in range(X_HEADS))
        return
    for b in range(xq_ref.shape[0]):
        for h in range(X_HEADS):
            cs = slice(h * X_HD, (h + 1) * X_HD)
            s = _dot_nt(xq_ref[b, :, cs].astype(BF16), mem_part(b, 0, h))
            p = jnp.exp(s - jnp.max(s, axis=-1, keepdims=True))
            p = p * (1.0 / jnp.sum(p, axis=-1, keepdims=True))
            o_ref[b, :, cs] = _dot(p.astype(BF16), mem_part(b, 1, h))


def _xattn(xq, mem_kv, tm, sb):
    bsz, t, _ = xq.shape
    return pl.pallas_call(
        _xattn_kernel,
        grid=(bsz // sb, t // tm),
        in_specs=[pl.BlockSpec((sb, tm, BRANCH_W), lambda b, i: (b, i, 0)),
                  pl.BlockSpec((sb,) + mem_kv.shape[1:], lambda b, i: (b, 0, 0))],
        out_specs=pl.BlockSpec((sb, tm, BRANCH_W), lambda b, i: (b, i, 0)),
        out_shape=jax.ShapeDtypeStruct(xq.shape, F32),
        compiler_params=_cparams(("parallel", "parallel")),
        name="xattn",
    )(xq, mem_kv)


def _merge_kernel(u_ref, h_ref, onsa_ref, ogla_ref, gr_ref, ox_ref, ggla_ref, wb_ref, wmg_ref, bmg_ref, wout_ref,
                  gpost_ref, o_ref):
    u = u_ref[...]
    parts = []
    for h in range(GLA_HEADS):
        cs = slice(h * GLA_DV, (h + 1) * GLA_DV)
        parts.append(_rms(ogla_ref[:, cs], ggla_ref[...]) * _silu(gr_ref[:, cs]))
    branches = [onsa_ref[...], jnp.concatenate(parts, axis=1), ox_ref[...]]
    mixed = None
    for b in range(3):
        cs = slice(b * D_MODEL, (b + 1) * D_MODEL)
        gate = jax.nn.sigmoid(_dot(u, wmg_ref[:, cs]) + bmg_ref[:, cs])
        term = gate * _dot(branches[b].astype(BF16), wb_ref[b])
        mixed = term if mixed is None else mixed + term
    m = _dot(mixed.astype(BF16), wout_ref[...])
    o_ref[...] = h_ref[...] + _rms(m, gpost_ref[...])


def _merge(u, h, o_nsa, o_gla, gr, o_x, g_gla_out, w_branch, w_mg, b_mg, w_out, g_post, tm):
    rows = h.shape[0]
    row = lambda w: pl.BlockSpec((tm, w), lambda i: (i, 0))
    full = lambda a: pl.BlockSpec(a.shape, lambda i: (0,) * a.ndim)
    return pl.pallas_call(
        _merge_kernel,
        grid=(rows // tm,),
        in_specs=[row(D_MODEL), row(D_MODEL), row(BRANCH_W), row(BRANCH_W), row(BRANCH_W), row(BRANCH_W),
                  full(g_gla_out), full(w_branch), full(w_mg), full(b_mg), full(w_out), full(g_post)],
        out_specs=row(D_MODEL),
        out_shape=jax.ShapeDtypeStruct((rows, D_MODEL), F32),
        compiler_params=_cparams(("parallel",)),
        name="merge",
    )(u, h, o_nsa, o_gla, gr, o_x, g_gla_out, w_branch, w_mg, b_mg, w_out, g_post)


def _page_copies(pt_ref, pool_ref, buf_ref, sem_ref, seq, slot):
    npages = buf_ref.shape[-1] // PAGE_SIZE
    return [pltpu.make_async_copy(pool_ref.at[pt_ref[seq, p]],
                                  buf_ref.at[slot, :, :, :, pl.ds(p * PAGE_SIZE, PAGE_SIZE)], sem_ref.at[slot])
            for p in range(npages)]


def _gather_pages(pt_ref, pool_ref, buf_ref, sem_ref):
    b = pl.program_id(0)
    slot = b % 2

    @pl.when(b == 0)
    def _():
        for cp in _page_copies(pt_ref, pool_ref, buf_ref, sem_ref, 0, 0):
            cp.start()

    @pl.when(b + 1 < pl.num_programs(0))
    def _():
        for cp in _page_copies(pt_ref, pool_ref, buf_ref, sem_ref, b + 1, 1 - slot):
            cp.start()

    for cp in _page_copies(pt_ref, pool_ref, buf_ref, sem_ref, b, slot):
        cp.wait()
    return slot


def _row_token(shape):
    return lax.broadcasted_iota(jnp.int32, shape, 0) % 8


def _tap_permutation():
    r = np.arange(2 * PAGE_SIZE)
    l, jj = r // CMP_STRIDE, r % CMP_STRIDE
    perm = np.zeros((2 * PAGE_SIZE, 2 * PAGE_SIZE), np.float32)
    perm[r, CMP_STRIDE * jj + l] = 1.0
    return jnp.asarray(perm, dtype=BF16)


def _sample_cmp_kernel(pt_ref, pool_ref, qbd_ref, perm_ref, wk_ref, wv_ref, bias_ref, w2k_ref, w2v_ref, ind_ref,
                       oc_ref, imp_ref, buf_ref, sem_ref, taps_ref, shift_ref, *, past):
    slot = _gather_pages(pt_ref, pool_ref, buf_ref, sem_ref)
    span = 2 * PAGE_SIZE
    nspan = buf_ref.shape[-1] // span
    nsub = nspan * CMP_STRIDE

    perm = perm_ref[...]
    for q in range(nspan):
        x = buf_ref[slot, :, :, :, q * span:(q + 1) * span]
        t = _dot_nt(perm, x.reshape(KV_SLAB, span).astype(BF16)).astype(BF16)
        for l in range(CMP_STRIDE):
            taps_ref[l, q * CMP_STRIDE:(q + 1) * CMP_STRIDE, :] = t[l * CMP_STRIDE:(l + 1) * CMP_STRIDE, :]

    abk, abv = _compress_ab(lambda l: taps_ref[l], wk_ref, wv_ref)
    kc = _compress_finish(abk, bias_ref[0, 0:1, :], w2k_ref, shift_ref).astype(BF16)
    vc = _compress_finish(abv, bias_ref[1, 0:1, :], w2v_ref, shift_ref).astype(BF16)
    qbd = qbd_ref[0]
    s = _dot_nt(qbd, kc)
    q_pos = past + _row_token(s.shape)
    mask = lax.broadcasted_iota(jnp.int32, s.shape, 1) * CMP_STRIDE + (CMP_LEN - 1) <= q_pos
    s = jnp.where(mask, s, NEG)
    p = jnp.exp2(s - jnp.max(s, axis=-1, keepdims=True))
    pn = jnp.where(mask, p * (1.0 / jnp.sum(p, axis=-1, keepdims=True)), 0.0)
    oc_ref[0] = _dot(pn.astype(BF16), vc)
    rows_g = NSA_HPG * 8
    psum = jnp.concatenate(
        [sum(pn[g * rows_g + h * 8:g * rows_g + (h + 1) * 8, :] for h in range(NSA_HPG)) for g in range(NSA_KV)],
        axis=0)
    imp_ref[0] = sum(_dot(part, ind_ref[...]) for part in _split3(psum))


def _sample_cmp(page_table, pool_t, qbd, wk2, wv2, bias, w2k, w2v, ind):
    nseq, npages = page_table.shape
    assert npages % 2 == 0
    past = npages * PAGE_SIZE
    nsub = past // CMP_STRIDE
    perm = _tap_permutation()
    full = lambda a: pl.BlockSpec(a.shape, lambda b, pt: (0,) * a.ndim)
    grid_spec = pltpu.PrefetchScalarGridSpec(
        num_scalar_prefetch=1,
        grid=(nseq,),
        in_specs=[pl.BlockSpec(memory_space=pl.ANY),
                  pl.BlockSpec((1,) + qbd.shape[1:], lambda b, pt: (b, 0, 0)),
                  full(perm), full(wk2), full(wv2), full(bias), full(w2k), full(w2v), full(ind)],
        out_specs=[pl.BlockSpec((1, 64, LANE), lambda b, pt: (b, 0, 0)),
                   pl.BlockSpec((1, 16, ind.shape[1]), lambda b, pt: (b, 0, 0))],
        scratch_shapes=[pltpu.VMEM((2, 2, NSA_KV, NSA_HD, past), F32),
                        pltpu.SemaphoreType.DMA((2,)),
                        pltpu.VMEM((CMP_STRIDE, nsub, KV_SLAB), BF16),
                        pltpu.VMEM((nsub + 8, 2 * CMP_HID), F32)],
    )
    return pl.pallas_call(
        functools.partial(_sample_cmp_kernel, past=past),
        grid_spec=grid_spec,
        out_shape=[jax.ShapeDtypeStruct((nseq, 64, LANE), F32),
                   jax.ShapeDtypeStruct((nseq, 16, ind.shape[1]), F32)],
        compiler_params=_cparams(("arbitrary",)),
        name="sample_cmp",
    )(page_table, pool_t, qbd, perm, wk2, wv2, bias, w2k, w2v, ind)


def _sample_select_kernel(imp_ref, sel_ref, *, past, n_real, n_top):
    imp_t = imp_ref[...].T
    jj = lax.broadcasted_iota(jnp.int32, imp_t.shape, 0)
    tok = lax.broadcasted_iota(jnp.int32, (1, imp_t.shape[1]), 1) % 8
    cur = (past + tok) // SLC_BLK
    sel_ref[...] = _select_blocks(_block_scores(imp_t, jj, cur, n_real), n_top).T


def _sample_select(imp, past, n_real, tm):
    rows, nbp = imp.shape
    return pl.pallas_call(
        functools.partial(_sample_select_kernel, past=past, n_real=n_real, n_top=min(SLC_TOPN, n_real)),
        grid=(rows // tm,),
        in_specs=[pl.BlockSpec((tm, nbp), lambda i: (i, 0))],
        out_specs=pl.BlockSpec((tm, nbp), lambda i: (i, 0)),
        out_shape=jax.ShapeDtypeStruct((rows, nbp), F32),
        compiler_params=_cparams(("parallel",)),
        name="sample_select",
    )(imp)


def _biased_softmax_pv(parts):
    ms = [s + bias for s, bias, _, _ in parts]
    m = functools.reduce(jnp.maximum, [jnp.max(x, axis=-1, keepdims=True) for x in ms])
    ps = [jnp.exp2(x - m) for x in ms]
    l = sum(jnp.sum(p, axis=-1, keepdims=True) for p in ps)
    o = sum((_dot_nt if part[3] else _dot)(p.astype(BF16), part[2]) for p, part in zip(ps, parts))
    return o * (1.0 / l)


def _sample_attn_kernel(pt_ref, pool_ref, qbd_ref, sel_ref, expand_ref, newslc_ref, win_ref, newwin_ref,
                        newwint_ref, gate_ref, oc_ref, o_ref, wout_ref, buf_ref, sem_ref):
    slot = _gather_pages(pt_ref, pool_ref, buf_ref, sem_ref)
    npast = buf_ref.shape[-1]
    nblk_past = npast // SLC_BLK
    qbd = qbd_ref[0]
    rows = qbd.shape[0]
    tok = _row_token((rows, LANE))
    lane = lax.broadcasted_iota(jnp.int32, (rows, LANE), 1)

    kt = buf_ref[slot, 0].reshape(NSA_KV * NSA_HD, npast).astype(BF16)
    vt = buf_ref[slot, 1].reshape(NSA_KV * NSA_HD, npast).astype(BF16)
    sel = sel_ref[0]
    selrows = jnp.concatenate([sel[g * 8:(g + 1) * 8, :] for g in range(NSA_KV) for _ in range(NSA_HPG)], axis=0)
    selb = (selrows - 1.0) * (-NEG)
    bias_p = _dot(selb[:, :nblk_past].astype(BF16), expand_ref[...])
    bias_n = jnp.where(lane <= tok, selb[:, nblk_past:nblk_past + 1], NEG)
    knew = newslc_ref[0, :, 0:LANE].astype(BF16)
    vnew = newslc_ref[0, :, LANE:KV_SLAB].astype(BF16)
    o_slc = _biased_softmax_pv([(_dot(qbd, kt), bias_p, vt, True), (_dot_nt(qbd, knew), bias_n, vnew, False)])

    wbuf = win_ref.shape[-1]
    lane_w = lax.broadcasted_iota(jnp.int32, (rows, wbuf), 1)
    bias_w = jnp.where(lane_w + (WINDOW - wbuf) > _row_token((rows, wbuf)), 0.0, NEG)
    kwn = newwin_ref[0, :, 0:LANE].astype(BF16)
    vwn = newwin_ref[0, :, LANE:KV_SLAB].astype(BF16)
    o_win = _biased_softmax_pv([(_dot(qbd, win_ref[0, 0].astype(BF16)), bias_w, win_ref[0, 1].astype(BF16), True),
                                (_dot_nt(qbd, kwn), jnp.where(lane <= tok, 0.0, NEG), vwn, False)])

    g = gate_ref[0]
    o_ref[0] = g[:, 0:1] * oc_ref[0] + g[:, 1:2] * o_slc + g[:, 2:3] * o_win

    tail = lax.broadcasted_iota(jnp.int32, (NSA_KV * NSA_HD, LANE), 1) >= LANE - 8
    for kv in range(2):
        rolled = pltpu.roll(win_ref[0, kv], wbuf - 8, axis=1)
        wout_ref[0, kv, :, 0:wbuf - LANE] = rolled[:, 0:wbuf - LANE]
        wout_ref[0, kv, :, wbuf - LANE:wbuf] = jnp.where(tail, newwint_ref[0, kv], rolled[:, wbuf - LANE:wbuf])


def _sample_attn(page_table, pool_t, qbd, sel, expand, newslc, win_t, newwin, newwin_t, gates, o_cmp):
    nseq, npages = page_table.shape
    per_seq = lambda a: pl.BlockSpec((1,) + a.shape[1:], lambda b, pt: (b,) + (0,) * (a.ndim - 1))
    grid_spec = pltpu.PrefetchScalarGridSpec(
        num_scalar_prefetch=1,
        grid=(nseq,),
        in_specs=[pl.BlockSpec(memory_space=pl.ANY), per_seq(qbd), per_seq(sel),
                  pl.BlockSpec(expand.shape, lambda b, pt: (0, 0)),
                  per_seq(newslc), per_seq(win_t), per_seq(newwin), per_seq(newwin_t), per_seq(gates),
                  per_seq(o_cmp)],
        out_specs=[per_seq(o_cmp), per_seq(win_t)],
        scratch_shapes=[pltpu.VMEM((2, 2, NSA_KV, NSA_HD, npages * PAGE_SIZE), F32),
                        pltpu.SemaphoreType.DMA((2,))],
    )
    return pl.pallas_call(
        _sample_attn_kernel,
        grid_spec=grid_spec,
        out_shape=[jax.ShapeDtypeStruct(o_cmp.shape, F32), jax.ShapeDtypeStruct(win_t.shape, F32)],
        compiler_params=_cparams(("arbitrary",)),
        name="sample_attn",
    )(page_table, pool_t, qbd, sel, expand, newslc, win_t, newwin, newwin_t, gates, o_cmp)


def _nsa_sample_path(q, kv_cmp, kv_slc, kv_win, ng, pool_cmp, pool_slc, win_buf, page_table, w_cmp1, pe_cmp, w_cmp2):
    nseq, npages = page_table.shape
    pos_minor = lambda a: jnp.transpose(a, (0, 2, 3, 4, 1))
    wbuf = win_buf.shape[1]
    win_t = pos_minor(win_buf).reshape(nseq, 2, NSA_KV * NSA_HD, wbuf)
    tn = q.shape[0] // nseq
    past = npages * PAGE_SIZE
    n_real = -(-(past + tn) // SLC_BLK)
    nbp = -(-n_real // (2 * LANE)) * (2 * LANE)
    nsub = past // CMP_STRIDE
    assert tn == 8 and past % SLC_BLK == 0 and (past + tn - CMP_LEN) // CMP_STRIDE + 1 == nsub - 1

    qg = q.reshape(nseq, tn, NSA_KV, NSA_HPG, NSA_HD).transpose(0, 2, 3, 1, 4).reshape(nseq, NSA_KV, NSA_HPG * tn,
                                                                                        NSA_HD)
    z = jnp.zeros_like(qg[:, 0])
    qbd = jnp.stack([jnp.concatenate([qg[:, 0], z], axis=-1), jnp.concatenate([z, qg[:, 1]], axis=-1)], axis=1)
    qbd = (qbd.reshape(nseq, NSA_HEADS * tn, LANE) * LOG2E).astype(BF16)
    gates = ng[:, :_NG].reshape(nseq, tn, NSA_HEADS, 3).transpose(0, 2, 1, 3).reshape(nseq, NSA_HEADS * tn, 3)
    gates = jnp.pad(gates, ((0, 0), (0, 0), (0, LANE - 3)))

    def new_block(kv):
        return jnp.pad(kv.reshape(nseq, tn, KV_SLAB), ((0, 0), (0, LANE - tn), (0, 0)))

    wk2, wv2 = _pack_cmp_w1(w_cmp1[0]), _pack_cmp_w1(w_cmp1[1])
    w2k, w2v = _pack_cmp_w2(w_cmp2[0]), _pack_cmp_w2(w_cmp2[1])
    bias = _cmp_bias(pe_cmp, w_cmp1)
    ind = _cmp_to_slc_t(nsub, nbp).T
    o_cmp, imp = _sample_cmp(page_table, pos_minor(pool_cmp), qbd, wk2, wv2, bias, w2k, w2v, ind)
    sel = _sample_select(imp.reshape(nseq * 2 * tn, nbp), past, n_real, 512).reshape(nseq, 2 * tn, nbp)
    nblk_past = past // SLC_BLK
    expand = jnp.asarray(np.repeat(np.eye(nblk_past, dtype=np.float32), SLC_BLK, axis=1), dtype=BF16)
    newwin_t = jnp.pad(kv_win.reshape(nseq, tn, KV_SLAB).transpose(0, 2, 1), ((0, 0), (0, 0), (LANE - tn, 0)))
    newwin_t = newwin_t.reshape(nseq, 2, NSA_KV * NSA_HD, LANE)
    o, win_new_t = _sample_attn(page_table, pos_minor(pool_slc), qbd, sel, expand, new_block(kv_slc), win_t,
                                new_block(kv_win), newwin_t, gates, o_cmp)
    o = o.reshape(nseq, NSA_KV, NSA_HPG, tn, NSA_KV, NSA_HD)
    o = jnp.stack([o[:, g, :, :, g, :] for g in range(NSA_KV)], axis=1)
    win_new = jnp.transpose(win_new_t.reshape(nseq, 2, NSA_KV, NSA_HD, wbuf), (0, 4, 1, 2, 3))
    return o.transpose(0, 3, 1, 2, 4).reshape(nseq * tn, BRANCH_W), win_new


def _mixer(h, mem_kv, nsa_fn, gla_s0, gla_cfg, wts, tm, x_tm):
    bsz, t, _ = h.shape
    rows = bsz * t
    h2 = h.reshape(rows, D_MODEL)
    (u, q, kcmp, kslc, kwin, ng, gq, gk, gv, la, gr, xq) = _in_proj(h2, wts['g_pre_mix'], wts['w_cat'], wts['w2_pad'],
                                                                   wts['b_gla_g'], tm)
    o_nsa, extra = nsa_fn(q, kcmp, kslc, kwin, ng)
    r3 = lambda a: a.reshape(bsz, t, a.shape[-1])
    chunk, nchunk, exact, sb = gla_cfg
    o_gla, s_fin = _gla(r3(gq), r3(gk), r3(gv), r3(la), gla_s0.reshape(bsz, GLA_HEADS // 2, LANE, LANE), chunk,
                        nchunk, exact, sb)
    o_x = _xattn(r3(xq), mem_kv, x_tm, sb)
    h_new = _merge(u, h2, o_nsa, o_gla.reshape(rows, BRANCH_W), gr, o_x.reshape(rows, BRANCH_W), wts['g_gla_out'],
                   wts['w_branch'], wts['w_mg'], wts['b_mg'], wts['w_out'], wts['g_post_mix'], tm)
    s_fin = s_fin.reshape(bsz, GLA_HEADS, GLA_DK, GLA_DV)
    return h_new.reshape(bsz, t, D_MODEL), (kcmp, kslc, kwin), s_fin, extra


def kernel(x_prompt, x_sample, mem_prompt, cache_cmp_kv, cache_slc_kv, cache_win_kv, state_gla, cache_mem_kv,
           page_table, g_pre_ff1, g_post_ff1, w_ff1_gu, w_ff1_d, g_pre_mix, g_post_mix, w_in, w_cmp1, pe_cmp,
           w_cmp2, w_gla_g2, b_gla_g, g_gla_out, g_mem, w_mem_kv, w_branch, w_merge_gate, b_merge_gate, w_out,
           g_pre_ff2, g_post_ff2, w_ff2_gu, w_ff2_d):
    depth = w_in.shape[0]
    assert depth == 1
    l = 0
    bp, tp, _ = x_prompt.shape
    bs, ts, _ = x_sample.shape
    assert bp == 1
    row1 = lambda a: a.reshape(1, -1)
    wts = dict(
        g_pre_mix=row1(g_pre_mix[l]), w_cat=_pack_w_in(w_in[l]),
        w2_pad=jnp.zeros((LANE, GLA_HEADS * GLA_DK), F32).at[_NG:_NG + GLA_RANK].set(w_gla_g2[l]).astype(BF16),
        b_gla_g=row1(b_gla_g[l]), g_gla_out=row1(g_gla_out[l]), w_branch=w_branch[l].astype(BF16),
        w_mg=w_merge_gate[l].astype(BF16), b_mg=row1(b_merge_gate[l]), w_out=w_out[l].astype(BF16),
        g_post_mix=row1(g_post_mix[l]))

    def ffn(x, g_pre, g_post, w_gu, w_d, tm):
        b, t, _ = x.shape
        return _ffn_half(x.reshape(b * t, D_MODEL), row1(g_pre), row1(g_post), w_gu, w_d, tm).reshape(x.shape)

    w1gu, w1d = w_ff1_gu[l].astype(BF16), w_ff1_d[l].astype(BF16)
    w2gu, w2d = w_ff2_gu[l].astype(BF16), w_ff2_d[l].astype(BF16)
    tm_p, tm_s = 512, 256

    hp = ffn(x_prompt, g_pre_ff1[l], g_post_ff1[l], w1gu, w1d, tm_p)
    mem_kv = _mem_proj(mem_prompt[0], row1(g_mem[l]), w_mem_kv[l].astype(BF16))
    nsa_p = lambda q, kc, ks, kw, ng: (_nsa_prompt_path(q, kc, ks, kw, ng, w_cmp1[l], pe_cmp[l], w_cmp2[l]), None)
    s0 = jnp.zeros((bp, GLA_HEADS, GLA_DK, GLA_DV), F32)
    hp, (cmp_p, slc_p, win_p), gla_p, _ = _mixer(hp, mem_kv.reshape(1, -1, X_HD), nsa_p, s0, (2 * GLA_CHUNK, 4, False, 1), wts, tm_p, tm_p)
    yp = ffn(hp, g_pre_ff2[l], g_post_ff2[l], w2gu, w2d, tm_p)

    hs = ffn(x_sample, g_pre_ff1[l], g_post_ff1[l], w1gu, w1d, tm_s)
    wbuf = cache_win_kv.shape[2]
    nsa_s = lambda q, kc, ks, kw, ng: _nsa_sample_path(q, kc, ks, kw, ng, cache_cmp_kv[l], cache_slc_kv[l],
                                                       cache_win_kv[l], page_table, w_cmp1[l], pe_cmp[l], w_cmp2[l])
    mem_s = cache_mem_kv[l].reshape(bs, -1, X_HD)
    hs, (cmp_s, slc_s, _), gla_s, win_s = _mixer(hs, mem_s, nsa_s, state_gla[l], (min(GLA_CHUNK, ts), 1, True, 8), wts,
                                                 tm_s, ts)
    ys = ffn(hs, g_pre_ff2[l], g_post_ff2[l], w2gu, w2d, tm_s)

    kv5 = lambda a, b, t: a.reshape(1, b, t, 2, NSA_KV, NSA_HD)
    wp = min(WINDOW, tp)
    return (yp, ys,
            kv5(cmp_p, bp, tp), kv5(slc_p, bp, tp), kv5(win_p.reshape(bp, tp, KV_SLAB)[:, tp - wp:], bp, wp),
            gla_p[None], mem_kv.reshape(1, bp, mem_kv.shape[0], 2, X_HEADS, X_HD),
            kv5(cmp_s, bs, ts), kv5(slc_s, bs, ts), kv5(win_s, bs, wbuf), gla_s[None])
```

```python
import functools

import numpy as np
import jax
import jax.numpy as jnp
from jax import lax
from jax.experimental import pallas as pl
from jax.experimental.pallas import tpu as pltpu

F32 = jnp.float32
BF16 = jnp.bfloat16

D_MODEL = 1024
BRANCH_W = D_MODEL // 2
NSA_HEADS = 8
NSA_KV = 2
NSA_HPG = NSA_HEADS // NSA_KV
NSA_HD = BRANCH_W // NSA_HEADS
CMP_LEN = 32
CMP_STRIDE = 16
CMP_HID = 128
SLC_BLK = 64
SLC_TOPN = 16
WINDOW = 512
GLA_HEADS = 4
GLA_DV = BRANCH_W // GLA_HEADS
GLA_DK = GLA_DV // 2
GLA_RANK = 16
GLA_TAU = 16.0
GLA_CHUNK = 64
X_HEADS = 4
X_HD = BRANCH_W // X_HEADS
D_FF = 2816
EPS = 1e-6
NEG = -1e30
FORCE = 1e6
PAGE_SIZE = 128

LANE = 128
BF16_ROWS = 16
VT_ROWS = NSA_HD + BF16_ROWS
LOG2E = 1.4426950408889634
KV_SLAB = 2 * NSA_KV * NSA_HD
VMEM_LIMIT = 56 * 1024 * 1024


def _cparams(sem):
    return pltpu.CompilerParams(dimension_semantics=sem, vmem_limit_bytes=VMEM_LIMIT)


def _rms(x, g):
    xf = x.astype(F32)
    return xf * lax.rsqrt(jnp.mean(xf * xf, axis=-1, keepdims=True) + EPS) * g


def _silu(x):
    return x * jax.nn.sigmoid(x)


def _dot(a, b):
    return jnp.dot(a, b, preferred_element_type=F32)


def _dot_nt(a, b):
    return lax.dot_general(a, b, (((1,), (1,)), ((), ())), preferred_element_type=F32)


def _dot_tn(a, b):
    return lax.dot_general(a, b, (((0,), (0,)), ((), ())), preferred_element_type=F32)


def _split3(x):
    hi = x.astype(BF16)
    r1 = x - hi.astype(F32)
    mid = r1.astype(BF16)
    lo = (r1 - mid.astype(F32)).astype(BF16)
    return hi, mid, lo


def _ffn_kernel(x_ref, gpre_ref, gpost_ref, wg_ref, wu_ref, wd_ref, o_ref, xn_ref, acc_ref):
    f = pl.program_id(1)

    @pl.when(f == 0)
    def _():
        xn_ref[...] = _rms(x_ref[...], gpre_ref[...]).astype(BF16)
        acc_ref[...] = jnp.zeros_like(acc_ref)

    xn = xn_ref[...]
    a = _dot(xn, wg_ref[...])
    b = _dot(xn, wu_ref[...])
    acc_ref[...] += _dot((_silu(a) * b).astype(BF16), wd_ref[...])

    @pl.when(f == pl.num_programs(1) - 1)
    def _():
        o_ref[...] = x_ref[...] + 0.5 * _rms(acc_ref[...], gpost_ref[...])


def _ffn_half(x, g_pre, g_post, w_gu, w_d, tm):
    rows = x.shape[0]
    tf = D_FF // 2
    nf = D_FF // tf
    return pl.pallas_call(
        _ffn_kernel,
        grid=(rows // tm, nf),
        in_specs=[
            pl.BlockSpec((tm, D_MODEL), lambda i, f: (i, 0)),
            pl.BlockSpec((1, D_MODEL), lambda i, f: (0, 0)),
            pl.BlockSpec((1, D_MODEL), lambda i, f: (0, 0)),
            pl.BlockSpec((D_MODEL, tf), lambda i, f: (0, f)),
            pl.BlockSpec((D_MODEL, tf), lambda i, f: (0, nf + f)),
            pl.BlockSpec((tf, D_MODEL), lambda i, f: (f, 0)),
        ],
        out_specs=pl.BlockSpec((tm, D_MODEL), lambda i, f: (i, 0)),
        out_shape=jax.ShapeDtypeStruct((rows, D_MODEL), F32),
        scratch_shapes=[pltpu.VMEM((tm, D_MODEL), BF16), pltpu.VMEM((tm, D_MODEL), F32)],
        compiler_params=_cparams(("parallel", "arbitrary")),
        name="ffn_half",
    )(x, g_pre, g_post, w_gu, w_gu, w_d)


_NG = NSA_HEADS * 3
_C_Q, _C_CMP, _C_SLC, _C_WIN, _C_GQ, _C_GK, _C_GV, _C_GR, _C_XQ, _C_SM, _C_END = (
    0, 512, 768, 1024, 1280, 1536, 1792, 2304, 2816, 3328, 3456)


def _pack_w_in(w_in):
    offs = np.cumsum([0, NSA_HEADS * NSA_HD, 3 * KV_SLAB, _NG, GLA_HEADS * GLA_DK, GLA_HEADS * GLA_DK,
                      GLA_HEADS * GLA_DV, GLA_RANK, GLA_HEADS * GLA_DV, X_HEADS * X_HD])
    q, kv, ng, gq, gk, gv, glr, gr, xq = [w_in[:, offs[i]:offs[i + 1]] for i in range(9)]
    pad = jnp.zeros((D_MODEL, LANE - _NG - GLA_RANK), w_in.dtype)
    return jnp.concatenate([q, kv, gq, gk, gv, gr, xq, ng, glr, pad], axis=1).astype(BF16)


def _in_proj_kernel(h_ref, g_ref, w_ref, w2_ref, b2_ref,
                    u_ref, q_ref, cmp_ref, slc_ref, win_ref, ng_ref, gq_ref, gk_ref, gv_ref, la_ref, gr_ref,
                    xq_ref):
    u = _rms(h_ref[...], g_ref[...]).astype(BF16)
    u_ref[...] = u

    def proj(lo, hi):
        return _dot(u, w_ref[:, lo:hi])

    q_ref[...] = proj(_C_Q, _C_CMP) * (NSA_HD ** -0.5)
    cmp_ref[...] = proj(_C_CMP, _C_SLC)
    slc_ref[...] = proj(_C_SLC, _C_WIN)
    win_ref[...] = proj(_C_WIN, _C_GQ)
    gq_ref[...] = proj(_C_GQ, _C_GK) * (GLA_DK ** -0.5)
    gk_ref[...] = proj(_C_GK, _C_GV)
    gv_ref[...] = proj(_C_GV, _C_GR)
    gr_ref[...] = proj(_C_GR, _C_XQ)
    xq_ref[...] = proj(_C_XQ, _C_SM) * (X_HD ** -0.5)
    small = proj(_C_SM, _C_END)
    ng_ref[...] = jax.nn.sigmoid(small)
    z = _dot(small.astype(BF16), w2_ref[...]) + b2_ref[...]
    log_sig = jnp.minimum(z, 0.0) - jnp.log1p(jnp.exp(-jnp.abs(z)))
    la_ref[...] = log_sig / GLA_TAU


def _in_proj(h, g_pre_mix, w_cat, w2_pad, b2, tm):
    rows = h.shape[0]
    widths = [(D_MODEL, BF16), (512, F32), (256, F32), (256, F32), (256, F32), (LANE, F32), (256, F32),
              (256, F32), (512, F32), (256, F32), (512, F32), (512, F32)]
    return pl.pallas_call(
        _in_proj_kernel,
        grid=(rows // tm,),
        in_specs=[
            pl.BlockSpec((tm, D_MODEL), lambda i: (i, 0)),
            pl.BlockSpec((1, D_MODEL), lambda i: (0, 0)),
            pl.BlockSpec((D_MODEL, _C_END), lambda i: (0, 0)),
            pl.BlockSpec((LANE, 256), lambda i: (0, 0)),
            pl.BlockSpec((1, 256), lambda i: (0, 0)),
        ],
        out_specs=[pl.BlockSpec((tm, w), lambda i: (i, 0)) for w, _ in widths],
        out_shape=[jax.ShapeDtypeStruct((rows, w), dt) for w, dt in widths],
        compiler_params=_cparams(("parallel",)),
        name="in_proj",
    )(h, g_pre_mix, w_cat, w2_pad, b2)


def _pack_cmp_w1(w1):
    wa, wb = w1[:CMP_STRIDE], w1[CMP_STRIDE:]
    z = jnp.zeros_like(wa)
    rows = []
    for g in range(NSA_KV):
        cols = []
        for half in (wa, wb):
            for g2 in range(NSA_KV):
                cols.append(half if g2 == g else z)
        rows.append(jnp.concatenate(cols, axis=2))
    w = jnp.stack(rows, axis=1)
    return w.reshape(CMP_STRIDE // 2, 2 * NSA_KV * NSA_HD, 4 * CMP_HID).astype(BF16)


def _pack_cmp_w2(w2):
    z = jnp.zeros_like(w2)
    return jnp.concatenate([jnp.concatenate([w2, z], axis=1), jnp.concatenate([z, w2], axis=1)], axis=0).astype(BF16)


def _compress_ab(load_pos, wk_ref, wv_ref):
    hk = hv = None
    for lp in range(CMP_STRIDE // 2):
        xa = load_pos(2 * lp)
        xb = load_pos(2 * lp + 1)
        lk = jnp.concatenate([xa[:, :LANE], xb[:, :LANE]], axis=1).astype(BF16)
        lv = jnp.concatenate([xa[:, LANE:], xb[:, LANE:]], axis=1).astype(BF16)
        dk = _dot(lk, wk_ref[lp])
        dv = _dot(lv, wv_ref[lp])
        hk = dk if hk is None else hk + dk
        hv = dv if hv is None else hv + dv
    return hk, hv


def _compress_finish(ab, bias, w2_ref, shift_ref):
    n = ab.shape[0]
    shift_ref[0:n, :] = ab[:, 2 * CMP_HID:]
    shift_ref[n:n + 8, :] = jnp.zeros((8, 2 * CMP_HID), F32)
    h = ab[:, :2 * CMP_HID] + shift_ref[pl.ds(1, n), :] + bias
    return _dot(_silu(h).astype(BF16), w2_ref[...])


def _cmp_bias_kernel(pe_ref, w1_ref, o_ref):
    for kv in range(2):
        b = _dot(pe_ref[kv].astype(BF16), w1_ref[kv])
        o_ref[kv] = jnp.concatenate([b, b], axis=1)


def _cmp_bias(pe_cmp, w_cmp1):
    pe = jnp.broadcast_to(pe_cmp.reshape(2, 1, CMP_LEN * NSA_HD), (2, 8, CMP_LEN * NSA_HD))
    w1 = w_cmp1.reshape(2, CMP_LEN * NSA_HD, CMP_HID).astype(BF16)
    return pl.pallas_call(
        _cmp_bias_kernel,
        out_shape=jax.ShapeDtypeStruct((2, 8, 2 * CMP_HID), F32),
        name="cmp_bias",
    )(pe, w1)


def _nsa_prep_kernel(cmpv_ref, slc_ref, win_ref, wk_ref, wv_ref,
                     abk_ref, abv_ref, kslc_ref, vslct_ref, kwin_ref, vwint_ref):
    abk, abv = _compress_ab(lambda l: cmpv_ref[:, l * KV_SLAB:(l + 1) * KV_SLAB], wk_ref, wv_ref)
    abk_ref[...] = abk
    abv_ref[...] = abv
    nchunk = slc_ref.shape[0] // LANE
    ones_row = (lax.broadcasted_iota(jnp.int32, (VT_ROWS - NSA_HD, LANE), 0) == 0).astype(F32)
    for src, k_ref, vt_ref in ((slc_ref, kslc_ref, vslct_ref), (win_ref, kwin_ref, vwint_ref)):
        k_ref[...] = src[:, :LANE].astype(BF16)
        for c in range(nchunk):
            vt = src[c * LANE:(c + 1) * LANE, LANE:].T
            vt_ref[c] = jnp.concatenate([vt[:NSA_HD], ones_row, vt[NSA_HD:], ones_row], axis=0).astype(BF16)


def _nsa_prep(kv_cmp, kv_slc, kv_win, wk2, wv2, tm):
    t = kv_cmp.shape[0]
    nsub = t // CMP_STRIDE
    cmp_view = kv_cmp.reshape(nsub, CMP_STRIDE * KV_SLAB)
    const3 = lambda i: (0, 0, 0)
    return pl.pallas_call(
        _nsa_prep_kernel,
        grid=(t // tm,),
        in_specs=[
            pl.BlockSpec((tm // CMP_STRIDE, CMP_STRIDE * KV_SLAB), lambda i: (i, 0)),
            pl.BlockSpec((tm, KV_SLAB), lambda i: (i, 0)),
            pl.BlockSpec((tm, KV_SLAB), lambda i: (i, 0)),
            pl.BlockSpec(wk2.shape, const3),
            pl.BlockSpec(wv2.shape, const3),
        ],
        out_specs=[
            pl.BlockSpec((tm // CMP_STRIDE, 4 * CMP_HID), lambda i: (i, 0)),
            pl.BlockSpec((tm // CMP_STRIDE, 4 * CMP_HID), lambda i: (i, 0)),
            pl.BlockSpec((tm, LANE), lambda i: (i, 0)),
            pl.BlockSpec((tm // LANE, NSA_KV * VT_ROWS, LANE), lambda i: (i, 0, 0)),
            pl.BlockSpec((tm, LANE), lambda i: (i, 0)),
            pl.BlockSpec((tm // LANE, NSA_KV * VT_ROWS, LANE), lambda i: (i, 0, 0)),
        ],
        out_shape=[
            jax.ShapeDtypeStruct((nsub, 4 * CMP_HID), F32),
            jax.ShapeDtypeStruct((nsub, 4 * CMP_HID), F32),
            jax.ShapeDtypeStruct((t, LANE), BF16),
            jax.ShapeDtypeStruct((t // LANE, NSA_KV * VT_ROWS, LANE), BF16),
            jax.ShapeDtypeStruct((t, LANE), BF16),
            jax.ShapeDtypeStruct((t // LANE, NSA_KV * VT_ROWS, LANE), BF16),
        ],
        compiler_params=_cparams(("parallel",)),
        name="nsa_prep",
    )(cmp_view, kv_slc, kv_win, wk2, wv2)


def _cmp_finish_kernel(abk_ref, abv_ref, bias_ref, w2k_ref, w2v_ref, kc_ref, vct_ref, shift_ref):
    n = abk_ref.shape[0]
    kc_ref[...] = _compress_finish(abk_ref[...], bias_ref[0, 0:1, :], w2k_ref, shift_ref).astype(BF16)
    vc = _compress_finish(abv_ref[...], bias_ref[1, 0:1, :], w2v_ref, shift_ref)
    for c in range(n // LANE):
        vct_ref[:, c * LANE:(c + 1) * LANE] = vc[c * LANE:(c + 1) * LANE, :].T.astype(BF16)


def _cmp_finish(abk, abv, bias, w2k, w2v):
    n = abk.shape[0]
    return pl.pallas_call(
        _cmp_finish_kernel,
        out_shape=[jax.ShapeDtypeStruct((n, LANE), BF16), jax.ShapeDtypeStruct((LANE, n), BF16)],
        scratch_shapes=[pltpu.VMEM((n + 8, 2 * CMP_HID), F32)],
        compiler_params=pltpu.CompilerParams(vmem_limit_bytes=VMEM_LIMIT),
        name="cmp_finish",
    )(abk, abv, bias, w2k, w2v)


def _select_blocks(score, n_top):
    nb = score.shape[0]
    jj = lax.broadcasted_iota(jnp.int32, score.shape, 0)
    taken = jnp.float32(-3e38)
    for _ in range(n_top):
        m = jnp.max(score, axis=0, keepdims=True)
        idx = jnp.min(jnp.where(score == m, jj, nb), axis=0, keepdims=True)
        score = jnp.where(jj == idx, taken, score)
    return (score == taken).astype(F32)


def _block_scores(imp, jj, cur, n_real):
    s = jnp.where(jj == 0, FORCE, jnp.where(jj == cur, FORCE, jnp.where(jj == cur - 1, FORCE, imp)))
    s = jnp.where(jj <= cur, s, NEG)
    return jnp.where(jj < n_real, s, -jnp.inf)


def _cmp_to_slc_t(nc_pad, nb_pad):
    i = np.arange(nc_pad)[None, :]
    j = np.arange(nb_pad)[:, None]
    ov = (i * CMP_STRIDE < (j + 1) * SLC_BLK) & (i * CMP_STRIDE + CMP_LEN > j * SLC_BLK)
    return jnp.asarray(ov.astype(np.float32), dtype=BF16)


TQ = LANE


SLC_CK = 8 * TQ


def _flash_step(qbd, k, vt, bias, m_ref, acc_ref):
    s = _dot(k, qbd)
    for g in range(NSA_KV):
        ps = []
        for h in range(NSA_HPG):
            lo = (g * NSA_HPG + h) * TQ
            sc = s[:, lo:lo + TQ] + bias[g]
            m_old = m_ref[:, lo:lo + TQ]
            m_new = jnp.maximum(m_old, jnp.max(sc, axis=0, keepdims=True))
            ps.append(jnp.exp2((sc - m_new).astype(BF16)))
            m_ref[:, lo:lo + TQ] = m_new
            acc_ref[g, :, h * TQ:(h + 1) * TQ] = acc_ref[g, :, h * TQ:(h + 1) * TQ] * jnp.exp2(m_old - m_new)
        acc_ref[g] += _dot(vt[g * VT_ROWS:(g + 1) * VT_ROWS, :], jnp.concatenate(ps, axis=1))


def _softmax_once(qbd, k, vt, bias):
    s = _dot(k, qbd)
    outs = []
    for g in range(NSA_KV):
        ps = []
        for h in range(NSA_HPG):
            lo = (g * NSA_HPG + h) * TQ
            sc = s[:, lo:lo + TQ] + bias
            ps.append(jnp.exp2((sc - jnp.max(sc, axis=0, keepdims=True)).astype(BF16)))
        o = _dot(vt[g * VT_ROWS:(g + 1) * VT_ROWS, :], jnp.concatenate(ps, axis=1))
        outs.append(o[:NSA_HD] * (1.0 / o[NSA_HD:NSA_HD + 1]))
    return outs


def _flash_reset(m_ref, acc_ref):
    m_ref[...] = jnp.full(m_ref.shape, NEG, F32)
    acc_ref[...] = jnp.zeros(acc_ref.shape, F32)


def _flash_result(acc_ref):
    return [acc_ref[g, :NSA_HD, :] * (1.0 / acc_ref[g, NSA_HD:NSA_HD + 1, :]) for g in range(NSA_KV)]


def _nsa_prompt_kernel(q_ref, ng_ref, kc_ref, vct_ref, indt_ref, kslc_ref, vslct_ref, kwin_ref, vwint_ref,
                       o_ref, selb_ref, m_ref, acc_ref, ocmp_ref):
    i = pl.program_id(0)
    s0 = i * TQ
    ncp = kc_ref.shape[0]
    nbp = indt_ref.shape[0]
    n_blocks = kslc_ref.shape[0] // SLC_BLK

    qt = jnp.concatenate([q_ref[:, c * LANE:(c + 1) * LANE].T for c in range(BRANCH_W // LANE)], axis=0)
    zero = jnp.zeros((NSA_HD, NSA_HPG * TQ), F32)
    rows = []
    for g in range(NSA_KV):
        blk = jnp.concatenate(
            [qt[(g * NSA_HPG + h) * NSA_HD:(g * NSA_HPG + h + 1) * NSA_HD, :] for h in range(NSA_HPG)], axis=1)
        rows.append(jnp.concatenate([blk, zero] if g == 0 else [zero, blk], axis=1))
    qbd = (jnp.concatenate(rows, axis=0) * LOG2E).astype(BF16)

    q_lane = s0 + lax.broadcasted_iota(jnp.int32, (1, TQ), 1)

    n_win = WINDOW + TQ
    w0 = pl.multiple_of(jnp.maximum(s0 - WINDOW, 0), TQ)
    dist = q_lane - (w0 + lax.broadcasted_iota(jnp.int32, (n_win, TQ), 0))
    bias_w = jnp.where(dist >= 0, jnp.where(dist < WINDOW, 0.0, NEG), NEG)
    vt_w = jnp.concatenate([vwint_ref[w0 // TQ + j] for j in range(n_win // TQ)], axis=1)
    o_win = _softmax_once(qbd, kwin_ref[pl.ds(w0, n_win), :], vt_w, bias_w)

    col_ok = q_lane >= CMP_LEN - 1
    cur = q_lane // SLC_BLK

    def cmp_and_select(nb_eff):
        nc_eff = nb_eff * (SLC_BLK // CMP_STRIDE)
        c_end = lax.broadcasted_iota(jnp.int32, (nc_eff, TQ), 0) * CMP_STRIDE + (CMP_LEN - 1)
        bias_c = jnp.where(c_end <= q_lane, 0.0, NEG)
        jj = lax.broadcasted_iota(jnp.int32, (nb_eff, TQ), 0)
        scores = []
        for g in range(NSA_KV):
            sc_all = _dot(kc_ref[0:nc_eff, :], qbd[:, g * NSA_HPG * TQ:(g + 1) * NSA_HPG * TQ])
            psum = jnp.zeros((nc_eff, TQ), F32)
            ps = []
            for h in range(NSA_HPG):
                sc = sc_all[:, h * TQ:(h + 1) * TQ] + bias_c
                p = jnp.exp2(sc - jnp.max(sc, axis=0, keepdims=True))
                pn = p * jnp.where(col_ok, 1.0 / jnp.sum(p, axis=0, keepdims=True), 0.0)
                psum = psum + pn
                ps.append(pn.astype(BF16))
            ocmp_ref[g] = _dot(vct_ref[g * NSA_HD:(g + 1) * NSA_HD, 0:nc_eff], jnp.concatenate(ps, axis=1))
            imp = sum(_dot(indt_ref[0:nb_eff, 0:nc_eff], part) for part in _split3(psum))
            scores.append(_block_scores(imp, jj, cur, n_blocks))
        sel = _select_blocks(jnp.concatenate(scores, axis=1), min(SLC_TOPN, n_blocks))
        for g in range(NSA_KV):
            selb_ref[g, 0:nb_eff, :] = (sel[:, g * TQ:(g + 1) * TQ] - 1.0) * (-NEG)
            if nb_eff < nbp:
                selb_ref[g, nb_eff:nbp, :] = jnp.full((nbp - nb_eff, TQ), NEG, F32)

    sizes = [nbp]
    while len(sizes) < 3 and sizes[0] % 64 == 0:
        sizes.insert(0, sizes[0] // 2)
    lo = 0
    for nb_eff in sizes:
        hi = nb_eff // 2 if nb_eff < nbp else pl.num_programs(0)

        @pl.when((i >= lo) & (i < hi))
        def _(nb_eff=nb_eff):
            cmp_and_select(nb_eff)
        lo = hi
    o_cmp = [ocmp_ref[g] for g in range(NSA_KV)]

    blk_per_step = SLC_CK // SLC_BLK

    def slc_inputs(c):
        k = kslc_ref[pl.ds(pl.multiple_of(c * SLC_CK, SLC_CK), SLC_CK), :]
        vt = jnp.concatenate([vslct_ref[c * (SLC_CK // TQ) + j] for j in range(SLC_CK // TQ)], axis=1)
        bias = []
        for g in range(NSA_KV):
            rows = selb_ref[g, pl.ds(pl.multiple_of(c * blk_per_step, blk_per_step), blk_per_step), :]
            bias.append(jnp.concatenate(
                [jnp.broadcast_to(rows[j:j + 1, :], (SLC_BLK, TQ)) for j in range(blk_per_step)], axis=0))
        return k, vt, bias

    _flash_reset(m_ref, acc_ref)

    def slc_step(c):
        k, vt, bias = slc_inputs(c)
        _flash_step(qbd, k, vt, bias, m_ref, acc_ref)

    def slc_pair(t, carry):
        slc_step(2 * t)
        slc_step(2 * t + 1)
        return carry

    c_last = s0 // SLC_CK
    lax.fori_loop(0, c_last // 2, slc_pair, 0)

    @pl.when(c_last % 2 == 1)
    def _():
        slc_step(c_last - 1)
    k, vt, bias = slc_inputs(c_last)
    key_pos = c_last * SLC_CK + lax.broadcasted_iota(jnp.int32, (SLC_CK, TQ), 0)
    causal = key_pos <= q_lane
    _flash_step(qbd, k, vt, [jnp.where(causal, b, NEG) for b in bias], m_ref, acc_ref)
    o_slc = _flash_result(acc_ref)

    ngt = ng_ref[...].T
    for g in range(NSA_KV):
        heads = []
        for h in range(NSA_HPG):
            r = (g * NSA_HPG + h) * 3
            cs = slice(h * TQ, (h + 1) * TQ)
            heads.append(ngt[r:r + 1, :] * o_cmp[g][:, cs] + ngt[r + 1:r + 2, :] * o_slc[g][:, cs]
                         + ngt[r + 2:r + 3, :] * o_win[g][:, cs])
        for pair in range(NSA_HPG // 2):
            blk = jnp.concatenate(heads[2 * pair:2 * pair + 2], axis=0).T
            col = (g * NSA_HPG // 2 + pair) * LANE
            o_ref[:, col:col + LANE] = blk


def _nsa_prompt(q, ng, kc, vct, indt, kslc, vslct, kwin, vwint):
    t = q.shape[0]
    full2 = lambda i: (0, 0)
    full3 = lambda i: (0, 0, 0)
    return pl.pallas_call(
        _nsa_prompt_kernel,
        grid=(t // TQ,),
        in_specs=[
            pl.BlockSpec((TQ, BRANCH_W), lambda i: (i, 0)),
            pl.BlockSpec((TQ, LANE), lambda i: (i, 0)),
            pl.BlockSpec(kc.shape, full2),
            pl.BlockSpec(vct.shape, full2),
            pl.BlockSpec(indt.shape, full2),
            pl.BlockSpec(kslc.shape, full2),
            pl.BlockSpec(vslct.shape, full3),
            pl.BlockSpec(kwin.shape, full2),
            pl.BlockSpec(vwint.shape, full3),
        ],
        out_specs=pl.BlockSpec((TQ, BRANCH_W), lambda i: (i, 0)),
        out_shape=jax.ShapeDtypeStruct((t, BRANCH_W), F32),
        scratch_shapes=[
            pltpu.VMEM((NSA_KV, indt.shape[0], TQ), F32),
            pltpu.VMEM((1, NSA_HEADS * TQ), F32),
            pltpu.VMEM((NSA_KV, VT_ROWS, NSA_HPG * TQ), F32),
            pltpu.VMEM((NSA_KV, NSA_HD, NSA_HPG * TQ), F32),
        ],
        compiler_params=_cparams(("arbitrary",)),
        name="nsa_prompt",
    )(q, ng, kc, vct, indt, kslc, vslct, kwin, vwint)


def _nsa_prompt_path(q, kv_cmp, kv_slc, kv_win, ng, w_cmp1, pe_cmp, w_cmp2):
    t = q.shape[0]
    wk2, wv2 = _pack_cmp_w1(w_cmp1[0]), _pack_cmp_w1(w_cmp1[1])
    w2k, w2v = _pack_cmp_w2(w_cmp2[0]), _pack_cmp_w2(w_cmp2[1])
    bias = _cmp_bias(pe_cmp, w_cmp1)
    abk, abv, kslc, vslct, kwin, vwint = _nsa_prep(kv_cmp, kv_slc, kv_win, wk2, wv2, min(t, 1024))
    kc, vct = _cmp_finish(abk, abv, bias, w2k, w2v)
    indt = _cmp_to_slc_t(t // CMP_STRIDE, t // SLC_BLK)
    return _nsa_prompt(q, ng, kc, vct, indt, kslc, vslct, kwin, vwint)


def _gla_kernel(gq_ref, gk_ref, gv_ref, la_ref, s0_ref, o_ref, sfin_ref, st_ref, *, chunk, nchunk, exact):
    j = pl.program_id(1)
    n_pair = GLA_HEADS // 2
    nseq = gq_ref.shape[0]

    @pl.when(j == 0)
    def _():
        for s in range(nseq):
            for p in range(n_pair):
                st_ref[s, p] = s0_ref[s, p].T

    if exact:
        hi = lax.Precision.HIGHEST
        mx = lambda x: x
        dot = lambda a, b: jnp.dot(a, b, precision=hi, preferred_element_type=F32)
        dot_nt = lambda a, b: lax.dot_general(a, b, (((1,), (1,)), ((), ())), precision=hi,
                                              preferred_element_type=F32)
        dot_tn = lambda a, b: lax.dot_general(a, b, (((0,), (0,)), ((), ())), precision=hi,
                                              preferred_element_type=F32)
    else:
        mx = lambda x: x.astype(BF16)
        dot, dot_nt, dot_tn = _dot, _dot_nt, _dot_tn

    row = lax.broadcasted_iota(jnp.int32, (chunk, chunk), 0)
    col = lax.broadcasted_iota(jnp.int32, (chunk, chunk), 1)
    tri_f = (row >= col).astype(F32)
    row2 = lax.broadcasted_iota(jnp.int32, (2 * chunk, 2 * chunk), 0)
    col2 = lax.broadcasted_iota(jnp.int32, (2 * chunk, 2 * chunk), 1)
    tri2 = row2 % chunk >= col2 % chunk
    own_lanes = (lax.broadcasted_iota(jnp.int32, (2 * chunk, LANE), 1) // GLA_DK
                 == lax.broadcasted_iota(jnp.int32, (2 * chunk, LANE), 0) // chunk)

    def stack_heads(x):
        return mx(jnp.where(own_lanes, jnp.concatenate([x, x], axis=0), 0.0))

    for s in range(nseq):
        carry = []
        for ci in range(nchunk):
            rows = slice(ci * chunk, (ci + 1) * chunk)
            la = la_ref[s, rows, :]
            if exact:
                b = dot(tri_f, la)
            else:
                b = sum(_dot(tri_f.astype(BF16), part) for part in _split3(la))
            b_end = b[chunk - 1:chunk, :]
            k = gk_ref[s, rows, :]
            qe = gq_ref[s, rows, :] * jnp.exp(b)
            ke = k * jnp.exp(-b)
            kend = k * jnp.exp(b_end - b)
            per_pair = []
            for p in range(n_pair):
                sl = slice(p * LANE, (p + 1) * LANE)
                q2 = stack_heads(qe[:, sl])
                att = jnp.where(tri2, dot_nt(q2, stack_heads(ke[:, sl])), 0.0)
                v2 = mx(jnp.concatenate([gv_ref[s, rows, (2 * p + hh) * GLA_DV:(2 * p + hh + 1) * GLA_DV]
                                         for hh in range(2)], axis=0))
                o2 = dot(mx(att), v2)
                for hh in range(2):
                    o_ref[s, rows, (2 * p + hh) * GLA_DV:(2 * p + hh + 1) * GLA_DV] = o2[hh * chunk:(hh + 1) * chunk]
                per_pair.append((jnp.exp(b_end[:, sl]), dot_tn(v2, stack_heads(kend[:, sl])), q2))
            carry.append(per_pair)

        for p in range(n_pair):
            st = st_ref[s, p]
            for ci in range(nchunk):
                rows = slice(ci * chunk, (ci + 1) * chunk)
                decay, kv_sum, q2 = carry[ci][p]
                inter = dot_nt(q2, mx(st))
                for hh in range(2):
                    o_ref[s, rows, (2 * p + hh) * GLA_DV:(2 * p + hh + 1) * GLA_DV] += inter[hh * chunk:(hh + 1) * chunk]
                st = st * decay + kv_sum
            st_ref[s, p] = st

    @pl.when(j == pl.num_programs(1) - 1)
    def _():
        for s in range(nseq):
            for p in range(n_pair):
                sfin_ref[s, p] = st_ref[s, p].T


def _gla(gq, gk, gv, la, s0, chunk, nchunk, exact, sb):
    bsz, t, _ = gq.shape
    rb = chunk * nchunk
    row_spec = lambda w: pl.BlockSpec((sb, rb, w), lambda b, j: (b, j, 0))
    st_spec = pl.BlockSpec((sb, GLA_HEADS // 2, LANE, LANE), lambda b, j: (b, 0, 0, 0))
    return pl.pallas_call(
        functools.partial(_gla_kernel, chunk=chunk, nchunk=nchunk, exact=exact),
        grid=(bsz // sb, t // rb),
        in_specs=[row_spec(256), row_spec(256), row_spec(512), row_spec(256), st_spec],
        out_specs=[row_spec(512), st_spec],
        out_shape=[jax.ShapeDtypeStruct((bsz, t, BRANCH_W), F32),
                   jax.ShapeDtypeStruct((bsz, GLA_HEADS // 2, LANE, LANE), F32)],
        scratch_shapes=[pltpu.VMEM((sb, GLA_HEADS // 2, LANE, LANE), F32)],
        compiler_params=_cparams(("parallel", "arbitrary")),
        name="gla",
    )(gq, gk, gv, la, s0)


def _mem_proj_kernel(mem_ref, g_ref, w_ref, o_ref):
    o_ref[...] = _dot(_rms(mem_ref[...], g_ref[...]).astype(BF16), w_ref[...])


def _mem_proj(mem, g_mem, w_mem_kv):
    return pl.pallas_call(
        _mem_proj_kernel,
        out_shape=jax.ShapeDtypeStruct((mem.shape[0], w_mem_kv.shape[1]), F32),
        compiler_params=pltpu.CompilerParams(vmem_limit_bytes=VMEM_LIMIT),
        name="mem_proj",
    )(mem, g_mem, w_mem_kv)


def _xattn_kernel(xq_ref, mem_ref, o_ref):
    tm = xq_ref.shape[1]
    n_mem = mem_ref.shape[1] // (2 * X_HEADS)

    def mem_part(b, kv, h):
        return mem_ref[b, pl.ds(kv * X_HEADS + h, n_mem, stride=2 * X_HEADS), :].astype(BF16)

    if tm * X_HEADS <= LANE:
        lane_head = lax.broadcasted_iota(jnp.int32, (tm, BRANCH_W), 1) // X_HD
        for b in range(xq_ref.shape[0]):
            xq = xq_ref[b]
            qbd = jnp.concatenate([jnp.where(lane_head == h, xq, 0.0) for h in range(X_HEADS)], axis=0)
            k_all = jnp.concatenate([mem_part(b, 0, h) for h in range(X_HEADS)], axis=1)
            v_all = jnp.concatenate([mem_part(b, 1, h) for h in range(X_HEADS)], axis=1)
            s = _dot_nt(qbd.astype(BF16), k_all)
            p = jnp.exp(s - jnp.max(s, axis=-1, keepdims=True))
            p = p * (1.0 / jnp.sum(p, axis=-1, keepdims=True))
            o = _dot(p.astype(BF16), v_all)
            o_ref[b] = sum(jnp.where(lane_head == h, o[h * tm:(h + 1) * tm], 0.0) for h in range(X_HEADS))
        return
    for b in range(xq_ref.shape[0]):
        for h in range(X_HEADS):
            cs = slice(h * X_HD, (h + 1) * X_HD)
            s = _dot_nt(xq_ref[b, :, cs].astype(BF16), mem_part(b, 0, h))
            p = jnp.exp(s - jnp.max(s, axis=-1, keepdims=True))
            p = p * (1.0 / jnp.sum(p, axis=-1, keepdims=True))
            o_ref[b, :, cs] = _dot(p.astype(BF16), mem_part(b, 1, h))


def _xattn(xq, mem_kv, tm, sb):
    bsz, t, _ = xq.shape
    return pl.pallas_call(
        _xattn_kernel,
        grid=(bsz // sb, t // tm),
        in_specs=[pl.BlockSpec((sb, tm, BRANCH_W), lambda b, i: (b, i, 0)),
                  pl.BlockSpec((sb,) + mem_kv.shape[1:], lambda b, i: (b, 0, 0))],
        out_specs=pl.BlockSpec((sb, tm, BRANCH_W), lambda b, i: (b, i, 0)),
        out_shape=jax.ShapeDtypeStruct(xq.shape, F32),
        compiler_params=_cparams(("parallel", "parallel")),
        name="xattn",
    )(xq, mem_kv)


def _merge_kernel(u_ref, h_ref, onsa_ref, ogla_ref, gr_ref, ox_ref, ggla_ref, wb_ref, wmg_ref, bmg_ref, wout_ref,
                  gpost_ref, o_ref):
    u = u_ref[...]
    parts = []
    for h in range(GLA_HEADS):
        cs = slice(h * GLA_DV, (h + 1) * GLA_DV)
        parts.append(_rms(ogla_ref[:, cs], ggla_ref[...]) * _silu(gr_ref[:, cs]))
    branches = [onsa_ref[...], jnp.concatenate(parts, axis=1), ox_ref[...]]
    mixed = None
    for b in range(3):
        cs = slice(b * D_MODEL, (b + 1) * D_MODEL)
        gate = jax.nn.sigmoid(_dot(u, wmg_ref[:, cs]) + bmg_ref[:, cs])
        term = gate * _dot(branches[b].astype(BF16), wb_ref[b])
        mixed = term if mixed is None else mixed + term
    m = _dot(mixed.astype(BF16), wout_ref[...])
    o_ref[...] = h_ref[...] + _rms(m, gpost_ref[...])


def _merge(u, h, o_nsa, o_gla, gr, o_x, g_gla_out, w_branch, w_mg, b_mg, w_out, g_post, tm):
    rows = h.shape[0]
    row = lambda w: pl.BlockSpec((tm, w), lambda i: (i, 0))
    full = lambda a: pl.BlockSpec(a.shape, lambda i: (0,) * a.ndim)
    return pl.pallas_call(
        _merge_kernel,
        grid=(rows // tm,),
        in_specs=[row(D_MODEL), row(D_MODEL), row(BRANCH_W), row(BRANCH_W), row(BRANCH_W), row(BRANCH_W),
                  full(g_gla_out), full(w_branch), full(w_mg), full(b_mg), full(w_out), full(g_post)],
        out_specs=row(D_MODEL),
        out_shape=jax.ShapeDtypeStruct((rows, D_MODEL), F32),
        compiler_params=_cparams(("parallel",)),
        name="merge",
    )(u, h, o_nsa, o_gla, gr, o_x, g_gla_out, w_branch, w_mg, b_mg, w_out, g_post)


def _page_copies(pt_ref, pool_ref, buf_ref, sem_ref, seq, slot):
    npages = buf_ref.shape[-1] // PAGE_SIZE
    return [pltpu.make_async_copy(pool_ref.at[pt_ref[seq, p]],
                                  buf_ref.at[slot, :, :, :, pl.ds(p * PAGE_SIZE, PAGE_SIZE)], sem_ref.at[slot])
            for p in range(npages)]


def _gather_pages(pt_ref, pool_ref, buf_ref, sem_ref):
    b = pl.program_id(0)
    slot = b % 2

    @pl.when(b == 0)
    def _():
        for cp in _page_copies(pt_ref, pool_ref, buf_ref, sem_ref, 0, 0):
            cp.start()

    @pl.when(b + 1 < pl.num_programs(0))
    def _():
        for cp in _page_copies(pt_ref, pool_ref, buf_ref, sem_ref, b + 1, 1 - slot):
            cp.start()

    for cp in _page_copies(pt_ref, pool_ref, buf_ref, sem_ref, b, slot):
        cp.wait()
    return slot


def _row_token(shape):
    return lax.broadcasted_iota(jnp.int32, shape, 0) % 8


def _tap_permutation():
    r = np.arange(2 * PAGE_SIZE)
    l, jj = r // CMP_STRIDE, r % CMP_STRIDE
    perm = np.zeros((2 * PAGE_SIZE, 2 * PAGE_SIZE), np.float32)
    perm[r, CMP_STRIDE * jj + l] = 1.0
    return jnp.asarray(perm, dtype=BF16)


def _sample_cmp_kernel(pt_ref, pool_ref, qbd_ref, perm_ref, wk_ref, wv_ref, bias_ref, w2k_ref, w2v_ref, ind_ref,
                       oc_ref, imp_ref, buf_ref, sem_ref, taps_ref, shift_ref, *, past):
    slot = _gather_pages(pt_ref, pool_ref, buf_ref, sem_ref)
    span = 2 * PAGE_SIZE
    nspan = buf_ref.shape[-1] // span
    nsub = nspan * CMP_STRIDE

    perm = perm_ref[...]
    for q in range(nspan):
        x = buf_ref[slot, :, :, :, q * span:(q + 1) * span]
        t = _dot_nt(perm, x.reshape(KV_SLAB, span).astype(BF16)).astype(BF16)
        for l in range(CMP_STRIDE):
            taps_ref[l, q * CMP_STRIDE:(q + 1) * CMP_STRIDE, :] = t[l * CMP_STRIDE:(l + 1) * CMP_STRIDE, :]

    abk, abv = _compress_ab(lambda l: taps_ref[l], wk_ref, wv_ref)
    kc = _compress_finish(abk, bias_ref[0, 0:1, :], w2k_ref, shift_ref).astype(BF16)
    vc = _compress_finish(abv, bias_ref[1, 0:1, :], w2v_ref, shift_ref).astype(BF16)
    qbd = qbd_ref[0]
    s = _dot_nt(qbd, kc)
    q_pos = past + _row_token(s.shape)
    mask = lax.broadcasted_iota(jnp.int32, s.shape, 1) * CMP_STRIDE + (CMP_LEN - 1) <= q_pos
    s = jnp.where(mask, s, NEG)
    p = jnp.exp2(s - jnp.max(s, axis=-1, keepdims=True))
    pn = jnp.where(mask, p * (1.0 / jnp.sum(p, axis=-1, keepdims=True)), 0.0)
    oc_ref[0] = _dot(pn.astype(BF16), vc)
    rows_g = NSA_HPG * 8
    psum = jnp.concatenate(
        [sum(pn[g * rows_g + h * 8:g * rows_g + (h + 1) * 8, :] for h in range(NSA_HPG)) for g in range(NSA_KV)],
        axis=0)
    imp_ref[0] = sum(_dot(part, ind_ref[...]) for part in _split3(psum))


def _sample_cmp(page_table, pool_t, qbd, wk2, wv2, bias, w2k, w2v, ind):
    nseq, npages = page_table.shape
    assert npages % 2 == 0
    past = npages * PAGE_SIZE
    nsub = past // CMP_STRIDE
    perm = _tap_permutation()
    full = lambda a: pl.BlockSpec(a.shape, lambda b, pt: (0,) * a.ndim)
    grid_spec = pltpu.PrefetchScalarGridSpec(
        num_scalar_prefetch=1,
        grid=(nseq,),
        in_specs=[pl.BlockSpec(memory_space=pl.ANY),
                  pl.BlockSpec((1,) + qbd.shape[1:], lambda b, pt: (b, 0, 0)),
                  full(perm), full(wk2), full(wv2), full(bias), full(w2k), full(w2v), full(ind)],
        out_specs=[pl.BlockSpec((1, 64, LANE), lambda b, pt: (b, 0, 0)),
                   pl.BlockSpec((1, 16, ind.shape[1]), lambda b, pt: (b, 0, 0))],
        scratch_shapes=[pltpu.VMEM((2, 2, NSA_KV, NSA_HD, past), F32),
                        pltpu.SemaphoreType.DMA((2,)),
                        pltpu.VMEM((CMP_STRIDE, nsub, KV_SLAB), BF16),
                        pltpu.VMEM((nsub + 8, 2 * CMP_HID), F32)],
    )
    return pl.pallas_call(
        functools.partial(_sample_cmp_kernel, past=past),
        grid_spec=grid_spec,
        out_shape=[jax.ShapeDtypeStruct((nseq, 64, LANE), F32),
                   jax.ShapeDtypeStruct((nseq, 16, ind.shape[1]), F32)],
        compiler_params=_cparams(("arbitrary",)),
        name="sample_cmp",
    )(page_table, pool_t, qbd, perm, wk2, wv2, bias, w2k, w2v, ind)


def _sample_select_kernel(imp_ref, sel_ref, *, past, n_real, n_top):
    imp_t = imp_ref[...].T
    jj = lax.broadcasted_iota(jnp.int32, imp_t.shape, 0)
    tok = lax.broadcasted_iota(jnp.int32, (1, imp_t.shape[1]), 1) % 8
    cur = (past + tok) // SLC_BLK
    sel_ref[...] = _select_blocks(_block_scores(imp_t, jj, cur, n_real), n_top).T


def _sample_select(imp, past, n_real, tm):
    rows, nbp = imp.shape
    return pl.pallas_call(
        functools.partial(_sample_select_kernel, past=past, n_real=n_real, n_top=min(SLC_TOPN, n_real)),
        grid=(rows // tm,),
        in_specs=[pl.BlockSpec((tm, nbp), lambda i: (i, 0))],
        out_specs=pl.BlockSpec((tm, nbp), lambda i: (i, 0)),
        out_shape=jax.ShapeDtypeStruct((rows, nbp), F32),
        compiler_params=_cparams(("parallel",)),
        name="sample_select",
    )(imp)


def _biased_softmax_pv(parts):
    ms = [s + bias for s, bias, _, _ in parts]
    m = functools.reduce(jnp.maximum, [jnp.max(x, axis=-1, keepdims=True) for x in ms])
    ps = [jnp.exp2(x - m) for x in ms]
    l = sum(jnp.sum(p, axis=-1, keepdims=True) for p in ps)
    o = sum((_dot_nt if part[3] else _dot)(p.astype(BF16), part[2]) for p, part in zip(ps, parts))
    return o * (1.0 / l)


def _sample_attn_kernel(pt_ref, pool_ref, qbd_ref, sel_ref, expand_ref, newslc_ref, win_ref, newwin_ref,
                        newwint_ref, gate_ref, oc_ref, o_ref, wout_ref, buf_ref, sem_ref):
    slot = _gather_pages(pt_ref, pool_ref, buf_ref, sem_ref)
    npast = buf_ref.shape[-1]
    nblk_past = npast // SLC_BLK
    qbd = qbd_ref[0]
    rows = qbd.shape[0]
    tok = _row_token((rows, LANE))
    lane = lax.broadcasted_iota(jnp.int32, (rows, LANE), 1)

    kt = buf_ref[slot, 0].reshape(NSA_KV * NSA_HD, npast).astype(BF16)
    vt = buf_ref[slot, 1].reshape(NSA_KV * NSA_HD, npast).astype(BF16)
    sel = sel_ref[0]
    selrows = jnp.concatenate([sel[g * 8:(g + 1) * 8, :] for g in range(NSA_KV) for _ in range(NSA_HPG)], axis=0)
    selb = (selrows - 1.0) * (-NEG)
    bias_p = _dot(selb[:, :nblk_past].astype(BF16), expand_ref[...])
    bias_n = jnp.where(lane <= tok, selb[:, nblk_past:nblk_past + 1], NEG)
    knew = newslc_ref[0, :, 0:LANE].astype(BF16)
    vnew = newslc_ref[0, :, LANE:KV_SLAB].astype(BF16)
    o_slc = _biased_softmax_pv([(_dot(qbd, kt), bias_p, vt, True), (_dot_nt(qbd, knew), bias_n, vnew, False)])

    wbuf = win_ref.shape[-1]
    lane_w = lax.broadcasted_iota(jnp.int32, (rows, wbuf), 1)
    bias_w = jnp.where(lane_w + (WINDOW - wbuf) > _row_token((rows, wbuf)), 0.0, NEG)
    kwn = newwin_ref[0, :, 0:LANE].astype(BF16)
    vwn = newwin_ref[0, :, LANE:KV_SLAB].astype(BF16)
    o_win = _biased_softmax_pv([(_dot(qbd, win_ref[0, 0].astype(BF16)), bias_w, win_ref[0, 1].astype(BF16), True),
                                (_dot_nt(qbd, kwn), jnp.where(lane <= tok, 0.0, NEG), vwn, False)])

    g = gate_ref[0]
    o_ref[0] = g[:, 0:1] * oc_ref[0] + g[:, 1:2] * o_slc + g[:, 2:3] * o_win

    tail = lax.broadcasted_iota(jnp.int32, (NSA_KV * NSA_HD, LANE), 1) >= LANE - 8
    for kv in range(2):
        rolled = pltpu.roll(win_ref[0, kv], wbuf - 8, axis=1)
        wout_ref[0, kv, :, 0:wbuf - LANE] = rolled[:, 0:wbuf - LANE]
        wout_ref[0, kv, :, wbuf - LANE:wbuf] = jnp.where(tail, newwint_ref[0, kv], rolled[:, wbuf - LANE:wbuf])


def _sample_attn(page_table, pool_t, qbd, sel, expand, newslc, win_t, newwin, newwin_t, gates, o_cmp):
    nseq, npages = page_table.shape
    per_seq = lambda a: pl.BlockSpec((1,) + a.shape[1:], lambda b, pt: (b,) + (0,) * (a.ndim - 1))
    grid_spec = pltpu.PrefetchScalarGridSpec(
        num_scalar_prefetch=1,
        grid=(nseq,),
        in_specs=[pl.BlockSpec(memory_space=pl.ANY), per_seq(qbd), per_seq(sel),
                  pl.BlockSpec(expand.shape, lambda b, pt: (0, 0)),
                  per_seq(newslc), per_seq(win_t), per_seq(newwin), per_seq(newwin_t), per_seq(gates),
                  per_seq(o_cmp)],
        out_specs=[per_seq(o_cmp), per_seq(win_t)],
        scratch_shapes=[pltpu.VMEM((2, 2, NSA_KV, NSA_HD, npages * PAGE_SIZE), F32),
                        pltpu.SemaphoreType.DMA((2,))],
    )
    return pl.pallas_call(
        _sample_attn_kernel,
        grid_spec=grid_spec,
        out_shape=[jax.ShapeDtypeStruct(o_cmp.shape, F32), jax.ShapeDtypeStruct(win_t.shape, F32)],
        compiler_params=_cparams(("arbitrary",)),
        name="sample_attn",
    )(page_table, pool_t, qbd, sel, expand, newslc, win_t, newwin, newwin_t, gates, o_cmp)


def _nsa_sample_path(q, kv_cmp, kv_slc, kv_win, ng, pool_cmp, pool_slc, win_buf, page_table, w_cmp1, pe_cmp, w_cmp2):
    nseq, npages = page_table.shape
    pos_minor = lambda a: jnp.transpose(a, (0, 2, 3, 4, 1))
    wbuf = win_buf.shape[1]
    win_t = pos_minor(win_buf).reshape(nseq, 2, NSA_KV * NSA_HD, wbuf)
    tn = q.shape[0] // nseq
    past = npages * PAGE_SIZE
    n_real = -(-(past + tn) // SLC_BLK)
    nbp = -(-n_real // (2 * LANE)) * (2 * LANE)
    nsub = past // CMP_STRIDE
    assert tn == 8 and past % SLC_BLK == 0 and (past + tn - CMP_LEN) // CMP_STRIDE + 1 == nsub - 1

    qg = q.reshape(nseq, tn, NSA_KV, NSA_HPG, NSA_HD).transpose(0, 2, 3, 1, 4).reshape(nseq, NSA_KV, NSA_HPG * tn,
                                                                                        NSA_HD)
    z = jnp.zeros_like(qg[:, 0])
    qbd = jnp.stack([jnp.concatenate([qg[:, 0], z], axis=-1), jnp.concatenate([z, qg[:, 1]], axis=-1)], axis=1)
    qbd = (qbd.reshape(nseq, NSA_HEADS * tn, LANE) * LOG2E).astype(BF16)
    gates = ng[:, :_NG].reshape(nseq, tn, NSA_HEADS, 3).transpose(0, 2, 1, 3).reshape(nseq, NSA_HEADS * tn, 3)
    gates = jnp.pad(gates, ((0, 0), (0, 0), (0, LANE - 3)))

    def new_block(kv):
        return jnp.pad(kv.reshape(nseq, tn, KV_SLAB), ((0, 0), (0, LANE - tn), (0, 0)))

    wk2, wv2 = _pack_cmp_w1(w_cmp1[0]), _pack_cmp_w1(w_cmp1[1])
    w2k, w2v = _pack_cmp_w2(w_cmp2[0]), _pack_cmp_w2(w_cmp2[1])
    bias = _cmp_bias(pe_cmp, w_cmp1)
    ind = _cmp_to_slc_t(nsub, nbp).T
    o_cmp, imp = _sample_cmp(page_table, pos_minor(pool_cmp), qbd, wk2, wv2, bias, w2k, w2v, ind)
    sel = _sample_select(imp.reshape(nseq * 2 * tn, nbp), past, n_real, 512).reshape(nseq, 2 * tn, nbp)
    nblk_past = past // SLC_BLK
    expand = jnp.asarray(np.repeat(np.eye(nblk_past, dtype=np.float32), SLC_BLK, axis=1), dtype=BF16)
    newwin_t = jnp.pad(kv_win.reshape(nseq, tn, KV_SLAB).transpose(0, 2, 1), ((0, 0), (0, 0), (LANE - tn, 0)))
    newwin_t = newwin_t.reshape(nseq, 2, NSA_KV * NSA_HD, LANE)
    o, win_new_t = _sample_attn(page_table, pos_minor(pool_slc), qbd, sel, expand, new_block(kv_slc), win_t,
                                new_block(kv_win), newwin_t, gates, o_cmp)
    o = o.reshape(nseq, NSA_KV, NSA_HPG, tn, NSA_KV, NSA_HD)
    o = jnp.stack([o[:, g, :, :, g, :] for g in range(NSA_KV)], axis=1)
    win_new = jnp.transpose(win_new_t.reshape(nseq, 2, NSA_KV, NSA_HD, wbuf), (0, 4, 1, 2, 3))
    return o.transpose(0, 3, 1, 2, 4).reshape(nseq * tn, BRANCH_W), win_new


def _mixer(h, mem_kv, nsa_fn, gla_s0, gla_cfg, wts, tm, x_tm):
    bsz, t, _ = h.shape
    rows = bsz * t
    h2 = h.reshape(rows, D_MODEL)
    (u, q, kcmp, kslc, kwin, ng, gq, gk, gv, la, gr, xq) = _in_proj(h2, wts['g_pre_mix'], wts['w_cat'], wts['w2_pad'],
                                                                   wts['b_gla_g'], tm)
    o_nsa, extra = nsa_fn(q, kcmp, kslc, kwin, ng)
    r3 = lambda a: a.reshape(bsz, t, a.shape[-1])
    chunk, nchunk, exact, sb = gla_cfg
    o_gla, s_fin = _gla(r3(gq), r3(gk), r3(gv), r3(la), gla_s0.reshape(bsz, GLA_HEADS // 2, LANE, LANE), chunk,
                        nchunk, exact, sb)
    o_x = _xattn(r3(xq), mem_kv, x_tm, sb)
    h_new = _merge(u, h2, o_nsa, o_gla.reshape(rows, BRANCH_W), gr, o_x.reshape(rows, BRANCH_W), wts['g_gla_out'],
                   wts['w_branch'], wts['w_mg'], wts['b_mg'], wts['w_out'], wts['g_post_mix'], tm)
    s_fin = s_fin.reshape(bsz, GLA_HEADS, GLA_DK, GLA_DV)
    return h_new.reshape(bsz, t, D_MODEL), (kcmp, kslc, kwin), s_fin, extra


def kernel(x_prompt, x_sample, mem_prompt, cache_cmp_kv, cache_slc_kv, cache_win_kv, state_gla, cache_mem_kv,
           page_table, g_pre_ff1, g_post_ff1, w_ff1_gu, w_ff1_d, g_pre_mix, g_post_mix, w_in, w_cmp1, pe_cmp,
           w_cmp2, w_gla_g2, b_gla_g, g_gla_out, g_mem, w_mem_kv, w_branch, w_merge_gate, b_merge_gate, w_out,
           g_pre_ff2, g_post_ff2, w_ff2_gu, w_ff2_d):
    depth = w_in.shape[0]
    assert depth == 1
    l = 0
    bp, tp, _ = x_prompt.shape
    bs, ts, _ = x_sample.shape
    assert bp == 1
    row1 = lambda a: a.reshape(1, -1)
    wts = dict(
        g_pre_mix=row1(g_pre_mix[l]), w_cat=_pack_w_in(w_in[l]),
        w2_pad=jnp.zeros((LANE, GLA_HEADS * GLA_DK), F32).at[_NG:_NG + GLA_RANK].set(w_gla_g2[l]).astype(BF16),
        b_gla_g=row1(b_gla_g[l]), g_gla_out=row1(g_gla_out[l]), w_branch=w_branch[l].astype(BF16),
        w_mg=w_merge_gate[l].astype(BF16), b_mg=row1(b_merge_gate[l]), w_out=w_out[l].astype(BF16),
        g_post_mix=row1(g_post_mix[l]))

    def ffn(x, g_pre, g_post, w_gu, w_d, tm):
        b, t, _ = x.shape
        return _ffn_half(x.reshape(b * t, D_MODEL), row1(g_pre), row1(g_post), w_gu, w_d, tm).reshape(x.shape)

    w1gu, w1d = w_ff1_gu[l].astype(BF16), w_ff1_d[l].astype(BF16)
    w2gu, w2d = w_ff2_gu[l].astype(BF16), w_ff2_d[l].astype(BF16)
    tm_p, tm_s = 512, 256

    hp = ffn(x_prompt, g_pre_ff1[l], g_post_ff1[l], w1gu, w1d, tm_p)
    mem_kv = _mem_proj(mem_prompt[0], row1(g_mem[l]), w_mem_kv[l].astype(BF16))
    nsa_p = lambda q, kc, ks, kw, ng: (_nsa_prompt_path(q, kc, ks, kw, ng, w_cmp1[l], pe_cmp[l], w_cmp2[l]), None)
    s0 = jnp.zeros((bp, GLA_HEADS, GLA_DK, GLA_DV), F32)
    hp, (cmp_p, slc_p, win_p), gla_p, _ = _mixer(hp, mem_kv.reshape(1, -1, X_HD), nsa_p, s0, (2 * GLA_CHUNK, 4, False, 1), wts, tm_p, tm_p)
    yp = ffn(hp, g_pre_ff2[l], g_post_ff2[l], w2gu, w2d, tm_p)

    hs = ffn(x_sample, g_pre_ff1[l], g_post_ff1[l], w1gu, w1d, tm_s)
    wbuf = cache_win_kv.shape[2]
    nsa_s = lambda q, kc, ks, kw, ng: _nsa_sample_path(q, kc, ks, kw, ng, cache_cmp_kv[l], cache_slc_kv[l],
                                                       cache_win_kv[l], page_table, w_cmp1[l], pe_cmp[l], w_cmp2[l])
    mem_s = cache_mem_kv[l].reshape(bs, -1, X_HD)
    hs, (cmp_s, slc_s, _), gla_s, win_s = _mixer(hs, mem_s, nsa_s, state_gla[l], (min(GLA_CHUNK, ts), 1, True, 8), wts,
                                                 tm_s, ts)
    ys = ffn(hs, g_pre_ff2[l], g_post_ff2[l], w2gu, w2d, tm_s)

    kv5 = lambda a, b, t: a.reshape(1, b, t, 2, NSA_KV, NSA_HD)
    wp = min(WINDOW, tp)
    return (yp, ys,
            kv5(cmp_p, bp, tp), kv5(slc_p, bp, tp), kv5(win_p.reshape(bp, tp, KV_SLAB)[:, tp - wp:], bp, wp),
            gla_p[None], mem_kv.reshape(1, bp, mem_kv.shape[0], 2, X_HEADS, X_HD),
            kv5(cmp_s, bs, ts), kv5(slc_s, bs, ts), kv5(win_s, bs, wbuf), gla_s[None])
```

```python
import functools

import numpy as np
import jax
import jax.numpy as jnp
from jax import lax
from jax.experimental import pallas as pl
from jax.experimental.pallas import tpu as pltpu

F32 = jnp.float32
BF16 = jnp.bfloat16

D_MODEL = 1024
BRANCH_W = D_MODEL // 2
NSA_HEADS = 8
NSA_KV = 2
NSA_HPG = NSA_HEADS // NSA_KV
NSA_HD = BRANCH_W // NSA_HEADS
CMP_LEN = 32
CMP_STRIDE = 16
CMP_HID = 128
SLC_BLK = 64
SLC_TOPN = 16
WINDOW = 512
GLA_HEADS = 4
GLA_DV = BRANCH_W // GLA_HEADS
GLA_DK = GLA_DV // 2
GLA_RANK = 16
GLA_TAU = 16.0
GLA_CHUNK = 64
X_HEADS = 4
X_HD = BRANCH_W // X_HEADS
D_FF = 2816
EPS = 1e-6
NEG = -1e30
FORCE = 1e6
PAGE_SIZE = 128

LANE = 128
BF16_ROWS = 16
VT_ROWS = NSA_HD + BF16_ROWS
LOG2E = 1.4426950408889634
KV_SLAB = 2 * NSA_KV * NSA_HD
VMEM_LIMIT = 56 * 1024 * 1024


def _cparams(sem):
    return pltpu.CompilerParams(dimension_semantics=sem, vmem_limit_bytes=VMEM_LIMIT)


def _rms(x, g):
    xf = x.astype(F32)
    return xf * lax.rsqrt(jnp.mean(xf * xf, axis=-1, keepdims=True) + EPS) * g


def _silu(x):
    return x * jax.nn.sigmoid(x)


def _dot(a, b):
    return jnp.dot(a, b, preferred_element_type=F32)


def _dot_nt(a, b):
    return lax.dot_general(a, b, (((1,), (1,)), ((), ())), preferred_element_type=F32)


def _dot_tn(a, b):
    return lax.dot_general(a, b, (((0,), (0,)), ((), ())), preferred_element_type=F32)


def _split3(x):
    hi = x.astype(BF16)
    r1 = x - hi.astype(F32)
    mid = r1.astype(BF16)
    lo = (r1 - mid.astype(F32)).astype(BF16)
    return hi, mid, lo


def _ffn_kernel(x_ref, gpre_ref, gpost_ref, wg_ref, wu_ref, wd_ref, o_ref, xn_ref, acc_ref):
    f = pl.program_id(1)

    @pl.when(f == 0)
    def _():
        xn_ref[...] = _rms(x_ref[...], gpre_ref[...]).astype(BF16)
        acc_ref[...] = jnp.zeros_like(acc_ref)

    xn = xn_ref[...]
    a = _dot(xn, wg_ref[...])
    b = _dot(xn, wu_ref[...])
    acc_ref[...] += _dot((_silu(a) * b).astype(BF16), wd_ref[...])

    @pl.when(f == pl.num_programs(1) - 1)
    def _():
        o_ref[...] = x_ref[...] + 0.5 * _rms(acc_ref[...], gpost_ref[...])


def _ffn_half(x, g_pre, g_post, w_gu, w_d, tm):
    rows = x.shape[0]
    tf = D_FF // 2
    nf = D_FF // tf
    return pl.pallas_call(
        _ffn_kernel,
        grid=(rows // tm, nf),
        in_specs=[
            pl.BlockSpec((tm, D_MODEL), lambda i, f: (i, 0)),
            pl.BlockSpec((1, D_MODEL), lambda i, f: (0, 0)),
            pl.BlockSpec((1, D_MODEL), lambda i, f: (0, 0)),
            pl.BlockSpec((D_MODEL, tf), lambda i, f: (0, f)),
            pl.BlockSpec((D_MODEL, tf), lambda i, f: (0, nf + f)),
            pl.BlockSpec((tf, D_MODEL), lambda i, f: (f, 0)),
        ],
        out_specs=pl.BlockSpec((tm, D_MODEL), lambda i, f: (i, 0)),
        out_shape=jax.ShapeDtypeStruct((rows, D_MODEL), F32),
        scratch_shapes=[pltpu.VMEM((tm, D_MODEL), BF16), pltpu.VMEM((tm, D_MODEL), F32)],
        compiler_params=_cparams(("parallel", "arbitrary")),
        name="ffn_half",
    )(x, g_pre, g_post, w_gu, w_gu, w_d)


_NG = NSA_HEADS * 3
_C_Q, _C_CMP, _C_SLC, _C_WIN, _C_GQ, _C_GK, _C_GV, _C_GR, _C_XQ, _C_SM, _C_END = (
    0, 512, 768, 1024, 1280, 1536, 1792, 2304, 2816, 3328, 3456)


def _pack_w_in(w_in):
    offs = np.cumsum([0, NSA_HEADS * NSA_HD, 3 * KV_SLAB, _NG, GLA_HEADS * GLA_DK, GLA_HEADS * GLA_DK,
                      GLA_HEADS * GLA_DV, GLA_RANK, GLA_HEADS * GLA_DV, X_HEADS * X_HD])
    q, kv, ng, gq, gk, gv, glr, gr, xq = [w_in[:, offs[i]:offs[i + 1]] for i in range(9)]
    pad = jnp.zeros((D_MODEL, LANE - _NG - GLA_RANK), w_in.dtype)
    return jnp.concatenate([q, kv, gq, gk, gv, gr, xq, ng, glr, pad], axis=1).astype(BF16)


def _in_proj_kernel(h_ref, g_ref, w_ref, w2_ref, b2_ref,
                    u_ref, q_ref, cmp_ref, slc_ref, win_ref, ng_ref, gq_ref, gk_ref, gv_ref, la_ref, gr_ref,
                    xq_ref):
    u = _rms(h_ref[...], g_ref[...]).astype(BF16)
    u_ref[...] = u

    def proj(lo, hi):
        return _dot(u, w_ref[:, lo:hi])

    q_ref[...] = proj(_C_Q, _C_CMP) * (NSA_HD ** -0.5)
    cmp_ref[...] = proj(_C_CMP, _C_SLC)
    slc_ref[...] = proj(_C_SLC, _C_WIN)
    win_ref[...] = proj(_C_WIN, _C_GQ)
    gq_ref[...] = proj(_C_GQ, _C_GK) * (GLA_DK ** -0.5)
    gk_ref[...] = proj(_C_GK, _C_GV)
    gv_ref[...] = proj(_C_GV, _C_GR)
    gr_ref[...] = proj(_C_GR, _C_XQ)
    xq_ref[...] = proj(_C_XQ, _C_SM) * (X_HD ** -0.5)
    small = proj(_C_SM, _C_END)
    ng_ref[...] = jax.nn.sigmoid(small)
    z = _dot(small.astype(BF16), w2_ref[...]) + b2_ref[...]
    log_sig = jnp.minimum(z, 0.0) - jnp.log1p(jnp.exp(-jnp.abs(z)))
    la_ref[...] = log_sig / GLA_TAU


def _in_proj(h, g_pre_mix, w_cat, w2_pad, b2, tm):
    rows = h.shape[0]
    widths = [(D_MODEL, BF16), (512, F32), (256, F32), (256, F32), (256, F32), (LANE, F32), (256, F32),
              (256, F32), (512, F32), (256, F32), (512, F32), (512, F32)]
    return pl.pallas_call(
        _in_proj_kernel,
        grid=(rows // tm,),
        in_specs=[
            pl.BlockSpec((tm, D_MODEL), lambda i: (i, 0)),
            pl.BlockSpec((1, D_MODEL), lambda i: (0, 0)),
            pl.BlockSpec((D_MODEL, _C_END), lambda i: (0, 0)),
            pl.BlockSpec((LANE, 256), lambda i: (0, 0)),
            pl.BlockSpec((1, 256), lambda i: (0, 0)),
        ],
        out_specs=[pl.BlockSpec((tm, w), lambda i: (i, 0)) for w, _ in widths],
        out_shape=[jax.ShapeDtypeStruct((rows, w), dt) for w, dt in widths],
        compiler_params=_cparams(("parallel",)),
        name="in_proj",
    )(h, g_pre_mix, w_cat, w2_pad, b2)


def _pack_cmp_w1(w1):
    wa, wb = w1[:CMP_STRIDE], w1[CMP_STRIDE:]
    z = jnp.zeros_like(wa)
    rows = []
    for g in range(NSA_KV):
        cols = []
        for half in (wa, wb):
            for g2 in range(NSA_KV):
                cols.append(half if g2 == g else z)
        rows.append(jnp.concatenate(cols, axis=2))
    w = jnp.stack(rows, axis=1)
    return w.reshape(CMP_STRIDE // 2, 2 * NSA_KV * NSA_HD, 4 * CMP_HID).astype(BF16)


def _pack_cmp_w2(w2):
    z = jnp.zeros_like(w2)
    return jnp.concatenate([jnp.concatenate([w2, z], axis=1), jnp.concatenate([z, w2], axis=1)], axis=0).astype(BF16)


def _compress_ab(load_pos, wk_ref, wv_ref):
    hk = hv = None
    for lp in range(CMP_STRIDE // 2):
        xa = load_pos(2 * lp)
        xb = load_pos(2 * lp + 1)
        lk = jnp.concatenate([xa[:, :LANE], xb[:, :LANE]], axis=1).astype(BF16)
        lv = jnp.concatenate([xa[:, LANE:], xb[:, LANE:]], axis=1).astype(BF16)
        dk = _dot(lk, wk_ref[lp])
        dv = _dot(lv, wv_ref[lp])
        hk = dk if hk is None else hk + dk
        hv = dv if hv is None else hv + dv
    return hk, hv


def _compress_finish(ab, bias, w2_ref, shift_ref):
    n = ab.shape[0]
    shift_ref[0:n, :] = ab[:, 2 * CMP_HID:]
    shift_ref[n:n + 8, :] = jnp.zeros((8, 2 * CMP_HID), F32)
    h = ab[:, :2 * CMP_HID] + shift_ref[pl.ds(1, n), :] + bias
    return _dot(_silu(h).astype(BF16), w2_ref[...])


def _cmp_bias_kernel(pe_ref, w1_ref, o_ref):
    for kv in range(2):
        b = _dot(pe_ref[kv].astype(BF16), w1_ref[kv])
        o_ref[kv] = jnp.concatenate([b, b], axis=1)


def _cmp_bias(pe_cmp, w_cmp1):
    pe = jnp.broadcast_to(pe_cmp.reshape(2, 1, CMP_LEN * NSA_HD), (2, 8, CMP_LEN * NSA_HD))
    w1 = w_cmp1.reshape(2, CMP_LEN * NSA_HD, CMP_HID).astype(BF16)
    return pl.pallas_call(
        _cmp_bias_kernel,
        out_shape=jax.ShapeDtypeStruct((2, 8, 2 * CMP_HID), F32),
        name="cmp_bias",
    )(pe, w1)


def _nsa_prep_kernel(cmpv_ref, slc_ref, win_ref, wk_ref, wv_ref,
                     abk_ref, abv_ref, kslc_ref, vslct_ref, kwin_ref, vwint_ref):
    abk, abv = _compress_ab(lambda l: cmpv_ref[:, l * KV_SLAB:(l + 1) * KV_SLAB], wk_ref, wv_ref)
    abk_ref[...] = abk
    abv_ref[...] = abv
    nchunk = slc_ref.shape[0] // LANE
    ones_row = (lax.broadcasted_iota(jnp.int32, (VT_ROWS - NSA_HD, LANE), 0) == 0).astype(F32)
    for src, k_ref, vt_ref in ((slc_ref, kslc_ref, vslct_ref), (win_ref, kwin_ref, vwint_ref)):
        k_ref[...] = src[:, :LANE].astype(BF16)
        for c in range(nchunk):
            vt = src[c * LANE:(c + 1) * LANE, LANE:].T
            vt_ref[c] = jnp.concatenate([vt[:NSA_HD], ones_row, vt[NSA_HD:], ones_row], axis=0).astype(BF16)


def _nsa_prep(kv_cmp, kv_slc, kv_win, wk2, wv2, tm):
    t = kv_cmp.shape[0]
    nsub = t // CMP_STRIDE
    cmp_view = kv_cmp.reshape(nsub, CMP_STRIDE * KV_SLAB)
    const3 = lambda i: (0, 0, 0)
    return pl.pallas_call(
        _nsa_prep_kernel,
        grid=(t // tm,),
        in_specs=[
            pl.BlockSpec((tm // CMP_STRIDE, CMP_STRIDE * KV_SLAB), lambda i: (i, 0)),
            pl.BlockSpec((tm, KV_SLAB), lambda i: (i, 0)),
            pl.BlockSpec((tm, KV_SLAB), lambda i: (i, 0)),
            pl.BlockSpec(wk2.shape, const3),
            pl.BlockSpec(wv2.shape, const3),
        ],
        out_specs=[
            pl.BlockSpec((tm // CMP_STRIDE, 4 * CMP_HID), lambda i: (i, 0)),
            pl.BlockSpec((tm // CMP_STRIDE, 4 * CMP_HID), lambda i: (i, 0)),
            pl.BlockSpec((tm, LANE), lambda i: (i, 0)),
            pl.BlockSpec((tm // LANE, NSA_KV * VT_ROWS, LANE), lambda i: (i, 0, 0)),
            pl.BlockSpec((tm, LANE), lambda i: (i, 0)),
            pl.BlockSpec((tm // LANE, NSA_KV * VT_ROWS, LANE), lambda i: (i, 0, 0)),
        ],
        out_shape=[
            jax.ShapeDtypeStruct((nsub, 4 * CMP_HID), F32),
            jax.ShapeDtypeStruct((nsub, 4 * CMP_HID), F32),
            jax.ShapeDtypeStruct((t, LANE), BF16),
            jax.ShapeDtypeStruct((t // LANE, NSA_KV * VT_ROWS, LANE), BF16),
            jax.ShapeDtypeStruct((t, LANE), BF16),
            jax.ShapeDtypeStruct((t // LANE, NSA_KV * VT_ROWS, LANE), BF16),
        ],
        compiler_params=_cparams(("parallel",)),
        name="nsa_prep",
    )(cmp_view, kv_slc, kv_win, wk2, wv2)


def _cmp_finish_kernel(abk_ref, abv_ref, bias_ref, w2k_ref, w2v_ref, kc_ref, vct_ref, shift_ref):
    n = abk_ref.shape[0]
    kc_ref[...] = _compress_finish(abk_ref[...], bias_ref[0, 0:1, :], w2k_ref, shift_ref).astype(BF16)
    vc = _compress_finish(abv_ref[...], bias_ref[1, 0:1, :], w2v_ref, shift_ref)
    for c in range(n // LANE):
        vct_ref[:, c * LANE:(c + 1) * LANE] = vc[c * LANE:(c + 1) * LANE, :].T.astype(BF16)


def _cmp_finish(abk, abv, bias, w2k, w2v):
    n = abk.shape[0]
    return pl.pallas_call(
        _cmp_finish_kernel,
        out_shape=[jax.ShapeDtypeStruct((n, LANE), BF16), jax.ShapeDtypeStruct((LANE, n), BF16)],
        scratch_shapes=[pltpu.VMEM((n + 8, 2 * CMP_HID), F32)],
        compiler_params=pltpu.CompilerParams(vmem_limit_bytes=VMEM_LIMIT),
        name="cmp_finish",
    )(abk, abv, bias, w2k, w2v)


def _select_blocks(score, n_top):
    nb = score.shape[0]
    jj = lax.broadcasted_iota(jnp.int32, score.shape, 0)
    taken = jnp.float32(-3e38)
    for _ in range(n_top):
        m = jnp.max(score, axis=0, keepdims=True)
        idx = jnp.min(jnp.where(score == m, jj, nb), axis=0, keepdims=True)
        score = jnp.where(jj == idx, taken, score)
    return (score == taken).astype(F32)


def _block_scores(imp, jj, cur, n_real):
    s = jnp.where(jj == 0, FORCE, jnp.where(jj == cur, FORCE, jnp.where(jj == cur - 1, FORCE, imp)))
    s = jnp.where(jj <= cur, s, NEG)
    return jnp.where(jj < n_real, s, -jnp.inf)


def _cmp_to_slc_t(nc_pad, nb_pad):
    i = np.arange(nc_pad)[None, :]
    j = np.arange(nb_pad)[:, None]
    ov = (i * CMP_STRIDE < (j + 1) * SLC_BLK) & (i * CMP_STRIDE + CMP_LEN > j * SLC_BLK)
    return jnp.asarray(ov.astype(np.float32), dtype=BF16)


TQ = LANE


SLC_CK = 8 * TQ


def _flash_step(qbd, k, vt, bias, m_ref, acc_ref):
    s = _dot(k, qbd)
    for g in range(NSA_KV):
        ps = []
        for h in range(NSA_HPG):
            lo = (g * NSA_HPG + h) * TQ
            sc = s[:, lo:lo + TQ] + bias[g]
            m_old = m_ref[:, lo:lo + TQ]
            m_new = jnp.maximum(m_old, jnp.max(sc, axis=0, keepdims=True))
            ps.append(jnp.exp2((sc - m_new).astype(BF16)))
            m_ref[:, lo:lo + TQ] = m_new
            acc_ref[g, :, h * TQ:(h + 1) * TQ] = acc_ref[g, :, h * TQ:(h + 1) * TQ] * jnp.exp2(m_old - m_new)
        acc_ref[g] += _dot(vt[g * VT_ROWS:(g + 1) * VT_ROWS, :], jnp.concatenate(ps, axis=1))


def _softmax_once(qbd, k, vt, bias):
    s = _dot(k, qbd)
    outs = []
    for g in range(NSA_KV):
        ps = []
        for h in range(NSA_HPG):
            lo = (g * NSA_HPG + h) * TQ
            sc = s[:, lo:lo + TQ] + bias
            ps.append(jnp.exp2((sc - jnp.max(sc, axis=0, keepdims=True)).astype(BF16)))
        o = _dot(vt[g * VT_ROWS:(g + 1) * VT_ROWS, :], jnp.concatenate(ps, axis=1))
        outs.append(o[:NSA_HD] * (1.0 / o[NSA_HD:NSA_HD + 1]))
    return outs


def _flash_reset(m_ref, acc_ref):
    m_ref[...] = jnp.full(m_ref.shape, NEG, F32)
    acc_ref[...] = jnp.zeros(acc_ref.shape, F32)


def _flash_result(acc_ref):
    return [acc_ref[g, :NSA_HD, :] * (1.0 / acc_ref[g, NSA_HD:NSA_HD + 1, :]) for g in range(NSA_KV)]


def _nsa_prompt_kernel(q_ref, ng_ref, kc_ref, vct_ref, indt_ref, kslc_ref, vslct_ref, kwin_ref, vwint_ref,
                       o_ref, selb_ref, m_ref, acc_ref, ocmp_ref):
    i = pl.program_id(0)
    s0 = i * TQ
    ncp = kc_ref.shape[0]
    nbp = indt_ref.shape[0]
    n_blocks = kslc_ref.shape[0] // SLC_BLK

    qt = jnp.concatenate([q_ref[:, c * LANE:(c + 1) * LANE].T for c in range(BRANCH_W // LANE)], axis=0)
    zero = jnp.zeros((NSA_HD, NSA_HPG * TQ), F32)
    rows = []
    for g in range(NSA_KV):
        blk = jnp.concatenate(
            [qt[(g * NSA_HPG + h) * NSA_HD:(g * NSA_HPG + h + 1) * NSA_HD, :] for h in range(NSA_HPG)], axis=1)
        rows.append(jnp.concatenate([blk, zero] if g == 0 else [zero, blk], axis=1))
    qbd = (jnp.concatenate(rows, axis=0) * LOG2E).astype(BF16)

    q_lane = s0 + lax.broadcasted_iota(jnp.int32, (1, TQ), 1)

    n_win = WINDOW + TQ
    w0 = pl.multiple_of(jnp.maximum(s0 - WINDOW, 0), TQ)
    dist = q_lane - (w0 + lax.broadcasted_iota(jnp.int32, (n_win, TQ), 0))
    bias_w = jnp.where(dist >= 0, jnp.where(dist < WINDOW, 0.0, NEG), NEG)
    vt_w = jnp.concatenate([vwint_ref[w0 // TQ + j] for j in range(n_win // TQ)], axis=1)
    o_win = _softmax_once(qbd, kwin_ref[pl.ds(w0, n_win), :], vt_w, bias_w)

    col_ok = q_lane >= CMP_LEN - 1
    cur = q_lane // SLC_BLK

    def cmp_and_select(nb_eff):
        nc_eff = nb_eff * (SLC_BLK // CMP_STRIDE)
        c_end = lax.broadcasted_iota(jnp.int32, (nc_eff, TQ), 0) * CMP_STRIDE + (CMP_LEN - 1)
        bias_c = jnp.where(c_end <= q_lane, 0.0, NEG)
        jj = lax.broadcasted_iota(jnp.int32, (nb_eff, TQ), 0)
        scores = []
        for g in range(NSA_KV):
            sc_all = _dot(kc_ref[0:nc_eff, :], qbd[:, g * NSA_HPG * TQ:(g + 1) * NSA_HPG * TQ])
            psum = jnp.zeros((nc_eff, TQ), F32)
            ps = []
            for h in range(NSA_HPG):
                sc = sc_all[:, h * TQ:(h + 1) * TQ] + bias_c
                p = jnp.exp2(sc - jnp.max(sc, axis=0, keepdims=True))
                pn = p * jnp.where(col_ok, 1.0 / jnp.sum(p, axis=0, keepdims=True), 0.0)
                psum = psum + pn
                ps.append(pn.astype(BF16))
            ocmp_ref[g] = _dot(vct_ref[g * NSA_HD:(g + 1) * NSA_HD, 0:nc_eff], jnp.concatenate(ps, axis=1))
            imp = sum(_dot(indt_ref[0:nb_eff, 0:nc_eff], part) for part in _split3(psum))
            scores.append(_block_scores(imp, jj, cur, n_blocks))
        sel = _select_blocks(jnp.concatenate(scores, axis=1), min(SLC_TOPN, n_blocks))
        for g in range(NSA_KV):
            selb_ref[g, 0:nb_eff, :] = (sel[:, g * TQ:(g + 1) * TQ] - 1.0) * (-NEG)
            if nb_eff < nbp:
                selb_ref[g, nb_eff:nbp, :] = jnp.full((nbp - nb_eff, TQ), NEG, F32)

    sizes = [nbp]
    while len(sizes) < 3 and sizes[0] % 64 == 0:
        sizes.insert(0, sizes[0] // 2)
    lo = 0
    for nb_eff in sizes:
        hi = nb_eff // 2 if nb_eff < nbp else pl.num_programs(0)

        @pl.when((i >= lo) & (i < hi))
        def _(nb_eff=nb_eff):
            cmp_and_select(nb_eff)
        lo = hi
    o_cmp = [ocmp_ref[g] for g in range(NSA_KV)]

    blk_per_step = SLC_CK // SLC_BLK

    def slc_inputs(c, nkeys=SLC_CK):
        nblk = nkeys // SLC_BLK
        k = kslc_ref[pl.ds(pl.multiple_of(c * nkeys, nkeys), nkeys), :]
        vt = jnp.concatenate([vslct_ref[c * (nkeys // TQ) + j] for j in range(nkeys // TQ)], axis=1)
        bias = []
        for g in range(NSA_KV):
            rows = selb_ref[g, pl.ds(pl.multiple_of(c * nblk, nblk), nblk), :]
            bias.append(jnp.concatenate(
                [jnp.broadcast_to(rows[j:j + 1, :], (SLC_BLK, TQ)) for j in range(nblk)], axis=0))
        return k, vt, bias

    _flash_reset(m_ref, acc_ref)

    def slc_step(c):
        k, vt, bias = slc_inputs(c)
        _flash_step(qbd, k, vt, bias, m_ref, acc_ref)

    def slc_pair(t, carry):
        slc_step(2 * t)
        slc_step(2 * t + 1)
        return carry

    c_last = s0 // SLC_CK
    lax.fori_loop(0, c_last // 2, slc_pair, 0)

    @pl.when(c_last % 2 == 1)
    def _():
        slc_step(c_last - 1)
    half = SLC_CK // 2
    h_last = s0 // half

    @pl.when(h_last % 2 == 1)
    def _():
        k, vt, bias = slc_inputs(h_last - 1, half)
        _flash_step(qbd, k, vt, bias, m_ref, acc_ref)
    k, vt, bias = slc_inputs(h_last, half)
    key_pos = h_last * half + lax.broadcasted_iota(jnp.int32, (half, TQ), 0)
    causal = key_pos <= q_lane
    _flash_step(qbd, k, vt, [jnp.where(causal, b, NEG) for b in bias], m_ref, acc_ref)
    o_slc = _flash_result(acc_ref)

    ngt = ng_ref[...].T
    for g in range(NSA_KV):
        heads = []
        for h in range(NSA_HPG):
            r = (g * NSA_HPG + h) * 3
            cs = slice(h * TQ, (h + 1) * TQ)
            heads.append(ngt[r:r + 1, :] * o_cmp[g][:, cs] + ngt[r + 1:r + 2, :] * o_slc[g][:, cs]
                         + ngt[r + 2:r + 3, :] * o_win[g][:, cs])
        for pair in range(NSA_HPG // 2):
            blk = jnp.concatenate(heads[2 * pair:2 * pair + 2], axis=0).T
            col = (g * NSA_HPG // 2 + pair) * LANE
            o_ref[:, col:col + LANE] = blk


def _nsa_prompt(q, ng, kc, vct, indt, kslc, vslct, kwin, vwint):
    t = q.shape[0]
    full2 = lambda i: (0, 0)
    full3 = lambda i: (0, 0, 0)
    return pl.pallas_call(
        _nsa_prompt_kernel,
        grid=(t // TQ,),
        in_specs=[
            pl.BlockSpec((TQ, BRANCH_W), lambda i: (i, 0)),
            pl.BlockSpec((TQ, LANE), lambda i: (i, 0)),
            pl.BlockSpec(kc.shape, full2),
            pl.BlockSpec(vct.shape, full2),
            pl.BlockSpec(indt.shape, full2),
            pl.BlockSpec(kslc.shape, full2),
            pl.BlockSpec(vslct.shape, full3),
            pl.BlockSpec(kwin.shape, full2),
            pl.BlockSpec(vwint.shape, full3),
        ],
        out_specs=pl.BlockSpec((TQ, BRANCH_W), lambda i: (i, 0)),
        out_shape=jax.ShapeDtypeStruct((t, BRANCH_W), F32),
        scratch_shapes=[
            pltpu.VMEM((NSA_KV, indt.shape[0], TQ), F32),
            pltpu.VMEM((1, NSA_HEADS * TQ), F32),
            pltpu.VMEM((NSA_KV, VT_ROWS, NSA_HPG * TQ), F32),
            pltpu.VMEM((NSA_KV, NSA_HD, NSA_HPG * TQ), F32),
        ],
        compiler_params=_cparams(("arbitrary",)),
        name="nsa_prompt",
    )(q, ng, kc, vct, indt, kslc, vslct, kwin, vwint)


def _nsa_prompt_path(q, kv_cmp, kv_slc, kv_win, ng, w_cmp1, pe_cmp, w_cmp2):
    t = q.shape[0]
    wk2, wv2 = _pack_cmp_w1(w_cmp1[0]), _pack_cmp_w1(w_cmp1[1])
    w2k, w2v = _pack_cmp_w2(w_cmp2[0]), _pack_cmp_w2(w_cmp2[1])
    bias = _cmp_bias(pe_cmp, w_cmp1)
    abk, abv, kslc, vslct, kwin, vwint = _nsa_prep(kv_cmp, kv_slc, kv_win, wk2, wv2, min(t, 1024))
    kc, vct = _cmp_finish(abk, abv, bias, w2k, w2v)
    indt = _cmp_to_slc_t(t // CMP_STRIDE, t // SLC_BLK)
    return _nsa_prompt(q, ng, kc, vct, indt, kslc, vslct, kwin, vwint)


def _gla_kernel(gq_ref, gk_ref, gv_ref, la_ref, s0_ref, o_ref, sfin_ref, st_ref, *, chunk, nchunk, exact):
    j = pl.program_id(1)
    n_pair = GLA_HEADS // 2
    nseq = gq_ref.shape[0]

    @pl.when(j == 0)
    def _():
        for s in range(nseq):
            for p in range(n_pair):
                st_ref[s, p] = s0_ref[s, p].T

    if exact:
        hi = lax.Precision.HIGHEST
        mx = lambda x: x
        dot = lambda a, b: jnp.dot(a, b, precision=hi, preferred_element_type=F32)
        dot_nt = lambda a, b: lax.dot_general(a, b, (((1,), (1,)), ((), ())), precision=hi,
                                              preferred_element_type=F32)
        dot_tn = lambda a, b: lax.dot_general(a, b, (((0,), (0,)), ((), ())), precision=hi,
                                              preferred_element_type=F32)
    else:
        mx = lambda x: x.astype(BF16)
        dot, dot_nt, dot_tn = _dot, _dot_nt, _dot_tn

    row = lax.broadcasted_iota(jnp.int32, (chunk, chunk), 0)
    col = lax.broadcasted_iota(jnp.int32, (chunk, chunk), 1)
    tri_f = (row >= col).astype(F32)
    row2 = lax.broadcasted_iota(jnp.int32, (2 * chunk, 2 * chunk), 0)
    col2 = lax.broadcasted_iota(jnp.int32, (2 * chunk, 2 * chunk), 1)
    tri2 = row2 % chunk >= col2 % chunk
    own_lanes = (lax.broadcasted_iota(jnp.int32, (2 * chunk, LANE), 1) // GLA_DK
                 == lax.broadcasted_iota(jnp.int32, (2 * chunk, LANE), 0) // chunk)

    def stack_heads(x):
        return mx(jnp.where(own_lanes, jnp.concatenate([x, x], axis=0), 0.0))

    for s in range(nseq):
        carry = []
        for ci in range(nchunk):
            rows = slice(ci * chunk, (ci + 1) * chunk)
            la = la_ref[s, rows, :]
            if exact:
                b = dot(tri_f, la)
            else:
                b = sum(_dot(tri_f.astype(BF16), part) for part in _split3(la))
            b_end = b[chunk - 1:chunk, :]
            k = gk_ref[s, rows, :]
            qe = gq_ref[s, rows, :] * jnp.exp(b)
            ke = k * jnp.exp(-b)
            kend = k * jnp.exp(b_end - b)
            per_pair = []
            for p in range(n_pair):
                sl = slice(p * LANE, (p + 1) * LANE)
                q2 = stack_heads(qe[:, sl])
                att = jnp.where(tri2, dot_nt(q2, stack_heads(ke[:, sl])), 0.0)
                v2 = mx(jnp.concatenate([gv_ref[s, rows, (2 * p + hh) * GLA_DV:(2 * p + hh + 1) * GLA_DV]
                                         for hh in range(2)], axis=0))
                o2 = dot(mx(att), v2)
                for hh in range(2):
                    o_ref[s, rows, (2 * p + hh) * GLA_DV:(2 * p + hh + 1) * GLA_DV] = o2[hh * chunk:(hh + 1) * chunk]
                per_pair.append((jnp.exp(b_end[:, sl]), dot_tn(v2, stack_heads(kend[:, sl])), q2))
            carry.append(per_pair)

        for p in range(n_pair):
            st = st_ref[s, p]
            for ci in range(nchunk):
                rows = slice(ci * chunk, (ci + 1) * chunk)
                decay, kv_sum, q2 = carry[ci][p]
                inter = dot_nt(q2, mx(st))
                for hh in range(2):
                    o_ref[s, rows, (2 * p + hh) * GLA_DV:(2 * p + hh + 1) * GLA_DV] += inter[hh * chunk:(hh + 1) * chunk]
                st = st * decay + kv_sum
            st_ref[s, p] = st

    @pl.when(j == pl.num_programs(1) - 1)
    def _():
        for s in range(nseq):
            for p in range(n_pair):
                sfin_ref[s, p] = st_ref[s, p].T


def _gla(gq, gk, gv, la, s0, chunk, nchunk, exact, sb):
    bsz, t, _ = gq.shape
    rb = chunk * nchunk
    row_spec = lambda w: pl.BlockSpec((sb, rb, w), lambda b, j: (b, j, 0))
    st_spec = pl.BlockSpec((sb, GLA_HEADS // 2, LANE, LANE), lambda b, j: (b, 0, 0, 0))
    return pl.pallas_call(
        functools.partial(_gla_kernel, chunk=chunk, nchunk=nchunk, exact=exact),
        grid=(bsz // sb, t // rb),
        in_specs=[row_spec(256), row_spec(256), row_spec(512), row_spec(256), st_spec],
        out_specs=[row_spec(512), st_spec],
        out_shape=[jax.ShapeDtypeStruct((bsz, t, BRANCH_W), F32),
                   jax.ShapeDtypeStruct((bsz, GLA_HEADS // 2, LANE, LANE), F32)],
        scratch_shapes=[pltpu.VMEM((sb, GLA_HEADS // 2, LANE, LANE), F32)],
        compiler_params=_cparams(("parallel", "arbitrary")),
        name="gla",
    )(gq, gk, gv, la, s0)


def _mem_proj_kernel(mem_ref, g_ref, w_ref, o_ref):
    o_ref[...] = _dot(_rms(mem_ref[...], g_ref[...]).astype(BF16), w_ref[...])


def _mem_proj(mem, g_mem, w_mem_kv):
    return pl.pallas_call(
        _mem_proj_kernel,
        out_shape=jax.ShapeDtypeStruct((mem.shape[0], w_mem_kv.shape[1]), F32),
        compiler_params=pltpu.CompilerParams(vmem_limit_bytes=VMEM_LIMIT),
        name="mem_proj",
    )(mem, g_mem, w_mem_kv)


def _xattn_kernel(xq_ref, mem_ref, o_ref):
    tm = xq_ref.shape[1]
    n_mem = mem_ref.shape[1] // (2 * X_HEADS)

    def mem_part(b, kv, h):
        return mem_ref[b, pl.ds(kv * X_HEADS + h, n_mem, stride=2 * X_HEADS), :].astype(BF16)

    if tm * X_HEADS <= LANE:
        lane_head = lax.broadcasted_iota(jnp.int32, (tm, BRANCH_W), 1) // X_HD
        for b in range(xq_ref.shape[0]):
            xq = xq_ref[b]
            qbd = jnp.concatenate([jnp.where(lane_head == h, xq, 0.0) for h in range(X_HEADS)], axis=0)
            k_all = jnp.concatenate([mem_part(b, 0, h) for h in range(X_HEADS)], axis=1)
            v_all = jnp.concatenate([mem_part(b, 1, h) for h in range(X_HEADS)], axis=1)
            s = _dot_nt(qbd.astype(BF16), k_all)
            p = jnp.exp(s - jnp.max(s, axis=-1, keepdims=True))
            p = p * (1.0 / jnp.sum(p, axis=-1, keepdims=True))
            o = _dot(p.astype(BF16), v_all)
            o_ref[b] = sum(jnp.where(lane_head == h, o[h * tm:(h + 1) * tm], 0.0) for h in range(X_HEADS))
        return
    for b in range(xq_ref.shape[0]):
        for h in range(X_HEADS):
            cs = slice(h * X_HD, (h + 1) * X_HD)
            s = _dot_nt(xq_ref[b, :, cs].astype(BF16), mem_part(b, 0, h))
            p = jnp.exp(s - jnp.max(s, axis=-1, keepdims=True))
            p = p * (1.0 / jnp.sum(p, axis=-1, keepdims=True))
            o_ref[b, :, cs] = _dot(p.astype(BF16), mem_part(b, 1, h))


def _xattn(xq, mem_kv, tm, sb):
    bsz, t, _ = xq.shape
    return pl.pallas_call(
        _xattn_kernel,
        grid=(bsz // sb, t // tm),
        in_specs=[pl.BlockSpec((sb, tm, BRANCH_W), lambda b, i: (b, i, 0)),
                  pl.BlockSpec((sb,) + mem_kv.shape[1:], lambda b, i: (b, 0, 0))],
        out_specs=pl.BlockSpec((sb, tm, BRANCH_W), lambda b, i: (b, i, 0)),
        out_shape=jax.ShapeDtypeStruct(xq.shape, F32),
        compiler_params=_cparams(("parallel", "parallel")),
        name="xattn",
    )(xq, mem_kv)


def _merge_kernel(u_ref, h_ref, onsa_ref, ogla_ref, gr_ref, ox_ref, ggla_ref, wb_ref, wmg_ref, bmg_ref, wout_ref,
                  gpost_ref, o_ref):
    u = u_ref[...]
    parts = []
    for h in range(GLA_HEADS):
        cs = slice(h * GLA_DV, (h + 1) * GLA_DV)
        parts.append(_rms(ogla_ref[:, cs], ggla_ref[...]) * _silu(gr_ref[:, cs]))
    branches = [onsa_ref[...], jnp.concatenate(parts, axis=1), ox_ref[...]]
    mixed = None
    for b in range(3):
        cs = slice(b * D_MODEL, (b + 1) * D_MODEL)
        gate = jax.nn.sigmoid(_dot(u, wmg_ref[:, cs]) + bmg_ref[:, cs])
        term = gate * _dot(branches[b].astype(BF16), wb_ref[b])
        mixed = term if mixed is None else mixed + term
    m = _dot(mixed.astype(BF16), wout_ref[...])
    o_ref[...] = h_ref[...] + _rms(m, gpost_ref[...])


def _merge(u, h, o_nsa, o_gla, gr, o_x, g_gla_out, w_branch, w_mg, b_mg, w_out, g_post, tm):
    rows = h.shape[0]
    row = lambda w: pl.BlockSpec((tm, w), lambda i: (i, 0))
    full = lambda a: pl.BlockSpec(a.shape, lambda i: (0,) * a.ndim)
    return pl.pallas_call(
        _merge_kernel,
        grid=(rows // tm,),
        in_specs=[row(D_MODEL), row(D_MODEL), row(BRANCH_W), row(BRANCH_W), row(BRANCH_W), row(BRANCH_W),
                  full(g_gla_out), full(w_branch), full(w_mg), full(b_mg), full(w_out), full(g_post)],
        out_specs=row(D_MODEL),
        out_shape=jax.ShapeDtypeStruct((rows, D_MODEL), F32),
        compiler_params=_cparams(("parallel",)),
        name="merge",
    )(u, h, o_nsa, o_gla, gr, o_x, g_gla_out, w_branch, w_mg, b_mg, w_out, g_post)


def _page_copies(pt_ref, pool_ref, buf_ref, sem_ref, seq, slot):
    npages = buf_ref.shape[-1] // PAGE_SIZE
    return [pltpu.make_async_copy(pool_ref.at[pt_ref[seq, p]],
                                  buf_ref.at[slot, :, :, :, pl.ds(p * PAGE_SIZE, PAGE_SIZE)], sem_ref.at[slot])
            for p in range(npages)]


def _gather_pages(pt_ref, pool_ref, buf_ref, sem_ref):
    b = pl.program_id(0)
    slot = b % 2

    @pl.when(b == 0)
    def _():
        for cp in _page_copies(pt_ref, pool_ref, buf_ref, sem_ref, 0, 0):
            cp.start()

    @pl.when(b + 1 < pl.num_programs(0))
    def _():
        for cp in _page_copies(pt_ref, pool_ref, buf_ref, sem_ref, b + 1, 1 - slot):
            cp.start()

    for cp in _page_copies(pt_ref, pool_ref, buf_ref, sem_ref, b, slot):
        cp.wait()
    return slot


def _row_token(shape):
    return lax.broadcasted_iota(jnp.int32, shape, 0) % 8


def _tap_permutation():
    r = np.arange(2 * PAGE_SIZE)
    l, jj = r // CMP_STRIDE, r % CMP_STRIDE
    perm = np.zeros((2 * PAGE_SIZE, 2 * PAGE_SIZE), np.float32)
    perm[r, CMP_STRIDE * jj + l] = 1.0
    return jnp.asarray(perm, dtype=BF16)


def _sample_cmp_kernel(pt_ref, pool_ref, qbd_ref, perm_ref, wk_ref, wv_ref, bias_ref, w2k_ref, w2v_ref, ind_ref,
                       oc_ref, imp_ref, buf_ref, sem_ref, taps_ref, shift_ref, *, past):
    slot = _gather_pages(pt_ref, pool_ref, buf_ref, sem_ref)
    span = 2 * PAGE_SIZE
    nspan = buf_ref.shape[-1] // span
    nsub = nspan * CMP_STRIDE

    perm = perm_ref[...]
    for q in range(nspan):
        x = buf_ref[slot, :, :, :, q * span:(q + 1) * span]
        t = _dot_nt(perm, x.reshape(KV_SLAB, span).astype(BF16)).astype(BF16)
        for l in range(CMP_STRIDE):
            taps_ref[l, q * CMP_STRIDE:(q + 1) * CMP_STRIDE, :] = t[l * CMP_STRIDE:(l + 1) * CMP_STRIDE, :]

    abk, abv = _compress_ab(lambda l: taps_ref[l], wk_ref, wv_ref)
    kc = _compress_finish(abk, bias_ref[0, 0:1, :], w2k_ref, shift_ref).astype(BF16)
    vc = _compress_finish(abv, bias_ref[1, 0:1, :], w2v_ref, shift_ref).astype(BF16)
    qbd = qbd_ref[0]
    s = _dot_nt(qbd, kc)
    q_pos = past + _row_token(s.shape)
    mask = lax.broadcasted_iota(jnp.int32, s.shape, 1) * CMP_STRIDE + (CMP_LEN - 1) <= q_pos
    s = jnp.where(mask, s, NEG)
    p = jnp.exp2(s - jnp.max(s, axis=-1, keepdims=True))
    pn = jnp.where(mask, p * (1.0 / jnp.sum(p, axis=-1, keepdims=True)), 0.0)
    oc_ref[0] = _dot(pn.astype(BF16), vc)
    rows_g = NSA_HPG * 8
    psum = jnp.concatenate(
        [sum(pn[g * rows_g + h * 8:g * rows_g + (h + 1) * 8, :] for h in range(NSA_HPG)) for g in range(NSA_KV)],
        axis=0)
    imp_ref[0] = sum(_dot(part, ind_ref[...]) for part in _split3(psum))


def _sample_cmp(page_table, pool_t, qbd, wk2, wv2, bias, w2k, w2v, ind):
    nseq, npages = page_table.shape
    assert npages % 2 == 0
    past = npages * PAGE_SIZE
    nsub = past // CMP_STRIDE
    perm = _tap_permutation()
    full = lambda a: pl.BlockSpec(a.shape, lambda b, pt: (0,) * a.ndim)
    grid_spec = pltpu.PrefetchScalarGridSpec(
        num_scalar_prefetch=1,
        grid=(nseq,),
        in_specs=[pl.BlockSpec(memory_space=pl.ANY),
                  pl.BlockSpec((1,) + qbd.shape[1:], lambda b, pt: (b, 0, 0)),
                  full(perm), full(wk2), full(wv2), full(bias), full(w2k), full(w2v), full(ind)],
        out_specs=[pl.BlockSpec((1, 64, LANE), lambda b, pt: (b, 0, 0)),
                   pl.BlockSpec((1, 16, ind.shape[1]), lambda b, pt: (b, 0, 0))],
        scratch_shapes=[pltpu.VMEM((2, 2, NSA_KV, NSA_HD, past), F32),
                        pltpu.SemaphoreType.DMA((2,)),
                        pltpu.VMEM((CMP_STRIDE, nsub, KV_SLAB), BF16),
                        pltpu.VMEM((nsub + 8, 2 * CMP_HID), F32)],
    )
    return pl.pallas_call(
        functools.partial(_sample_cmp_kernel, past=past),
        grid_spec=grid_spec,
        out_shape=[jax.ShapeDtypeStruct((nseq, 64, LANE), F32),
                   jax.ShapeDtypeStruct((nseq, 16, ind.shape[1]), F32)],
        compiler_params=_cparams(("arbitrary",)),
        name="sample_cmp",
    )(page_table, pool_t, qbd, perm, wk2, wv2, bias, w2k, w2v, ind)


def _sample_select_kernel(imp_ref, sel_ref, *, past, n_real, n_top):
    imp_t = imp_ref[...].T
    jj = lax.broadcasted_iota(jnp.int32, imp_t.shape, 0)
    tok = lax.broadcasted_iota(jnp.int32, (1, imp_t.shape[1]), 1) % 8
    cur = (past + tok) // SLC_BLK
    sel_ref[...] = _select_blocks(_block_scores(imp_t, jj, cur, n_real), n_top).T


def _sample_select(imp, past, n_real, tm):
    rows, nbp = imp.shape
    return pl.pallas_call(
        functools.partial(_sample_select_kernel, past=past, n_real=n_real, n_top=min(SLC_TOPN, n_real)),
        grid=(rows // tm,),
        in_specs=[pl.BlockSpec((tm, nbp), lambda i: (i, 0))],
        out_specs=pl.BlockSpec((tm, nbp), lambda i: (i, 0)),
        out_shape=jax.ShapeDtypeStruct((rows, nbp), F32),
        compiler_params=_cparams(("parallel",)),
        name="sample_select",
    )(imp)


def _biased_softmax_pv(parts):
    ms = [s + bias for s, bias, _, _ in parts]
    m = functools.reduce(jnp.maximum, [jnp.max(x, axis=-1, keepdims=True) for x in ms])
    ps = [jnp.exp2(x - m) for x in ms]
    l = sum(jnp.sum(p, axis=-1, keepdims=True) for p in ps)
    o = sum((_dot_nt if part[3] else _dot)(p.astype(BF16), part[2]) for p, part in zip(ps, parts))
    return o * (1.0 / l)


def _sample_attn_kernel(pt_ref, pool_ref, qbd_ref, sel_ref, expand_ref, newslc_ref, win_ref, newwin_ref,
                        newwint_ref, gate_ref, oc_ref, o_ref, wout_ref, buf_ref, sem_ref):
    slot = _gather_pages(pt_ref, pool_ref, buf_ref, sem_ref)
    npast = buf_ref.shape[-1]
    nblk_past = npast // SLC_BLK
    qbd = qbd_ref[0]
    rows = qbd.shape[0]
    tok = _row_token((rows, LANE))
    lane = lax.broadcasted_iota(jnp.int32, (rows, LANE), 1)

    kt = buf_ref[slot, 0].reshape(NSA_KV * NSA_HD, npast).astype(BF16)
    vt = buf_ref[slot, 1].reshape(NSA_KV * NSA_HD, npast).astype(BF16)
    sel = sel_ref[0]
    selrows = jnp.concatenate([sel[g * 8:(g + 1) * 8, :] for g in range(NSA_KV) for _ in range(NSA_HPG)], axis=0)
    selb = (selrows - 1.0) * (-NEG)
    bias_p = _dot(selb[:, :nblk_past].astype(BF16), expand_ref[...])
    bias_n = jnp.where(lane <= tok, selb[:, nblk_past:nblk_past + 1], NEG)
    knew = newslc_ref[0, :, 0:LANE].astype(BF16)
    vnew = newslc_ref[0, :, LANE:KV_SLAB].astype(BF16)
    o_slc = _biased_softmax_pv([(_dot(qbd, kt), bias_p, vt, True), (_dot_nt(qbd, knew), bias_n, vnew, False)])

    wbuf = win_ref.shape[-1]
    lane_w = lax.broadcasted_iota(jnp.int32, (rows, wbuf), 1)
    bias_w = jnp.where(lane_w + (WINDOW - wbuf) > _row_token((rows, wbuf)), 0.0, NEG)
    kwn = newwin_ref[0, :, 0:LANE].astype(BF16)
    vwn = newwin_ref[0, :, LANE:KV_SLAB].astype(BF16)
    o_win = _biased_softmax_pv([(_dot(qbd, win_ref[0, 0].astype(BF16)), bias_w, win_ref[0, 1].astype(BF16), True),
                                (_dot_nt(qbd, kwn), jnp.where(lane <= tok, 0.0, NEG), vwn, False)])

    g = gate_ref[0]
    o_ref[0] = g[:, 0:1] * oc_ref[0] + g[:, 1:2] * o_slc + g[:, 2:3] * o_win

    tail = lax.broadcasted_iota(jnp.int32, (NSA_KV * NSA_HD, LANE), 1) >= LANE - 8
    for kv in range(2):
        rolled = pltpu.roll(win_ref[0, kv], wbuf - 8, axis=1)
        wout_ref[0, kv, :, 0:wbuf - LANE] = rolled[:, 0:wbuf - LANE]
        wout_ref[0, kv, :, wbuf - LANE:wbuf] = jnp.where(tail, newwint_ref[0, kv], rolled[:, wbuf - LANE:wbuf])


def _sample_attn(page_table, pool_t, qbd, sel, expand, newslc, win_t, newwin, newwin_t, gates, o_cmp):
    nseq, npages = page_table.shape
    per_seq = lambda a: pl.BlockSpec((1,) + a.shape[1:], lambda b, pt: (b,) + (0,) * (a.ndim - 1))
    grid_spec = pltpu.PrefetchScalarGridSpec(
        num_scalar_prefetch=1,
        grid=(nseq,),
        in_specs=[pl.BlockSpec(memory_space=pl.ANY), per_seq(qbd), per_seq(sel),
                  pl.BlockSpec(expand.shape, lambda b, pt: (0, 0)),
                  per_seq(newslc), per_seq(win_t), per_seq(newwin), per_seq(newwin_t), per_seq(gates),
                  per_seq(o_cmp)],
        out_specs=[per_seq(o_cmp), per_seq(win_t)],
        scratch_shapes=[pltpu.VMEM((2, 2, NSA_KV, NSA_HD, npages * PAGE_SIZE), F32),
                        pltpu.SemaphoreType.DMA((2,))],
    )
    return pl.pallas_call(
        _sample_attn_kernel,
        grid_spec=grid_spec,
        out_shape=[jax.ShapeDtypeStruct(o_cmp.shape, F32), jax.ShapeDtypeStruct(win_t.shape, F32)],
        compiler_params=_cparams(("arbitrary",)),
        name="sample_attn",
    )(page_table, pool_t, qbd, sel, expand, newslc, win_t, newwin, newwin_t, gates, o_cmp)


def _nsa_sample_path(q, kv_cmp, kv_slc, kv_win, ng, pool_cmp, pool_slc, win_buf, page_table, w_cmp1, pe_cmp, w_cmp2):
    nseq, npages = page_table.shape
    pos_minor = lambda a: jnp.transpose(a, (0, 2, 3, 4, 1))
    wbuf = win_buf.shape[1]
    win_t = pos_minor(win_buf).reshape(nseq, 2, NSA_KV * NSA_HD, wbuf)
    tn = q.shape[0] // nseq
    past = npages * PAGE_SIZE
    n_real = -(-(past + tn) // SLC_BLK)
    nbp = -(-n_real // (2 * LANE)) * (2 * LANE)
    nsub = past // CMP_STRIDE
    assert tn == 8 and past % SLC_BLK == 0 and (past + tn - CMP_LEN) // CMP_STRIDE + 1 == nsub - 1

    qg = q.reshape(nseq, tn, NSA_KV, NSA_HPG, NSA_HD).transpose(0, 2, 3, 1, 4).reshape(nseq, NSA_KV, NSA_HPG * tn,
                                                                                        NSA_HD)
    z = jnp.zeros_like(qg[:, 0])
    qbd = jnp.stack([jnp.concatenate([qg[:, 0], z], axis=-1), jnp.concatenate([z, qg[:, 1]], axis=-1)], axis=1)
    qbd = (qbd.reshape(nseq, NSA_HEADS * tn, LANE) * LOG2E).astype(BF16)
    gates = ng[:, :_NG].reshape(nseq, tn, NSA_HEADS, 3).transpose(0, 2, 1, 3).reshape(nseq, NSA_HEADS * tn, 3)
    gates = jnp.pad(gates, ((0, 0), (0, 0), (0, LANE - 3)))

    def new_block(kv):
        return jnp.pad(kv.reshape(nseq, tn, KV_SLAB), ((0, 0), (0, LANE - tn), (0, 0)))

    wk2, wv2 = _pack_cmp_w1(w_cmp1[0]), _pack_cmp_w1(w_cmp1[1])
    w2k, w2v = _pack_cmp_w2(w_cmp2[0]), _pack_cmp_w2(w_cmp2[1])
    bias = _cmp_bias(pe_cmp, w_cmp1)
    ind = _cmp_to_slc_t(nsub, nbp).T
    o_cmp, imp = _sample_cmp(page_table, pos_minor(pool_cmp), qbd, wk2, wv2, bias, w2k, w2v, ind)
    sel = _sample_select(imp.reshape(nseq * 2 * tn, nbp), past, n_real, 512).reshape(nseq, 2 * tn, nbp)
    nblk_past = past // SLC_BLK
    expand = jnp.asarray(np.repeat(np.eye(nblk_past, dtype=np.float32), SLC_BLK, axis=1), dtype=BF16)
    newwin_t = jnp.pad(kv_win.reshape(nseq, tn, KV_SLAB).transpose(0, 2, 1), ((0, 0), (0, 0), (LANE - tn, 0)))
    newwin_t = newwin_t.reshape(nseq, 2, NSA_KV * NSA_HD, LANE)
    o, win_new_t = _sample_attn(page_table, pos_minor(pool_slc), qbd, sel, expand, new_block(kv_slc), win_t,
                                new_block(kv_win), newwin_t, gates, o_cmp)
    o = o.reshape(nseq, NSA_KV, NSA_HPG, tn, NSA_KV, NSA_HD)
    o = jnp.stack([o[:, g, :, :, g, :] for g in range(NSA_KV)], axis=1)
    win_new = jnp.transpose(win_new_t.reshape(nseq, 2, NSA_KV, NSA_HD, wbuf), (0, 4, 1, 2, 3))
    return o.transpose(0, 3, 1, 2, 4).reshape(nseq * tn, BRANCH_W), win_new


def _mixer(h, mem_kv, nsa_fn, gla_s0, gla_cfg, wts, tm, x_tm):
    bsz, t, _ = h.shape
    rows = bsz * t
    h2 = h.reshape(rows, D_MODEL)
    (u, q, kcmp, kslc, kwin, ng, gq, gk, gv, la, gr, xq) = _in_proj(h2, wts['g_pre_mix'], wts['w_cat'], wts['w2_pad'],
                                                                   wts['b_gla_g'], tm)
    o_nsa, extra = nsa_fn(q, kcmp, kslc, kwin, ng)
    r3 = lambda a: a.reshape(bsz, t, a.shape[-1])
    chunk, nchunk, exact, sb = gla_cfg
    o_gla, s_fin = _gla(r3(gq), r3(gk), r3(gv), r3(la), gla_s0.reshape(bsz, GLA_HEADS // 2, LANE, LANE), chunk,
                        nchunk, exact, sb)
    o_x = _xattn(r3(xq), mem_kv, x_tm, sb)
    h_new = _merge(u, h2, o_nsa, o_gla.reshape(rows, BRANCH_W), gr, o_x.reshape(rows, BRANCH_W), wts['g_gla_out'],
                   wts['w_branch'], wts['w_mg'], wts['b_mg'], wts['w_out'], wts['g_post_mix'], tm)
    s_fin = s_fin.reshape(bsz, GLA_HEADS, GLA_DK, GLA_DV)
    return h_new.reshape(bsz, t, D_MODEL), (kcmp, kslc, kwin), s_fin, extra


def kernel(x_prompt, x_sample, mem_prompt, cache_cmp_kv, cache_slc_kv, cache_win_kv, state_gla, cache_mem_kv,
           page_table, g_pre_ff1, g_post_ff1, w_ff1_gu, w_ff1_d, g_pre_mix, g_post_mix, w_in, w_cmp1, pe_cmp,
           w_cmp2, w_gla_g2, b_gla_g, g_gla_out, g_mem, w_mem_kv, w_branch, w_merge_gate, b_merge_gate, w_out,
           g_pre_ff2, g_post_ff2, w_ff2_gu, w_ff2_d):
    depth = w_in.shape[0]
    assert depth == 1
    l = 0
    bp, tp, _ = x_prompt.shape
    bs, ts, _ = x_sample.shape
    assert bp == 1
    row1 = lambda a: a.reshape(1, -1)
    wts = dict(
        g_pre_mix=row1(g_pre_mix[l]), w_cat=_pack_w_in(w_in[l]),
        w2_pad=jnp.zeros((LANE, GLA_HEADS * GLA_DK), F32).at[_NG:_NG + GLA_RANK].set(w_gla_g2[l]).astype(BF16),
        b_gla_g=row1(b_gla_g[l]), g_gla_out=row1(g_gla_out[l]), w_branch=w_branch[l].astype(BF16),
        w_mg=w_merge_gate[l].astype(BF16), b_mg=row1(b_merge_gate[l]), w_out=w_out[l].astype(BF16),
        g_post_mix=row1(g_post_mix[l]))

    def ffn(x, g_pre, g_post, w_gu, w_d, tm):
        b, t, _ = x.shape
        return _ffn_half(x.reshape(b * t, D_MODEL), row1(g_pre), row1(g_post), w_gu, w_d, tm).reshape(x.shape)

    w1gu, w1d = w_ff1_gu[l].astype(BF16), w_ff1_d[l].astype(BF16)
    w2gu, w2d = w_ff2_gu[l].astype(BF16), w_ff2_d[l].astype(BF16)
    tm_p, tm_s = 512, 256

    hp = ffn(x_prompt, g_pre_ff1[l], g_post_ff1[l], w1gu, w1d, tm_p)
    mem_kv = _mem_proj(mem_prompt[0], row1(g_mem[l]), w_mem_kv[l].astype(BF16))
    nsa_p = lambda q, kc, ks, kw, ng: (_nsa_prompt_path(q, kc, ks, kw, ng, w_cmp1[l], pe_cmp[l], w_cmp2[l]), None)
    s0 = jnp.zeros((bp, GLA_HEADS, GLA_DK, GLA_DV), F32)
    hp, (cmp_p, slc_p, win_p), gla_p, _ = _mixer(hp, mem_kv.reshape(1, -1, X_HD), nsa_p, s0, (2 * GLA_CHUNK, 4, False, 1), wts, tm_p, tm_p)
    yp = ffn(hp, g_pre_ff2[l], g_post_ff2[l], w2gu, w2d, tm_p)

    hs = ffn(x_sample, g_pre_ff1[l], g_post_ff1[l], w1gu, w1d, tm_s)
    wbuf = cache_win_kv.shape[2]
    nsa_s = lambda q, kc, ks, kw, ng: _nsa_sample_path(q, kc, ks, kw, ng, cache_cmp_kv[l], cache_slc_kv[l],
                                                       cache_win_kv[l], page_table, w_cmp1[l], pe_cmp[l], w_cmp2[l])
    mem_s = cache_mem_kv[l].reshape(bs, -1, X_HD)
    hs, (cmp_s, slc_s, _), gla_s, win_s = _mixer(hs, mem_s, nsa_s, state_gla[l], (min(GLA_CHUNK, ts), 1, True, 8), wts,
                                                 tm_s, ts)
    ys = ffn(hs, g_pre_ff2[l], g_post_ff2[l], w2gu, w2d, tm_s)

    kv5 = lambda a, b, t: a.reshape(1, b, t, 2, NSA_KV, NSA_HD)
    wp = min(WINDOW, tp)
    return (yp, ys,
            kv5(cmp_p, bp, tp), kv5(slc_p, bp, tp), kv5(win_p.reshape(bp, tp, KV_SLAB)[:, tp - wp:], bp, wp),
            gla_p[None], mem_kv.reshape(1, bp, mem_kv.shape[0], 2, X_HEADS, X_HD),
            kv5(cmp_s, bs, ts), kv5(slc_s, bs, ts), kv5(win_s, bs, wbuf), gla_s[None])
```
